```python
import jax, jax.numpy as jnp
from jax import lax
import numpy as np

D_MODEL = 1024
BATCH = 2
SEQ = 8192
DEPTH = 2

GRID_W = 64
CTX_LEN = 256
HEAD_DIM = 128
N_Q_HEADS = 4
N_KV_HEADS = 2
ATTN_WIDTH = N_Q_HEADS * HEAD_DIM
KV_WIDTH = N_KV_HEADS * HEAD_DIM
CONV_WIDTH = D_MODEL - ATTN_WIDTH
SHORT_CONV_K = 3
IN_PROJ_WIDTH = ATTN_WIDTH + 2 * KV_WIDTH + 3 * CONV_WIDTH
PROJ_SPLITS = (ATTN_WIDTH,
               ATTN_WIDTH + KV_WIDTH,
               ATTN_WIDTH + 2 * KV_WIDTH,
               ATTN_WIDTH + 2 * KV_WIDTH + CONV_WIDTH,
               ATTN_WIDTH + 2 * KV_WIDTH + 2 * CONV_WIDTH)
Q_BLOCK = 128
ROPE_THETA = 10000.0
POOL_WINDOWS = (2, 4, 8, 16)
N_POOL_GROUPS = len(POOL_WINDOWS)
POOL_GROUP = D_MODEL // N_POOL_GROUPS
D_FF = 2816
FFN_CONV_K = 3
N_MOD = 6
EPS = 1e-6
N_EVEN = (DEPTH + 1) // 2
N_ODD = DEPTH // 2

kernel_name = "hybrid_gqa_shortconv_pool_convglu_dit"


def rms_norm(x, gain):
    xf = x.astype(jnp.float32)
    y = xf * lax.rsqrt(jnp.mean(xf * xf, axis=-1, keepdims=True) + EPS)
    return (y * gain.astype(jnp.float32)).astype(x.dtype)


def modulate(h, shift, scale):
    return h * (1.0 + scale) + shift


def dwconv_centred(h, w):
    k = w.shape[0]
    p = k // 2
    n = h.shape[1]
    hp = jnp.pad(h, ((0, 0), (p, p), (0, 0)))
    return sum(hp[:, j:j + n] * w[j] for j in range(k))


def axial_rope_tables(rows):
    row_ids = jnp.repeat(jnp.arange(rows), GRID_W).astype(jnp.float32)
    col_ids = jnp.tile(jnp.arange(GRID_W), rows).astype(jnp.float32)
    axis_dim = HEAD_DIM // 2
    inv_freq = jnp.power(ROPE_THETA, -jnp.arange(0, axis_dim, 2, dtype=jnp.float32) / axis_dim)
    ang = jnp.stack([row_ids[:, None] * inv_freq, col_ids[:, None] * inv_freq], axis=1)
    return jnp.cos(ang), jnp.sin(ang)


def apply_axial_rope(x, cos, sin):
    b, n, h, d = x.shape
    xr = x.reshape(b, n, h, 2, 2, d // 4)
    x1, x2 = xr[..., 0, :], xr[..., 1, :]
    c = cos[None, :, None].astype(x.dtype)
    s = sin[None, :, None].astype(x.dtype)
    out = jnp.stack([x1 * c - x2 * s, x2 * c + x1 * s], axis=-2)
    return out.reshape(b, n, h, d)


def gqa_softmax(qg, k, v):
    scores = jnp.einsum('bqkgd,btkd->bkgqt', qg, k).astype(jnp.float32) * (HEAD_DIM ** -0.5)
    p = jax.nn.softmax(scores, axis=-1).astype(v.dtype)
    return jnp.einsum('bkgqt,btkd->bqkgd', p, v)


def latent_attention(q, k_lat, v_lat, k_ctx, v_ctx):
    b, n, hq, hd = q.shape
    g = hq // N_KV_HEADS
    k_all = jnp.concatenate([k_ctx, k_lat], axis=1)
    v_all = jnp.concatenate([v_ctx, v_lat], axis=1)
    nblk = n // Q_BLOCK
    qb = q.reshape(b, nblk, Q_BLOCK, N_KV_HEADS, g, hd).transpose(1, 0, 2, 3, 4, 5)
    out = lax.map(lambda qblk: gqa_softmax(qblk, k_all, v_all), qb)
    return out.transpose(1, 0, 2, 3, 4, 5).reshape(b, n, hq * hd)


def context_attention(q, k, v):
    b, n, hq, hd = q.shape
    qg = q.reshape(b, n, N_KV_HEADS, hq // N_KV_HEADS, hd)
    return gqa_softmax(qg, k, v).reshape(b, n, hq * hd)


def split_projection(p, q_gain, k_gain):
    b, n, _ = p.shape
    q, k, v, gate_b, gate_c, x_in = jnp.split(p, PROJ_SPLITS, axis=-1)
    q = rms_norm(q.reshape(b, n, N_Q_HEADS, HEAD_DIM), q_gain)
    k = rms_norm(k.reshape(b, n, N_KV_HEADS, HEAD_DIM), k_gain)
    v = v.reshape(b, n, N_KV_HEADS, HEAD_DIM)
    return q, k, v, gate_b, gate_c, x_in


def even_mixer(a_lat, a_ctx, cos, sin, w_in, q_gain, k_gain, conv_w, w_out, with_ctx_out):
    b, lc, _ = a_ctx.shape
    if with_ctx_out:
        cq, ck, cv, cgb, cgc, cxin = split_projection(a_ctx @ w_in, q_gain, k_gain)
    else:
        ckv = a_ctx @ w_in[:, ATTN_WIDTH:ATTN_WIDTH + 2 * KV_WIDTH]
        ck, cv = jnp.split(ckv, 2, axis=-1)
        ck = rms_norm(ck.reshape(b, lc, N_KV_HEADS, HEAD_DIM), k_gain)
        cv = cv.reshape(b, lc, N_KV_HEADS, HEAD_DIM)
    q, k, v, gate_b, gate_c, x_in = split_projection(a_lat @ w_in, q_gain, k_gain)
    q = apply_axial_rope(q, cos, sin)
    k = apply_axial_rope(k, cos, sin)
    attn = latent_attention(q, k, v, ck, cv)
    conv = gate_b * dwconv_centred(gate_c * x_in, conv_w)
    y_lat = jnp.concatenate([attn, conv], axis=-1) @ w_out
    y_ctx = None
    if with_ctx_out:
        c_attn = context_attention(cq, ck, cv)
        c_conv = cgb * dwconv_centred(cgc * cxin, conv_w)
        y_ctx = jnp.concatenate([c_attn, c_conv], axis=-1) @ w_out
    return y_lat, y_ctx


def pool_mixer(h, pool_w, pool_scale):
    b, n, d = h.shape
    hf = h.astype(jnp.float32)
    cs = jnp.pad(jnp.cumsum(hf, axis=1), ((0, 0), (1, 0), (0, 0)))
    t = jnp.arange(n)
    groups = []
    for gi, w in enumerate(POOL_WINDOWS):
        lo = jnp.clip(t - w // 2, 0, n)
        hi = jnp.clip(t + w - w // 2, 0, n)
        sl = slice(gi * POOL_GROUP, (gi + 1) * POOL_GROUP)
        csg = cs[:, :, sl]
        mean = (csg[:, hi] - csg[:, lo]) / (hi - lo).astype(jnp.float32)[None, :, None]
        groups.append(mean - hf[:, :, sl])
    pooled = jnp.stack(groups, axis=2).astype(h.dtype)
    mixed = jnp.einsum('bngc,gce->bnge', pooled, pool_w).reshape(b, n, d)
    return mixed * pool_scale


def conv_glu(h, w_up, conv_w, conv_b, w_down):
    gate, val = jnp.split(h @ w_up, 2, axis=-1)
    gate = dwconv_centred(gate, conv_w) + conv_b
    return (jax.nn.silu(gate) * val) @ w_down


def setup_inputs(seed: int = 0) -> dict:
    key = jax.random.key(seed)
    ks = jax.random.split(key, 24)
    f32 = jnp.float32
    nrm = lambda k, shape, s: jax.random.normal(k, shape, f32) * s
    d = D_MODEL
    return {
        "x": nrm(ks[0], (BATCH, SEQ, d), 1.0),
        "c": nrm(ks[1], (BATCH, d), 1.0),
        "ctx": nrm(ks[2], (BATCH, CTX_LEN, d), 1.0),
        "c_ctx": nrm(ks[3], (d,), 1.0),
        "ada_w": nrm(ks[4], (DEPTH, d, N_MOD * d), 0.5 * d ** -0.5),
        "ada_b": nrm(ks[5], (DEPTH, N_MOD * d), 0.02),
        "mix_norm": 1.0 + nrm(ks[6], (DEPTH, d), 0.02),
        "ffn_norm": 1.0 + nrm(ks[7], (DEPTH, d), 0.02),
        "even_w_in": nrm(ks[8], (N_EVEN, d, IN_PROJ_WIDTH), d ** -0.5),
        "even_q_gain": 1.0 + nrm(ks[9], (N_EVEN, HEAD_DIM), 0.02),
        "even_k_gain": 1.0 + nrm(ks[10], (N_EVEN, HEAD_DIM), 0.02),
        "even_conv_w": nrm(ks[11], (N_EVEN, SHORT_CONV_K, CONV_WIDTH), SHORT_CONV_K ** -0.5),
        "even_w_out": nrm(ks[12], (N_EVEN, ATTN_WIDTH + CONV_WIDTH, d), (ATTN_WIDTH + CONV_WIDTH) ** -0.5),
        "odd_pool_w": nrm(ks[13], (N_ODD, N_POOL_GROUPS, POOL_GROUP, POOL_GROUP), POOL_GROUP ** -0.5),
        "odd_pool_scale": 1.0 + nrm(ks[14], (N_ODD, d), 0.02),
        "ffn_w_up": nrm(ks[15], (DEPTH, d, 2 * D_FF), d ** -0.5),
        "ffn_conv_w": nrm(ks[16], (DEPTH, FFN_CONV_K, D_FF), FFN_CONV_K ** -0.5),
        "ffn_conv_b": nrm(ks[17], (DEPTH, D_FF), 0.02),
        "ffn_w_down": nrm(ks[18], (DEPTH, D_FF, d), D_FF ** -0.5),
    }


def reference(x, c, ctx, c_ctx, ada_w, ada_b, mix_norm, ffn_norm, even_w_in, even_q_gain, even_k_gain,
              even_conv_w, even_w_out, odd_pool_w, odd_pool_scale, ffn_w_up, ffn_conv_w, ffn_conv_b, ffn_w_down):
    b, n, d = x.shape
    rows = n // GRID_W
    cos, sin = axial_rope_tables(rows)
    silu_c = jax.nn.silu(c)
    silu_cc = jax.nn.silu(c_ctx)
    h_ctx = ctx
    for l in range(DEPTH):
        ctx_later = any(j % 2 == 0 for j in range(l + 1, DEPTH))
        need_ctx = (l % 2 == 0) or ctx_later
        mod_lat = (silu_c @ ada_w[l] + ada_b[l])[:, None, :]
        sh1, sc1, g1, sh2, sc2, g2 = jnp.split(mod_lat, N_MOD, axis=-1)
        a_lat = modulate(rms_norm(x, mix_norm[l]), sh1, sc1)
        if need_ctx:
            mod_ctx = (silu_cc @ ada_w[l] + ada_b[l])[None, None, :]
            csh1, csc1, cg1, csh2, csc2, cg2 = jnp.split(mod_ctx, N_MOD, axis=-1)
            a_ctx = modulate(rms_norm(h_ctx, mix_norm[l]), csh1, csc1)
        if l % 2 == 0:
            e = l // 2
            y_lat, y_ctx = even_mixer(a_lat, a_ctx, cos, sin, even_w_in[e], even_q_gain[e], even_k_gain[e],
                                      even_conv_w[e], even_w_out[e], ctx_later)
        else:
            o = l // 2
            y_lat = pool_mixer(a_lat, odd_pool_w[o], odd_pool_scale[o])
            y_ctx = pool_mixer(a_ctx, odd_pool_w[o], odd_pool_scale[o]) if ctx_later else None
        x = x + g1 * y_lat
        f_lat = modulate(rms_norm(x, ffn_norm[l]), sh2, sc2)
        x = x + g2 * conv_glu(f_lat, ffn_w_up[l], ffn_conv_w[l], ffn_conv_b[l], ffn_w_down[l])
        if ctx_later:
            h_ctx = h_ctx + cg1 * y_ctx
            f_ctx = modulate(rms_norm(h_ctx, ffn_norm[l]), csh2, csc2)
            h_ctx = h_ctx + cg2 * conv_glu(f_ctx, ffn_w_up[l], ffn_conv_w[l], ffn_conv_b[l], ffn_w_down[l])
    return x
```

```python
import functools

import jax
import jax.numpy as jnp
from jax import lax
from jax.experimental import pallas as pl
from jax.experimental.pallas import tpu as pltpu

D_MODEL = 1024
GRID_W = 64
HEAD_DIM = 128
N_Q_HEADS = 4
N_KV_HEADS = 2
Q_PER_KV = N_Q_HEADS // N_KV_HEADS
ATTN_WIDTH = N_Q_HEADS * HEAD_DIM
KV_WIDTH = N_KV_HEADS * HEAD_DIM
CONV_WIDTH = D_MODEL - ATTN_WIDTH
IN_PROJ_WIDTH = ATTN_WIDTH + 2 * KV_WIDTH + 3 * CONV_WIDTH
ROPE_THETA = 10000.0
POOL_WINDOWS = (2, 4, 8, 16)
POOL_GROUP = D_MODEL // len(POOL_WINDOWS)
D_FF = 2816
N_MOD = 6
EPS = 1e-6

SUBLANES = 8
LANES = 128
VMEM_LIMIT_BYTES = 56 * 1024 * 1024

HALO = SUBLANES
POOL_HALO = 2 * SUBLANES
TOKEN_TILE = 512
Q_TILE = 512
KV_TILE = 1024
FF_CHUNK = 256
ADALN_COLS = 1536

_BF16 = jnp.bfloat16
_F32 = jnp.float32


def _compiler_params(n_axes):
    return pltpu.CompilerParams(dimension_semantics=("arbitrary",) * n_axes,
                                vmem_limit_bytes=VMEM_LIMIT_BYTES)


def _resident(shape, index_map):
    return pl.BlockSpec(shape, index_map, pipeline_mode=pl.Buffered(1))


def _norm_modulate(x, gain, shift, scale):
    y = x * lax.rsqrt(jnp.mean(x * x, axis=-1, keepdims=True) + EPS)
    return (y * gain) * (1.0 + scale) + shift


def _silu(x):
    return x * (1.0 / (1.0 + jnp.exp(-x)))


def _shift_rows(u, k):
    return pltpu.roll(u, k % u.shape[0], axis=0)


def _adaln_kernel(ct_ref, w_ref, b_ref, o_ref):
    s = _silu(ct_ref[...])
    w = w_ref[0]
    n_vec = 3
    rows = [jnp.sum(s[:, r:r + 1] * w, axis=0, keepdims=True) + b_ref[0] for r in range(n_vec)]
    rows.append(jnp.zeros((SUBLANES - n_vec, w.shape[1]), _F32))
    o_ref[0] = jnp.concatenate(rows, axis=0)


def _adaln(cond_t, ada_w, ada_b):
    depth, d, width = ada_w.shape
    return pl.pallas_call(
        _adaln_kernel,
        grid=(depth, width // ADALN_COLS),
        in_specs=[
            pl.BlockSpec((d, SUBLANES), lambda l, j: (0, 0)),
            pl.BlockSpec((1, d, ADALN_COLS), lambda l, j: (l, 0, j)),
            pl.BlockSpec((1, 1, ADALN_COLS), lambda l, j: (l, 0, j)),
        ],
        out_specs=pl.BlockSpec((1, SUBLANES, ADALN_COLS), lambda l, j: (l, 0, j)),
        out_shape=jax.ShapeDtypeStruct((depth, SUBLANES, width), _F32),
        compiler_params=_compiler_params(2),
        name="adaln",
    )(cond_t, ada_w, ada_b.reshape(depth, 1, width))


def _head_rms(xh, gain):
    return xh * lax.rsqrt(jnp.mean(xh * xh, axis=-1, keepdims=True) + EPS) * gain


def _rope(xh, cos, sin_signed, first_half):
    partner = jnp.where(first_half, pltpu.roll(xh, HEAD_DIM - 32, axis=1), pltpu.roll(xh, 32, axis=1))
    return xh * cos + partner * sin_signed


def _rope_tables(n):
    t = jnp.arange(n)
    row = (t // GRID_W).astype(_F32)
    col = (t % GRID_W).astype(_F32)
    axis_dim = HEAD_DIM // 2
    inv_freq = jnp.power(ROPE_THETA, -jnp.arange(0, axis_dim, 2, dtype=_F32) / axis_dim)
    ang_r = row[:, None] * inv_freq
    ang_c = col[:, None] * inv_freq
    cos = jnp.concatenate([jnp.cos(ang_r)] * 2 + [jnp.cos(ang_c)] * 2, axis=1)
    sin = jnp.concatenate([-jnp.sin(ang_r), jnp.sin(ang_r), -jnp.sin(ang_c), jnp.sin(ang_c)], axis=1)
    return cos, sin


def _ctx_kv_kernel(ctx_ref, mod_ref, gain_ref, w_ref, kg_ref, k_ref, v_ref):
    a = _norm_modulate(ctx_ref[0], gain_ref[...], mod_ref[0, 0:1, :], mod_ref[0, 1:2, :]).astype(_BF16)
    kv = jnp.dot(a, w_ref[...], preferred_element_type=_F32)
    for h in range(N_KV_HEADS):
        sl = slice(h * HEAD_DIM, (h + 1) * HEAD_DIM)
        k_ref[0, :, sl] = _head_rms(kv[:, sl], kg_ref[...]).astype(_BF16)
    v_ref[0] = kv[:, KV_WIDTH:].astype(_BF16)


def _ctx_kv(ctx, mod_ctx, gain, w_kv, k_gain):
    b, lc, d = ctx.shape
    kv_shape = jax.ShapeDtypeStruct((b, lc, KV_WIDTH), _BF16)
    return pl.pallas_call(
        _ctx_kv_kernel,
        grid=(b,),
        in_specs=[
            pl.BlockSpec((1, lc, d), lambda i: (i, 0, 0)),
            pl.BlockSpec((1, N_MOD, d), lambda i: (0, 0, 0)),
            pl.BlockSpec((1, d), lambda i: (0, 0)),
            pl.BlockSpec((d, 2 * KV_WIDTH), lambda i: (0, 0)),
            pl.BlockSpec((1, HEAD_DIM), lambda i: (0, 0)),
        ],
        out_specs=[pl.BlockSpec((1, lc, KV_WIDTH), lambda i: (i, 0, 0))] * 2,
        out_shape=[kv_shape, kv_shape],
        compiler_params=_compiler_params(1),
        name="ctx_kv",
    )(ctx, mod_ctx, gain, w_kv, k_gain)


def _halo_specs(tile, halo, n, d):
    per_tile = tile // halo
    n_halo_blocks = n // halo
    prev_spec = pl.BlockSpec((1, halo, d), lambda b, i: (b, jnp.maximum(i * per_tile - 1, 0), 0))
    main_spec = pl.BlockSpec((1, tile, d), lambda b, i: (b, i, 0))
    next_spec = pl.BlockSpec((1, halo, d), lambda b, i: (b, jnp.minimum((i + 1) * per_tile, n_halo_blocks - 1), 0))
    return prev_spec, main_spec, next_spec


def _halo_row_mask(tile, halo):
    i = pl.program_id(1)
    r = lax.broadcasted_iota(jnp.int32, (tile + 2 * halo, 1), 0)
    ok = jnp.logical_and(jnp.logical_or(r >= halo, i > 0),
                         jnp.logical_or(r < halo + tile, i < pl.num_programs(1) - 1))
    return ok.astype(_F32)


def _in_proj_kernel(xp_ref, x_ref, xn_ref, mod_ref, gain_ref, w_ref, qg_ref, kg_ref, cw_ref, cos_ref, sin_ref,
                    q_ref, k_ref, v_ref, conv_ref):
    tile = x_ref.shape[1]
    gain, shift, scale = gain_ref[...], mod_ref[0, 0:1, :], mod_ref[0, 1:2, :]
    a_main = _norm_modulate(x_ref[0], gain, shift, scale)
    a_prev = _norm_modulate(xp_ref[0], gain, shift, scale)
    a_next = _norm_modulate(xn_ref[0], gain, shift, scale)
    a_ext = jnp.concatenate([a_prev, a_main, a_next], axis=0).astype(_BF16)
    a_mb = a_main.astype(_BF16)

    cos, sin = cos_ref[...], sin_ref[...]
    lane = lax.broadcasted_iota(jnp.int32, (1, HEAD_DIM), 1)
    first_half = (lane % 64) < 32

    o_k = ATTN_WIDTH
    o_v = o_k + KV_WIDTH
    o_b = o_v + KV_WIDTH
    o_c = o_b + CONV_WIDTH

    q = jnp.dot(a_mb, w_ref[:, 0:o_k], preferred_element_type=_F32)
    for h in range(N_Q_HEADS):
        sl = slice(h * HEAD_DIM, (h + 1) * HEAD_DIM)
        qh = _rope(_head_rms(q[:, sl], qg_ref[...]), cos, sin, first_half)
        q_ref[0, :, sl] = (qh * (HEAD_DIM ** -0.5)).astype(_BF16)

    k = jnp.dot(a_mb, w_ref[:, o_k:o_v], preferred_element_type=_F32)
    for h in range(N_KV_HEADS):
        sl = slice(h * HEAD_DIM, (h + 1) * HEAD_DIM)
        k_ref[0, :, sl] = _rope(_head_rms(k[:, sl], kg_ref[...]), cos, sin, first_half).astype(_BF16)

    v_ref[0] = jnp.dot(a_mb, w_ref[:, o_v:o_b], preferred_element_type=_F32).astype(_BF16)

    gate_b = jnp.dot(a_mb, w_ref[:, o_b:o_c], preferred_element_type=_F32)
    cx = jnp.dot(a_ext, w_ref[:, o_c:], preferred_element_type=_F32)
    u = cx[:, :CONV_WIDTH] * cx[:, CONV_WIDTH:] * _halo_row_mask(tile, HALO)
    cw = cw_ref[...]
    conv = cw[0:1] * _shift_rows(u, 1) + cw[1:2] * u + cw[2:3] * _shift_rows(u, -1)
    conv_ref[0] = (gate_b * conv[HALO:HALO + tile]).astype(_BF16)


def _in_proj(x, mod, gain, w_in, q_gain, k_gain, conv_w, cos, sin):
    b, n, d = x.shape
    tile = TOKEN_TILE
    prev_spec, main_spec, next_spec = _halo_specs(tile, HALO, n, d)
    const = lambda *shape: _resident(shape, lambda bi, i: (0,) * len(shape))
    out = lambda width: pl.BlockSpec((1, tile, width), lambda bi, i: (bi, i, 0))
    return pl.pallas_call(
        _in_proj_kernel,
        grid=(b, n // tile),
        in_specs=[
            prev_spec, main_spec, next_spec,
            pl.BlockSpec((1, N_MOD, d), lambda bi, i: (bi, 0, 0)),
            const(1, d),
            const(d, IN_PROJ_WIDTH),
            const(1, HEAD_DIM), const(1, HEAD_DIM),
            const(3, CONV_WIDTH),
            pl.BlockSpec((tile, HEAD_DIM), lambda bi, i: (i, 0)),
            pl.BlockSpec((tile, HEAD_DIM), lambda bi, i: (i, 0)),
        ],
        out_specs=[out(ATTN_WIDTH), out(KV_WIDTH), out(KV_WIDTH), out(CONV_WIDTH)],
        out_shape=[jax.ShapeDtypeStruct((b, n, w), _BF16) for w in (ATTN_WIDTH, KV_WIDTH, KV_WIDTH, CONV_WIDTH)],
        compiler_params=_compiler_params(2),
        name="in_proj",
    )(x, x, x, mod, gain, w_in, q_gain, k_gain, conv_w, cos, sin)


def _attn_kernel(q_ref, kc_ref, vc_ref, k_ref, v_ref, conv_ref, x_ref, mod_ref, wo_ref, o_ref,
                 m_ref, l_ref, acc_ref, attn_ref):
    tq = q_ref.shape[1]
    n_keys = k_ref.shape[1]

    def flash_step(qs, kch, vch):
        s = lax.dot_general(qs, kch, (((1,), (1,)), ((), ())), preferred_element_type=_F32)
        m_prev = m_ref[...]
        m_new = jnp.maximum(m_prev, jnp.max(s, axis=-1, keepdims=True))
        alpha = jnp.exp(m_prev - m_new)
        p = jnp.exp(s - m_new)
        l_ref[...] = alpha * l_ref[...] + jnp.sum(p, axis=-1, keepdims=True)
        acc_ref[...] = alpha * acc_ref[...] + jnp.dot(p.astype(_BF16), vch, preferred_element_type=_F32)
        m_ref[...] = m_new

    for g in range(N_KV_HEADS):
        kv_sl = slice(g * HEAD_DIM, (g + 1) * HEAD_DIM)
        qs = jnp.concatenate(
            [q_ref[0, :, (g * Q_PER_KV + j) * HEAD_DIM:(g * Q_PER_KV + j + 1) * HEAD_DIM] for j in range(Q_PER_KV)],
            axis=0)
        m_ref[...] = jnp.full(m_ref.shape, -jnp.inf, _F32)
        l_ref[...] = jnp.zeros(l_ref.shape, _F32)
        acc_ref[...] = jnp.zeros(acc_ref.shape, _F32)
        flash_step(qs, kc_ref[0, :, kv_sl], vc_ref[0, :, kv_sl])

        def body(j, carry):
            rows = pl.ds(pl.multiple_of(j * KV_TILE, KV_TILE), KV_TILE)
            flash_step(qs, k_ref[0, rows, kv_sl], v_ref[0, rows, kv_sl])
            return carry

        lax.fori_loop(0, n_keys // KV_TILE, body, 0)
        out = (acc_ref[...] / l_ref[...]).astype(_BF16)
        for j in range(Q_PER_KV):
            col = (g * Q_PER_KV + j) * HEAD_DIM
            attn_ref[:, col:col + HEAD_DIM] = out[j * tq:(j + 1) * tq]

    mixed = jnp.concatenate([attn_ref[...], conv_ref[0]], axis=-1)
    y = jnp.dot(mixed, wo_ref[...], preferred_element_type=_F32)
    o_ref[0] = x_ref[0] + mod_ref[0, 2:3, :] * y


def _attn_out(q, k_ctx, v_ctx, k, v, conv, x, mod, w_out):
    b, n, d = x.shape
    lc = k_ctx.shape[1]
    tq = Q_TILE
    per_batch = lambda rows, width: pl.BlockSpec((1, rows, width), lambda bi, i: (bi, 0, 0))
    tile = lambda width: pl.BlockSpec((1, tq, width), lambda bi, i: (bi, i, 0))
    return pl.pallas_call(
        _attn_kernel,
        grid=(b, n // tq),
        in_specs=[
            tile(ATTN_WIDTH),
            per_batch(lc, KV_WIDTH), per_batch(lc, KV_WIDTH),
            per_batch(n, KV_WIDTH), per_batch(n, KV_WIDTH),
            tile(CONV_WIDTH),
            tile(d),
            per_batch(N_MOD, d),
            _resident((d, d), lambda bi, i: (0, 0)),
        ],
        out_specs=tile(d),
        out_shape=jax.ShapeDtypeStruct((b, n, d), _F32),
        scratch_shapes=[
            pltpu.VMEM((Q_PER_KV * tq, 1), _F32),
            pltpu.VMEM((Q_PER_KV * tq, 1), _F32),
            pltpu.VMEM((Q_PER_KV * tq, HEAD_DIM), _F32),
            pltpu.VMEM((tq, ATTN_WIDTH), _BF16),
        ],
        compiler_params=_compiler_params(2),
        name="attn_out",
    )(q, k_ctx, v_ctx, k, v, conv, x, mod, w_out)


def _ffn_kernel(xp_ref, x_ref, xn_ref, mod_ref, gain_ref, wg_ref, wv_ref, cwb_ref, wd_ref, o_ref, acc_ref):
    tile = x_ref.shape[1]
    gain, shift, scale = gain_ref[...], mod_ref[0, 3:4, :], mod_ref[0, 4:5, :]
    x = x_ref[0]
    f_main = _norm_modulate(x, gain, shift, scale)
    f_prev = _norm_modulate(xp_ref[0], gain, shift, scale)
    f_next = _norm_modulate(xn_ref[0], gain, shift, scale)
    f_ext = jnp.concatenate([f_prev, f_main, f_next], axis=0).astype(_BF16)
    f_mb = f_main.astype(_BF16)
    row_mask = _halo_row_mask(tile, HALO)
    acc_ref[...] = jnp.zeros(acc_ref.shape, _F32)

    def body(c, carry):
        gate = jnp.dot(f_ext, wg_ref[c], preferred_element_type=_F32) * row_mask
        val = jnp.dot(f_mb, wv_ref[c], preferred_element_type=_F32)
        cwb = cwb_ref[c]
        conv = cwb[0:1] * _shift_rows(gate, 1) + cwb[1:2] * gate + cwb[2:3] * _shift_rows(gate, -1)
        gate_c = conv[HALO:HALO + tile] + cwb[3:4]
        h = (_silu(gate_c) * val).astype(_BF16)
        acc_ref[...] += jnp.dot(h, wd_ref[c], preferred_element_type=_F32)
        return carry

    lax.fori_loop(0, wg_ref.shape[0], body, 0)
    o_ref[0] = x + mod_ref[0, 5:6, :] * acc_ref[...]


def _ffn(x, mod, gain, w_gate, w_val, conv_wb, w_down):
    b, n, d = x.shape
    tile = TOKEN_TILE
    n_chunks = w_gate.shape[0]
    prev_spec, main_spec, next_spec = _halo_specs(tile, HALO, n, d)
    const3 = lambda a: _resident(a.shape, lambda bi, i: (0, 0, 0))
    return pl.pallas_call(
        _ffn_kernel,
        grid=(b, n // tile),
        in_specs=[
            prev_spec, main_spec, next_spec,
            pl.BlockSpec((1, N_MOD, d), lambda bi, i: (bi, 0, 0)),
            _resident((1, d), lambda bi, i: (0, 0)),
            const3(w_gate), const3(w_val), const3(conv_wb), const3(w_down),
        ],
        out_specs=main_spec,
        out_shape=jax.ShapeDtypeStruct((b, n, d), _F32),
        scratch_shapes=[pltpu.VMEM((tile, d), _F32)],
        compiler_params=_compiler_params(2),
        name="ffn",
    )(x, x, x, mod, gain, w_gate, w_val, conv_wb, w_down)


def _ffn_weights(w_up, conv_w, conv_b, w_down):
    d = w_up.shape[0]
    n_chunks = D_FF // FF_CHUNK
    chunked = lambda w: w.reshape(d, n_chunks, FF_CHUNK).transpose(1, 0, 2).astype(_BF16)
    w_gate, w_val = chunked(w_up[:, :D_FF]), chunked(w_up[:, D_FF:])
    taps_bias = jnp.concatenate([conv_w, conv_b[None, :], jnp.zeros((SUBLANES - 4, D_FF), _F32)], axis=0)
    conv_wb = taps_bias.reshape(SUBLANES, n_chunks, FF_CHUNK).transpose(1, 0, 2)
    return w_gate, w_val, conv_wb, w_down.reshape(n_chunks, FF_CHUNK, d).astype(_BF16)


def _pool_kernel(xp_ref, x_ref, xn_ref, mod_ref, gain_ref, pw_ref, ps_ref, o_ref):
    tile = x_ref.shape[1]
    halo = xp_ref.shape[1]
    n_total = tile * pl.num_programs(1)
    gain, shift, scale = gain_ref[...], mod_ref[0, 0:1, :], mod_ref[0, 1:2, :]
    x = x_ref[0]
    a_prev = _norm_modulate(xp_ref[0], gain, shift, scale)
    a_main = _norm_modulate(x, gain, shift, scale)
    a_next = _norm_modulate(xn_ref[0], gain, shift, scale)
    a_ext = jnp.concatenate([a_prev, a_main, a_next], axis=0) * _halo_row_mask(tile, halo)
    t = pl.program_id(1) * tile + lax.broadcasted_iota(jnp.int32, (tile, 1), 0)

    mixed = []
    for gi, w in enumerate(POOL_WINDOWS):
        a = a_ext[:, gi * POOL_GROUP:(gi + 1) * POOL_GROUP]
        s = _shift_rows(a, 1) + a
        k = 2
        while k < w:
            s = _shift_rows(s, k // 2) + _shift_rows(s, -(k // 2))
            k *= 2
        count = jnp.minimum(t + w - w // 2, n_total) - jnp.maximum(t - w // 2, 0)
        pooled = s[halo:halo + tile] / count.astype(_F32) - a[halo:halo + tile]
        mixed.append(jnp.dot(pooled.astype(_BF16), pw_ref[gi], preferred_element_type=_F32))
    y = jnp.concatenate(mixed, axis=-1) * ps_ref[...]
    o_ref[0] = x + mod_ref[0, 2:3, :] * y


def _pool(x, mod, gain, pool_w, pool_scale):
    b, n, d = x.shape
    tile = TOKEN_TILE
    prev_spec, main_spec, next_spec = _halo_specs(tile, POOL_HALO, n, d)
    return pl.pallas_call(
        _pool_kernel,
        grid=(b, n // tile),
        in_specs=[
            prev_spec, main_spec, next_spec,
            pl.BlockSpec((1, N_MOD, d), lambda bi, i: (bi, 0, 0)),
            _resident((1, d), lambda bi, i: (0, 0)),
            _resident(pool_w.shape, lambda bi, i: (0, 0, 0)),
            _resident((1, d), lambda bi, i: (0, 0)),
        ],
        out_specs=main_spec,
        out_shape=jax.ShapeDtypeStruct((b, n, d), _F32),
        compiler_params=_compiler_params(2),
        name="pool",
    )(x, x, x, mod, gain, pool_w, pool_scale)


def kernel(x, c, ctx, c_ctx, ada_w, ada_b, mix_norm, ffn_norm, even_w_in, even_q_gain, even_k_gain, even_conv_w,
           even_w_out, odd_pool_w, odd_pool_scale, ffn_w_up, ffn_conv_w, ffn_conv_b, ffn_w_down):
    b, n, d = x.shape
    depth = ada_w.shape[0]
    assert depth == 2 and d == D_MODEL and n % TOKEN_TILE == 0 and n % KV_TILE == 0 and n % Q_TILE == 0

    cond = jnp.concatenate([c, c_ctx[None, :], jnp.zeros((SUBLANES - b - 1, d), _F32)], axis=0)
    mods = _adaln(cond.T, ada_w, ada_b).reshape(depth, SUBLANES, N_MOD, d)
    row = lambda v: v.reshape(1, -1)

    w_in = even_w_in[0].astype(_BF16)
    cos, sin = _rope_tables(n)
    k_ctx, v_ctx = _ctx_kv(ctx, mods[0, b:b + 1], row(mix_norm[0]),
                           w_in[:, ATTN_WIDTH:ATTN_WIDTH + 2 * KV_WIDTH], row(even_k_gain[0]))
    q, k, v, conv = _in_proj(x, mods[0, :b], row(mix_norm[0]), w_in, row(even_q_gain[0]), row(even_k_gain[0]),
                             even_conv_w[0], cos, sin)
    x = _attn_out(q, k_ctx, v_ctx, k, v, conv, x, mods[0, :b], even_w_out[0].astype(_BF16))
    x = _ffn(x, mods[0, :b], row(ffn_norm[0]),
             *_ffn_weights(ffn_w_up[0], ffn_conv_w[0], ffn_conv_b[0], ffn_w_down[0]))

    x = _pool(x, mods[1, :b], row(mix_norm[1]), odd_pool_w[0].astype(_BF16), row(odd_pool_scale[0]))
    x = _ffn(x, mods[1, :b], row(ffn_norm[1]),
             *_ffn_weights(ffn_w_up[1], ffn_conv_w[1], ffn_conv_b[1], ffn_w_down[1]))
    return x
```

```python
import functools

import jax
import jax.numpy as jnp
from jax import lax
from jax.experimental import pallas as pl
from jax.experimental.pallas import tpu as pltpu

D_MODEL = 1024
GRID_W = 64
HEAD_DIM = 128
N_Q_HEADS = 4
N_KV_HEADS = 2
Q_PER_KV = N_Q_HEADS // N_KV_HEADS
ATTN_WIDTH = N_Q_HEADS * HEAD_DIM
KV_WIDTH = N_KV_HEADS * HEAD_DIM
CONV_WIDTH = D_MODEL - ATTN_WIDTH
IN_PROJ_WIDTH = ATTN_WIDTH + 2 * KV_WIDTH + 3 * CONV_WIDTH
ROPE_THETA = 10000.0
SCORE_SCALE_LOG2 = HEAD_DIM ** -0.5 * 1.4426950408889634
POOL_WINDOWS = (2, 4, 8, 16)
POOL_GROUP = D_MODEL // len(POOL_WINDOWS)
D_FF = 2816
N_MOD = 6
EPS = 1e-6

SUBLANES = 8
LANES = 128
VMEM_LIMIT_BYTES = 56 * 1024 * 1024

HALO = SUBLANES
POOL_HALO = 2 * SUBLANES
TOKEN_TILE = 512
Q_TILE = 512
KV_TILE = 768
ONES_ROWS = 16
FF_CHUNK = 256
ADALN_COLS = 1536

_BF16 = jnp.bfloat16
_F32 = jnp.float32


def _compiler_params(n_axes):
    return pltpu.CompilerParams(dimension_semantics=("arbitrary",) * n_axes,
                                vmem_limit_bytes=VMEM_LIMIT_BYTES)


def _resident(shape, index_map):
    return pl.BlockSpec(shape, index_map, pipeline_mode=pl.Buffered(1))


def _norm_modulate(x, gain, shift, scale):
    y = x * lax.rsqrt(jnp.mean(x * x, axis=-1, keepdims=True) + EPS)
    return (y * gain) * (1.0 + scale) + shift


def _silu(x):
    return x * (1.0 / (1.0 + jnp.exp(-x)))


def _shift_rows(u, k):
    return pltpu.roll(u, k % u.shape[0], axis=0)


def _adaln_kernel(ct_ref, w_ref, b_ref, o_ref):
    s = _silu(ct_ref[...])
    w = w_ref[0]
    n_vec = 3
    rows = [jnp.sum(s[:, r:r + 1] * w, axis=0, keepdims=True) + b_ref[0] for r in range(n_vec)]
    rows.append(jnp.zeros((SUBLANES - n_vec, w.shape[1]), _F32))
    o_ref[0] = jnp.concatenate(rows, axis=0)


def _adaln(cond_t, ada_w, ada_b):
    depth, d, width = ada_w.shape
    return pl.pallas_call(
        _adaln_kernel,
        grid=(depth, width // ADALN_COLS),
        in_specs=[
            pl.BlockSpec((d, SUBLANES), lambda l, j: (0, 0)),
            pl.BlockSpec((1, d, ADALN_COLS), lambda l, j: (l, 0, j)),
            pl.BlockSpec((1, 1, ADALN_COLS), lambda l, j: (l, 0, j)),
        ],
        out_specs=pl.BlockSpec((1, SUBLANES, ADALN_COLS), lambda l, j: (l, 0, j)),
        out_shape=jax.ShapeDtypeStruct((depth, SUBLANES, width), _F32),
        compiler_params=_compiler_params(2),
        name="adaln",
    )(cond_t, ada_w, ada_b.reshape(depth, 1, width))


def _head_rms(xh, gain):
    return xh * lax.rsqrt(jnp.mean(xh * xh, axis=-1, keepdims=True) + EPS) * gain


def _rope(xh, cos, sin_signed, first_half):
    partner = jnp.where(first_half, pltpu.roll(xh, HEAD_DIM - 32, axis=1), pltpu.roll(xh, 32, axis=1))
    return xh * cos + partner * sin_signed


def _rope_tables(n):
    t = jnp.arange(n)
    row = (t // GRID_W).astype(_F32)
    col = (t % GRID_W).astype(_F32)
    axis_dim = HEAD_DIM // 2
    inv_freq = jnp.power(ROPE_THETA, -jnp.arange(0, axis_dim, 2, dtype=_F32) / axis_dim)
    ang_r = row[:, None] * inv_freq
    ang_c = col[:, None] * inv_freq
    cos = jnp.concatenate([jnp.cos(ang_r)] * 2 + [jnp.cos(ang_c)] * 2, axis=1)
    sin = jnp.concatenate([-jnp.sin(ang_r), jnp.sin(ang_r), -jnp.sin(ang_c), jnp.sin(ang_c)], axis=1)
    return cos, sin


def _ctx_kv_kernel(ctx_ref, mod_ref, gain_ref, w_ref, kg_ref, k_ref, vt_ref):
    a = _norm_modulate(ctx_ref[0], gain_ref[...], mod_ref[0, 0:1, :], mod_ref[0, 1:2, :]).astype(_BF16)
    kv = jnp.dot(a, w_ref[...], preferred_element_type=_F32)
    for h in range(N_KV_HEADS):
        sl = slice(h * HEAD_DIM, (h + 1) * HEAD_DIM)
        k_ref[0, :, sl] = _head_rms(kv[:, sl], kg_ref[...]).astype(_BF16)
    vt_ref[0] = kv[:, KV_WIDTH:].T.astype(_BF16)


def _ctx_kv(ctx, mod_ctx, gain, w_kv, k_gain):
    b, lc, d = ctx.shape
    return pl.pallas_call(
        _ctx_kv_kernel,
        grid=(b,),
        in_specs=[
            pl.BlockSpec((1, lc, d), lambda i: (i, 0, 0)),
            pl.BlockSpec((1, N_MOD, d), lambda i: (0, 0, 0)),
            pl.BlockSpec((1, d), lambda i: (0, 0)),
            pl.BlockSpec((d, 2 * KV_WIDTH), lambda i: (0, 0)),
            pl.BlockSpec((1, HEAD_DIM), lambda i: (0, 0)),
        ],
        out_specs=[pl.BlockSpec((1, lc, KV_WIDTH), lambda i: (i, 0, 0)),
                   pl.BlockSpec((1, KV_WIDTH, lc), lambda i: (i, 0, 0))],
        out_shape=[jax.ShapeDtypeStruct((b, lc, KV_WIDTH), _BF16), jax.ShapeDtypeStruct((b, KV_WIDTH, lc), _BF16)],
        compiler_params=_compiler_params(1),
        name="ctx_kv",
    )(ctx, mod_ctx, gain, w_kv, k_gain)


def _halo_specs(tile, halo, n, d):
    per_tile = tile // halo
    n_halo_blocks = n // halo
    prev_spec = pl.BlockSpec((1, halo, d), lambda b, i: (b, jnp.maximum(i * per_tile - 1, 0), 0))
    main_spec = pl.BlockSpec((1, tile, d), lambda b, i: (b, i, 0))
    next_spec = pl.BlockSpec((1, halo, d), lambda b, i: (b, jnp.minimum((i + 1) * per_tile, n_halo_blocks - 1), 0))
    return prev_spec, main_spec, next_spec


def _halo_row_mask(tile, halo):
    i = pl.program_id(1)
    r = lax.broadcasted_iota(jnp.int32, (tile + 2 * halo, 1), 0)
    ok = jnp.logical_and(jnp.logical_or(r >= halo, i > 0),
                         jnp.logical_or(r < halo + tile, i < pl.num_programs(1) - 1))
    return ok.astype(_F32)


def _in_proj_kernel(xp_ref, x_ref, xn_ref, mod_ref, gain_ref, w_ref, qg_ref, kg_ref, cw_ref, cos_ref, sin_ref,
                    q_ref, k_ref, vt_ref, conv_ref):
    tile = x_ref.shape[1]
    gain, shift, scale = gain_ref[...], mod_ref[0, 0:1, :], mod_ref[0, 1:2, :]
    a_main = _norm_modulate(x_ref[0], gain, shift, scale)
    a_prev = _norm_modulate(xp_ref[0], gain, shift, scale)
    a_next = _norm_modulate(xn_ref[0], gain, shift, scale)
    a_ext = jnp.concatenate([a_prev, a_main, a_next], axis=0).astype(_BF16)
    a_mb = a_main.astype(_BF16)

    cos, sin = cos_ref[...], sin_ref[...]
    lane = lax.broadcasted_iota(jnp.int32, (1, HEAD_DIM), 1)
    first_half = (lane % 64) < 32

    o_k = ATTN_WIDTH
    o_v = o_k + KV_WIDTH
    o_b = o_v + KV_WIDTH
    o_c = o_b + CONV_WIDTH

    q = jnp.dot(a_mb, w_ref[:, 0:o_k], preferred_element_type=_F32)
    for h in range(N_Q_HEADS):
        sl = slice(h * HEAD_DIM, (h + 1) * HEAD_DIM)
        qh = _rope(_head_rms(q[:, sl], qg_ref[...]), cos, sin, first_half)
        q_ref[0, :, sl] = (qh * SCORE_SCALE_LOG2).astype(_BF16)

    k = jnp.dot(a_mb, w_ref[:, o_k:o_v], preferred_element_type=_F32)
    for h in range(N_KV_HEADS):
        sl = slice(h * HEAD_DIM, (h + 1) * HEAD_DIM)
        k_ref[0, :, sl] = _rope(_head_rms(k[:, sl], kg_ref[...]), cos, sin, first_half).astype(_BF16)

    vt_ref[0] = jnp.dot(a_mb, w_ref[:, o_v:o_b], preferred_element_type=_F32).T.astype(_BF16)

    gate_b = jnp.dot(a_mb, w_ref[:, o_b:o_c], preferred_element_type=_F32)
    cx = jnp.dot(a_ext, w_ref[:, o_c:], preferred_element_type=_F32)
    u = cx[:, :CONV_WIDTH] * cx[:, CONV_WIDTH:] * _halo_row_mask(tile, HALO)
    cw = cw_ref[...]
    conv = cw[0:1] * _shift_rows(u, 1) + cw[1:2] * u + cw[2:3] * _shift_rows(u, -1)
    conv_ref[0] = (gate_b * conv[HALO:HALO + tile]).astype(_BF16)


def _in_proj(x, mod, gain, w_in, q_gain, k_gain, conv_w, cos, sin):
    b, n, d = x.shape
    tile = TOKEN_TILE
    prev_spec, main_spec, next_spec = _halo_specs(tile, HALO, n, d)
    const = lambda *shape: _resident(shape, lambda bi, i: (0,) * len(shape))
    out = lambda width: pl.BlockSpec((1, tile, width), lambda bi, i: (bi, i, 0))
    return pl.pallas_call(
        _in_proj_kernel,
        grid=(b, n // tile),
        in_specs=[
            prev_spec, main_spec, next_spec,
            pl.BlockSpec((1, N_MOD, d), lambda bi, i: (bi, 0, 0)),
            const(1, d),
            const(d, IN_PROJ_WIDTH),
            const(1, HEAD_DIM), const(1, HEAD_DIM),
            const(3, CONV_WIDTH),
            pl.BlockSpec((tile, HEAD_DIM), lambda bi, i: (i, 0)),
            pl.BlockSpec((tile, HEAD_DIM), lambda bi, i: (i, 0)),
        ],
        out_specs=[out(ATTN_WIDTH), out(KV_WIDTH),
                   pl.BlockSpec((1, KV_WIDTH, tile), lambda bi, i: (bi, 0, i)),
                   out(CONV_WIDTH)],
        out_shape=[jax.ShapeDtypeStruct((b, n, ATTN_WIDTH), _BF16), jax.ShapeDtypeStruct((b, n, KV_WIDTH), _BF16),
                   jax.ShapeDtypeStruct((b, KV_WIDTH, n), _BF16), jax.ShapeDtypeStruct((b, n, CONV_WIDTH), _BF16)],
        compiler_params=_compiler_params(2),
        name="in_proj",
    )(x, x, x, mod, gain, w_in, q_gain, k_gain, conv_w, cos, sin)


def _attn_kernel(q_ref, k_ref, vt_ref, conv_ref, x_ref, mod_ref, wo_ref, o_ref,
                 s_ref, cmax_ref, m_ref, acc_ref, attn_ref):
    n_chunks = k_ref.shape[1] // KV_TILE

    def keys_of(j):
        return pl.ds(pl.multiple_of(j * KV_TILE, KV_TILE), KV_TILE)

    def score_stage(g, j, c):
        head = g * Q_PER_KV + c
        kch = k_ref[0, keys_of(j), g * HEAD_DIM:(g + 1) * HEAD_DIM]
        qh = q_ref[0, :, head * HEAD_DIM:(head + 1) * HEAD_DIM]
        s = lax.dot_general(kch, qh, (((1,), (1,)), ((), ())), preferred_element_type=_F32)
        s_ref[c] = s
        cmax_ref[c] = jnp.max(s, axis=0, keepdims=True)

    def value_stage(g, j, c):
        vtch = vt_ref[0, g * HEAD_DIM:(g + 1) * HEAD_DIM, keys_of(j)]
        v_aug = jnp.concatenate([vtch, jnp.ones((ONES_ROWS, KV_TILE), _BF16)], axis=0)
        m_prev = m_ref[c]
        m_new = jnp.maximum(m_prev, cmax_ref[c])
        alpha = jnp.exp2(m_prev - m_new)
        p = jnp.exp2(s_ref[c] - m_new).astype(_BF16)
        acc_ref[c] = alpha * acc_ref[c] + jnp.dot(v_aug, p, preferred_element_type=_F32)
        m_ref[c] = m_new

    for g in range(N_KV_HEADS):
        m_ref[...] = jnp.full(m_ref.shape, -jnp.inf, _F32)
        acc_ref[...] = jnp.zeros(acc_ref.shape, _F32)
        score_stage(g, 0, 0)

        def body(j, carry):
            score_stage(g, j, 1)
            value_stage(g, j, 0)
            score_stage(g, j + 1, 0)
            value_stage(g, j, 1)
            return carry

        lax.fori_loop(0, n_chunks - 1, body, 0)
        score_stage(g, n_chunks - 1, 1)
        value_stage(g, n_chunks - 1, 0)
        value_stage(g, n_chunks - 1, 1)
        for c in range(Q_PER_KV):
            acc = acc_ref[c]
            out_t = acc[:HEAD_DIM] / acc[HEAD_DIM:HEAD_DIM + 1]
            head = g * Q_PER_KV + c
            attn_ref[:, head * HEAD_DIM:(head + 1) * HEAD_DIM] = out_t.T.astype(_BF16)

    mixed = jnp.concatenate([attn_ref[...], conv_ref[0]], axis=-1)
    y = jnp.dot(mixed, wo_ref[...], preferred_element_type=_F32)
    o_ref[0] = x_ref[0] + mod_ref[0, 2:3, :] * y


def _attn_out(q, k_all, vt_all, conv, x, mod, w_out):
    b, n, d = x.shape
    n_keys = k_all.shape[1]
    assert n_keys % KV_TILE == 0
    tq = Q_TILE
    per_batch = lambda rows, width: pl.BlockSpec((1, rows, width), lambda bi, i: (bi, 0, 0))
    tile = lambda width: pl.BlockSpec((1, tq, width), lambda bi, i: (bi, i, 0))
    return pl.pallas_call(
        _attn_kernel,
        grid=(b, n // tq),
        in_specs=[
            tile(ATTN_WIDTH),
            per_batch(n_keys, KV_WIDTH), per_batch(KV_WIDTH, n_keys),
            tile(CONV_WIDTH),
            tile(d),
            per_batch(N_MOD, d),
            _resident((d, d), lambda bi, i: (0, 0)),
        ],
        out_specs=tile(d),
        out_shape=jax.ShapeDtypeStruct((b, n, d), _F32),
        scratch_shapes=[
            pltpu.VMEM((Q_PER_KV, KV_TILE, tq), _F32),
            pltpu.VMEM((Q_PER_KV, 1, tq), _F32),
            pltpu.VMEM((Q_PER_KV, 1, tq), _F32),
            pltpu.VMEM((Q_PER_KV, HEAD_DIM + ONES_ROWS, tq), _F32),
            pltpu.VMEM((tq, ATTN_WIDTH), _BF16),
        ],
        compiler_params=_compiler_params(2),
        name="attn_out",
    )(q, k_all, vt_all, conv, x, mod, w_out)


def _ffn_kernel(xp_ref, x_ref, xn_ref, mod_ref, gain_ref, wg_ref, wv_ref, cwb_ref, wd_ref, o_ref, acc_ref):
    tile = x_ref.shape[1]
    gain, shift, scale = gain_ref[...], mod_ref[0, 3:4, :], mod_ref[0, 4:5, :]
    x = x_ref[0]
    f_main = _norm_modulate(x, gain, shift, scale)
    f_prev = _norm_modulate(xp_ref[0], gain, shift, scale)
    f_next = _norm_modulate(xn_ref[0], gain, shift, scale)
    f_ext = jnp.concatenate([f_prev, f_main, f_next], axis=0).astype(_BF16)
    f_mb = f_main.astype(_BF16)
    row_mask = _halo_row_mask(tile, HALO)
    acc_ref[...] = jnp.zeros(acc_ref.shape, _F32)

    def body(c, carry):
        gate = jnp.dot(f_ext, wg_ref[c], preferred_element_type=_F32) * row_mask
        val = jnp.dot(f_mb, wv_ref[c], preferred_element_type=_F32)
        cwb = cwb_ref[c]
        conv = cwb[0:1] * _shift_rows(gate, 1) + cwb[1:2] * gate + cwb[2:3] * _shift_rows(gate, -1)
        gate_c = conv[HALO:HALO + tile] + cwb[3:4]
        h = (_silu(gate_c) * val).astype(_BF16)
        acc_ref[...] += jnp.dot(h, wd_ref[c], preferred_element_type=_F32)
        return carry

    lax.fori_loop(0, wg_ref.shape[0], body, 0, unroll=True)
    o_ref[0] = x + mod_ref[0, 5:6, :] * acc_ref[...]


def _ffn(x, mod, gain, w_gate, w_val, conv_wb, w_down):
    b, n, d = x.shape
    tile = TOKEN_TILE
    n_chunks = w_gate.shape[0]
    prev_spec, main_spec, next_spec = _halo_specs(tile, HALO, n, d)
    const3 = lambda a: _resident(a.shape, lambda bi, i: (0, 0, 0))
    return pl.pallas_call(
        _ffn_kernel,
        grid=(b, n // tile),
        in_specs=[
            prev_spec, main_spec, next_spec,
            pl.BlockSpec((1, N_MOD, d), lambda bi, i: (bi, 0, 0)),
            _resident((1, d), lambda bi, i: (0, 0)),
            const3(w_gate), const3(w_val), const3(conv_wb), const3(w_down),
        ],
        out_specs=main_spec,
        out_shape=jax.ShapeDtypeStruct((b, n, d), _F32),
        scratch_shapes=[pltpu.VMEM((tile, d), _F32)],
        compiler_params=_compiler_params(2),
        name="ffn",
    )(x, x, x, mod, gain, w_gate, w_val, conv_wb, w_down)


def _ffn_weights(w_up, conv_w, conv_b, w_down):
    d = w_up.shape[0]
    n_chunks = D_FF // FF_CHUNK
    chunked = lambda w: w.reshape(d, n_chunks, FF_CHUNK).transpose(1, 0, 2).astype(_BF16)
    w_gate, w_val = chunked(w_up[:, :D_FF]), chunked(w_up[:, D_FF:])
    taps_bias = jnp.concatenate([conv_w, conv_b[None, :], jnp.zeros((SUBLANES - 4, D_FF), _F32)], axis=0)
    conv_wb = taps_bias.reshape(SUBLANES, n_chunks, FF_CHUNK).transpose(1, 0, 2)
    return w_gate, w_val, conv_wb, w_down.reshape(n_chunks, FF_CHUNK, d).astype(_BF16)


def _pool_kernel(xp_ref, x_ref, xn_ref, mod_ref, gain_ref, pw_ref, ps_ref, o_ref):
    tile = x_ref.shape[1]
    halo = xp_ref.shape[1]
    n_total = tile * pl.num_programs(1)
    gain, shift, scale = gain_ref[...], mod_ref[0, 0:1, :], mod_ref[0, 1:2, :]
    x = x_ref[0]
    a_prev = _norm_modulate(xp_ref[0], gain, shift, scale)
    a_main = _norm_modulate(x, gain, shift, scale)
    a_next = _norm_modulate(xn_ref[0], gain, shift, scale)
    a_ext = jnp.concatenate([a_prev, a_main, a_next], axis=0) * _halo_row_mask(tile, halo)
    t = pl.program_id(1) * tile + lax.broadcasted_iota(jnp.int32, (tile, 1), 0)

    mixed = []
    for gi, w in enumerate(POOL_WINDOWS):
        a = a_ext[:, gi * POOL_GROUP:(gi + 1) * POOL_GROUP]
        s = _shift_rows(a, 1) + a
        k = 2
        while k < w:
            s = _shift_rows(s, k // 2) + _shift_rows(s, -(k // 2))
            k *= 2
        count = jnp.minimum(t + w - w // 2, n_total) - jnp.maximum(t - w // 2, 0)
        pooled = s[halo:halo + tile] / count.astype(_F32) - a[halo:halo + tile]
        mixed.append(jnp.dot(pooled.astype(_BF16), pw_ref[gi], preferred_element_type=_F32))
    y = jnp.concatenate(mixed, axis=-1) * ps_ref[...]
    o_ref[0] = x + mod_ref[0, 2:3, :] * y


def _pool(x, mod, gain, pool_w, pool_scale):
    b, n, d = x.shape
    tile = TOKEN_TILE
    prev_spec, main_spec, next_spec = _halo_specs(tile, POOL_HALO, n, d)
    return pl.pallas_call(
        _pool_kernel,
        grid=(b, n // tile),
        in_specs=[
            prev_spec, main_spec, next_spec,
            pl.BlockSpec((1, N_MOD, d), lambda bi, i: (bi, 0, 0)),
            _resident((1, d), lambda bi, i: (0, 0)),
            _resident(pool_w.shape, lambda bi, i: (0, 0, 0)),
            _resident((1, d), lambda bi, i: (0, 0)),
        ],
        out_specs=main_spec,
        out_shape=jax.ShapeDtypeStruct((b, n, d), _F32),
        compiler_params=_compiler_params(2),
        name="pool",
    )(x, x, x, mod, gain, pool_w, pool_scale)


def kernel(x, c, ctx, c_ctx, ada_w, ada_b, mix_norm, ffn_norm, even_w_in, even_q_gain, even_k_gain, even_conv_w,
           even_w_out, odd_pool_w, odd_pool_scale, ffn_w_up, ffn_conv_w, ffn_conv_b, ffn_w_down):
    b, n, d = x.shape
    depth = ada_w.shape[0]
    assert depth == 2 and d == D_MODEL and n % TOKEN_TILE == 0 and n % Q_TILE == 0

    cond = jnp.concatenate([c, c_ctx[None, :], jnp.zeros((SUBLANES - b - 1, d), _F32)], axis=0)
    mods = _adaln(cond.T, ada_w, ada_b).reshape(depth, SUBLANES, N_MOD, d)
    row = lambda v: v.reshape(1, -1)

    w_in = even_w_in[0].astype(_BF16)
    cos, sin = _rope_tables(n)
    k_ctx, vt_ctx = _ctx_kv(ctx, mods[0, b:b + 1], row(mix_norm[0]),
                            w_in[:, ATTN_WIDTH:ATTN_WIDTH + 2 * KV_WIDTH], row(even_k_gain[0]))
    q, k, vt, conv = _in_proj(x, mods[0, :b], row(mix_norm[0]), w_in, row(even_q_gain[0]), row(even_k_gain[0]),
                              even_conv_w[0], cos, sin)
    k_all = jnp.concatenate([k_ctx, k], axis=1)
    vt_all = jnp.concatenate([vt_ctx, vt], axis=2)
    x = _attn_out(q, k_all, vt_all, conv, x, mods[0, :b], even_w_out[0].astype(_BF16))
    x = _ffn(x, mods[0, :b], row(ffn_norm[0]),
             *_ffn_weights(ffn_w_up[0], ffn_conv_w[0], ffn_conv_b[0], ffn_w_down[0]))

    x = _pool(x, mods[1, :b], row(mix_norm[1]), odd_pool_w[0].astype(_BF16), row(odd_pool_scale[0]))
    x = _ffn(x, mods[1, :b], row(ffn_norm[1]),
             *_ffn_weights(ffn_w_up[1], ffn_conv_w[1], ffn_conv_b[1], ffn_w_down[1]))
    return x
```

```python
import functools

import jax
import jax.numpy as jnp
from jax import lax
from jax.experimental import pallas as pl
from jax.experimental.pallas import tpu as pltpu

D_MODEL = 1024
GRID_W = 64
HEAD_DIM = 128
N_Q_HEADS = 4
N_KV_HEADS = 2
Q_PER_KV = N_Q_HEADS // N_KV_HEADS
ATTN_WIDTH = N_Q_HEADS * HEAD_DIM
KV_WIDTH = N_KV_HEADS * HEAD_DIM
CONV_WIDTH = D_MODEL - ATTN_WIDTH
IN_PROJ_WIDTH = ATTN_WIDTH + 2 * KV_WIDTH + 3 * CONV_WIDTH
ROPE_THETA = 10000.0
SCORE_SCALE_LOG2 = HEAD_DIM ** -0.5 * 1.4426950408889634
POOL_WINDOWS = (2, 4, 8, 16)
POOL_GROUP = D_MODEL // len(POOL_WINDOWS)
D_FF = 2816
N_MOD = 6
EPS = 1e-6

SUBLANES = 8
LANES = 128
VMEM_LIMIT_BYTES = 56 * 1024 * 1024

HALO = SUBLANES
POOL_HALO = 2 * SUBLANES
TOKEN_TILE = 512
Q_TILE = 512
KV_TILE = 768
ONES_ROWS = 16
FF_CHUNK = 256
ADALN_COLS = 1536

_BF16 = jnp.bfloat16
_F32 = jnp.float32


def _compiler_params(n_axes):
    return pltpu.CompilerParams(dimension_semantics=("arbitrary",) * n_axes,
                                vmem_limit_bytes=VMEM_LIMIT_BYTES)


def _resident(shape, index_map):
    return pl.BlockSpec(shape, index_map, pipeline_mode=pl.Buffered(1))


def _norm_modulate(x, gain, shift, scale):
    y = x * lax.rsqrt(jnp.mean(x * x, axis=-1, keepdims=True) + EPS)
    return (y * gain) * (1.0 + scale) + shift


def _silu(x):
    return x * (1.0 / (1.0 + jnp.exp(-x)))


def _shift_rows(u, k):
    return pltpu.roll(u, k % u.shape[0], axis=0)


def _adaln_kernel(ct_ref, w_ref, b_ref, o_ref):
    s = _silu(ct_ref[...])
    w = w_ref[0]
    n_vec = 3
    rows = [jnp.sum(s[:, r:r + 1] * w, axis=0, keepdims=True) + b_ref[0] for r in range(n_vec)]
    rows.append(jnp.zeros((SUBLANES - n_vec, w.shape[1]), _F32))
    o_ref[0] = jnp.concatenate(rows, axis=0)


def _adaln(cond_t, ada_w, ada_b):
    depth, d, width = ada_w.shape
    return pl.pallas_call(
        _adaln_kernel,
        grid=(depth, width // ADALN_COLS),
        in_specs=[
            pl.BlockSpec((d, SUBLANES), lambda l, j: (0, 0)),
            pl.BlockSpec((1, d, ADALN_COLS), lambda l, j: (l, 0, j)),
            pl.BlockSpec((1, 1, ADALN_COLS), lambda l, j: (l, 0, j)),
        ],
        out_specs=pl.BlockSpec((1, SUBLANES, ADALN_COLS), lambda l, j: (l, 0, j)),
        out_shape=jax.ShapeDtypeStruct((depth, SUBLANES, width), _F32),
        compiler_params=_compiler_params(2),
        name="adaln",
    )(cond_t, ada_w, ada_b.reshape(depth, 1, width))


def _head_rms(xh, gain):
    return xh * lax.rsqrt(jnp.mean(xh * xh, axis=-1, keepdims=True) + EPS) * gain


def _rope(xh, cos, sin_signed, first_half):
    partner = jnp.where(first_half, pltpu.roll(xh, HEAD_DIM - 32, axis=1), pltpu.roll(xh, 32, axis=1))
    return xh * cos + partner * sin_signed


def _rope_tables(n):
    t = jnp.arange(n)
    row = (t // GRID_W).astype(_F32)
    col = (t % GRID_W).astype(_F32)
    axis_dim = HEAD_DIM // 2
    inv_freq = jnp.power(ROPE_THETA, -jnp.arange(0, axis_dim, 2, dtype=_F32) / axis_dim)
    ang_r = row[:, None] * inv_freq
    ang_c = col[:, None] * inv_freq
    cos = jnp.concatenate([jnp.cos(ang_r)] * 2 + [jnp.cos(ang_c)] * 2, axis=1)
    sin = jnp.concatenate([-jnp.sin(ang_r), jnp.sin(ang_r), -jnp.sin(ang_c), jnp.sin(ang_c)], axis=1)
    return cos, sin


def _ctx_kv_kernel(ctx_ref, mod_ref, gain_ref, w_ref, kg_ref, k_ref, vt_ref):
    a = _norm_modulate(ctx_ref[0], gain_ref[...], mod_ref[0, 0:1, :], mod_ref[0, 1:2, :]).astype(_BF16)
    kv = jnp.dot(a, w_ref[...], preferred_element_type=_F32)
    for h in range(N_KV_HEADS):
        sl = slice(h * HEAD_DIM, (h + 1) * HEAD_DIM)
        k_ref[0, :, sl] = _head_rms(kv[:, sl], kg_ref[...]).astype(_BF16)
    vt_ref[0] = kv[:, KV_WIDTH:].T.astype(_BF16)


def _ctx_kv(ctx, mod_ctx, gain, w_kv, k_gain):
    b, lc, d = ctx.shape
    return pl.pallas_call(
        _ctx_kv_kernel,
        grid=(b,),
        in_specs=[
            pl.BlockSpec((1, lc, d), lambda i: (i, 0, 0)),
            pl.BlockSpec((1, N_MOD, d), lambda i: (0, 0, 0)),
            pl.BlockSpec((1, d), lambda i: (0, 0)),
            pl.BlockSpec((d, 2 * KV_WIDTH), lambda i: (0, 0)),
            pl.BlockSpec((1, HEAD_DIM), lambda i: (0, 0)),
        ],
        out_specs=[pl.BlockSpec((1, lc, KV_WIDTH), lambda i: (i, 0, 0)),
                   pl.BlockSpec((1, KV_WIDTH, lc), lambda i: (i, 0, 0))],
        out_shape=[jax.ShapeDtypeStruct((b, lc, KV_WIDTH), _BF16), jax.ShapeDtypeStruct((b, KV_WIDTH, lc), _BF16)],
        compiler_params=_compiler_params(1),
        name="ctx_kv",
    )(ctx, mod_ctx, gain, w_kv, k_gain)


def _halo_specs(tile, halo, n, d):
    per_tile = tile // halo
    n_halo_blocks = n // halo
    prev_spec = pl.BlockSpec((1, halo, d), lambda b, i: (b, jnp.maximum(i * per_tile - 1, 0), 0))
    main_spec = pl.BlockSpec((1, tile, d), lambda b, i: (b, i, 0))
    next_spec = pl.BlockSpec((1, halo, d), lambda b, i: (b, jnp.minimum((i + 1) * per_tile, n_halo_blocks - 1), 0))
    return prev_spec, main_spec, next_spec


def _edge_flags():
    i = pl.program_id(1)
    return (i > 0).astype(_F32), (i < pl.num_programs(1) - 1).astype(_F32)


def _in_proj_kernel(xp_ref, x_ref, xn_ref, mod_ref, gain_ref, w_ref, qg_ref, kg_ref, cw_ref, cos_ref, sin_ref,
                    q_ref, k_ref, vt_ref, conv_ref):
    tile = x_ref.shape[1]
    gain, shift, scale = gain_ref[...], mod_ref[0, 0:1, :], mod_ref[0, 1:2, :]
    prev_ok, next_ok = _edge_flags()
    a_main = _norm_modulate(x_ref[0], gain, shift, scale)
    a_prev = _norm_modulate(xp_ref[0], gain, shift, scale) * prev_ok
    a_next = _norm_modulate(xn_ref[0], gain, shift, scale) * next_ok
    a_ext = jnp.concatenate([a_prev, a_main, a_next], axis=0).astype(_BF16)
    a_mb = a_main.astype(_BF16)

    cos, sin = cos_ref[...], sin_ref[...]
    lane = lax.broadcasted_iota(jnp.int32, (1, HEAD_DIM), 1)
    first_half = (lane % 64) < 32

    o_k = ATTN_WIDTH
    o_v = o_k + KV_WIDTH
    o_b = o_v + KV_WIDTH
    o_c = o_b + CONV_WIDTH

    q = jnp.dot(a_mb, w_ref[:, 0:o_k], preferred_element_type=_F32)
    for h in range(N_Q_HEADS):
        sl = slice(h * HEAD_DIM, (h + 1) * HEAD_DIM)
        qh = _rope(_head_rms(q[:, sl], qg_ref[...]), cos, sin, first_half)
        q_ref[0, :, sl] = (qh * SCORE_SCALE_LOG2).astype(_BF16)

    k = jnp.dot(a_mb, w_ref[:, o_k:o_v], preferred_element_type=_F32)
    for h in range(N_KV_HEADS):
        sl = slice(h * HEAD_DIM, (h + 1) * HEAD_DIM)
        k_ref[0, :, sl] = _rope(_head_rms(k[:, sl], kg_ref[...]), cos, sin, first_half).astype(_BF16)

    vt_ref[0] = jnp.dot(a_mb, w_ref[:, o_v:o_b], preferred_element_type=_F32).T.astype(_BF16)

    gate_b = jnp.dot(a_mb, w_ref[:, o_b:o_c], preferred_element_type=_F32)
    cx = jnp.dot(a_ext, w_ref[:, o_c:], preferred_element_type=_F32)
    u = cx[:, :CONV_WIDTH] * cx[:, CONV_WIDTH:]
    cw = cw_ref[...]
    conv = cw[0:1] * _shift_rows(u, 1) + cw[1:2] * u + cw[2:3] * _shift_rows(u, -1)
    conv_ref[0] = (gate_b * conv[HALO:HALO + tile]).astype(_BF16)


def _in_proj(x, mod, gain, w_in, q_gain, k_gain, conv_w, cos, sin):
    b, n, d = x.shape
    tile = TOKEN_TILE
    prev_spec, main_spec, next_spec = _halo_specs(tile, HALO, n, d)
    const = lambda *shape: _resident(shape, lambda bi, i: (0,) * len(shape))
    out = lambda width: pl.BlockSpec((1, tile, width), lambda bi, i: (bi, i, 0))
    return pl.pallas_call(
        _in_proj_kernel,
        grid=(b, n // tile),
        in_specs=[
            prev_spec, main_spec, next_spec,
            pl.BlockSpec((1, N_MOD, d), lambda bi, i: (bi, 0, 0)),
            const(1, d),
            const(d, IN_PROJ_WIDTH),
            const(1, HEAD_DIM), const(1, HEAD_DIM),
            const(3, CONV_WIDTH),
            pl.BlockSpec((tile, HEAD_DIM), lambda bi, i: (i, 0)),
            pl.BlockSpec((tile, HEAD_DIM), lambda bi, i: (i, 0)),
        ],
        out_specs=[out(ATTN_WIDTH), out(KV_WIDTH),
                   pl.BlockSpec((1, KV_WIDTH, tile), lambda bi, i: (bi, 0, i)),
                   out(CONV_WIDTH)],
        out_shape=[jax.ShapeDtypeStruct((b, n, ATTN_WIDTH), _BF16), jax.ShapeDtypeStruct((b, n, KV_WIDTH), _BF16),
                   jax.ShapeDtypeStruct((b, KV_WIDTH, n), _BF16), jax.ShapeDtypeStruct((b, n, CONV_WIDTH), _BF16)],
        compiler_params=_compiler_params(2),
        name="in_proj",
    )(x, x, x, mod, gain, w_in, q_gain, k_gain, conv_w, cos, sin)


def _attn_kernel(q_ref, k_ref, vt_ref, conv_ref, x_ref, mod_ref, wo_ref, o_ref,
                 s_ref, cmax_ref, m_ref, acc_ref, attn_ref):
    n_chunks = k_ref.shape[1] // KV_TILE

    def keys_of(j):
        return pl.ds(pl.multiple_of(j * KV_TILE, KV_TILE), KV_TILE)

    def score_stage(g, j, c):
        head = g * Q_PER_KV + c
        kch = k_ref[0, keys_of(j), g * HEAD_DIM:(g + 1) * HEAD_DIM]
        qh = q_ref[0, :, head * HEAD_DIM:(head + 1) * HEAD_DIM]
        s = lax.dot_general(kch, qh, (((1,), (1,)), ((), ())), preferred_element_type=_F32)
        s_ref[c] = s
        cmax_ref[c] = jnp.max(s, axis=0, keepdims=True)

    def value_stage(g, j, c):
        vtch = vt_ref[0, g * HEAD_DIM:(g + 1) * HEAD_DIM, keys_of(j)]
        v_aug = jnp.concatenate([vtch, jnp.ones((ONES_ROWS, KV_TILE), _BF16)], axis=0)
        m_prev = m_ref[c]
        m_new = jnp.maximum(m_prev, cmax_ref[c])
        alpha = jnp.exp2(m_prev - m_new)
        p = jnp.exp2(s_ref[c] - m_new).astype(_BF16)
        acc_ref[c] = alpha * acc_ref[c] + jnp.dot(v_aug, p, preferred_element_type=_F32)
        m_ref[c] = m_new

    for g in range(N_KV_HEADS):
        m_ref[...] = jnp.full(m_ref.shape, -jnp.inf, _F32)
        acc_ref[...] = jnp.zeros(acc_ref.shape, _F32)
        score_stage(g, 0, 0)

        def body(j, carry):
            score_stage(g, j, 1)
            value_stage(g, j, 0)
            score_stage(g, j + 1, 0)
            value_stage(g, j, 1)
            return carry

        lax.fori_loop(0, n_chunks - 1, body, 0)
        score_stage(g, n_chunks - 1, 1)
        value_stage(g, n_chunks - 1, 0)
        value_stage(g, n_chunks - 1, 1)
        for c in range(Q_PER_KV):
            acc = acc_ref[c]
            out_t = acc[:HEAD_DIM] / acc[HEAD_DIM:HEAD_DIM + 1]
            head = g * Q_PER_KV + c
            attn_ref[:, head * HEAD_DIM:(head + 1) * HEAD_DIM] = out_t.T.astype(_BF16)

    mixed = jnp.concatenate([attn_ref[...], conv_ref[0]], axis=-1)
    y = jnp.dot(mixed, wo_ref[...], preferred_element_type=_F32)
    o_ref[0] = x_ref[0] + mod_ref[0, 2:3, :] * y


def _attn_out(q, k_all, vt_all, conv, x, mod, w_out):
    b, n, d = x.shape
    n_keys = k_all.shape[1]
    assert n_keys % KV_TILE == 0
    tq = Q_TILE
    per_batch = lambda rows, width: pl.BlockSpec((1, rows, width), lambda bi, i: (bi, 0, 0))
    tile = lambda width: pl.BlockSpec((1, tq, width), lambda bi, i: (bi, i, 0))
    return pl.pallas_call(
        _attn_kernel,
        grid=(b, n // tq),
        in_specs=[
            tile(ATTN_WIDTH),
            per_batch(n_keys, KV_WIDTH), per_batch(KV_WIDTH, n_keys),
            tile(CONV_WIDTH),
            tile(d),
            per_batch(N_MOD, d),
            _resident((d, d), lambda bi, i: (0, 0)),
        ],
        out_specs=tile(d),
        out_shape=jax.ShapeDtypeStruct((b, n, d), _F32),
        scratch_shapes=[
            pltpu.VMEM((Q_PER_KV, KV_TILE, tq), _F32),
            pltpu.VMEM((Q_PER_KV, 1, tq), _F32),
            pltpu.VMEM((Q_PER_KV, 1, tq), _F32),
            pltpu.VMEM((Q_PER_KV, HEAD_DIM + ONES_ROWS, tq), _F32),
            pltpu.VMEM((tq, ATTN_WIDTH), _BF16),
        ],
        compiler_params=_compiler_params(2),
        name="attn_out",
    )(q, k_all, vt_all, conv, x, mod, w_out)


def _ffn_kernel(xp_ref, x_ref, xn_ref, mod_ref, gain_ref, wup_ref, cwb_ref, wd_ref, o_ref, h_ref):
    tile = x_ref.shape[1]
    gain, shift, scale = gain_ref[...], mod_ref[0, 3:4, :], mod_ref[0, 4:5, :]
    x = x_ref[0]
    prev_ok, next_ok = _edge_flags()
    f_main = _norm_modulate(x, gain, shift, scale)
    f_prev = _norm_modulate(xp_ref[0], gain, shift, scale) * prev_ok
    f_next = _norm_modulate(xn_ref[0], gain, shift, scale) * next_ok
    f_ext = jnp.concatenate([f_prev, f_main, f_next], axis=0).astype(_BF16)
    f_mb = f_main.astype(_BF16)
    n_chunks = D_FF // FF_CHUNK

    def up(c):
        cols = slice(c * FF_CHUNK, (c + 1) * FF_CHUNK)
        val_cols = slice(D_FF + c * FF_CHUNK, D_FF + (c + 1) * FF_CHUNK)
        return (jnp.dot(f_ext, wup_ref[:, cols], preferred_element_type=_F32),
                jnp.dot(f_mb, wup_ref[:, val_cols], preferred_element_type=_F32))

    def act(c, gate, val):
        cols = slice(c * FF_CHUNK, (c + 1) * FF_CHUNK)
        cwb = cwb_ref[:, cols]
        conv = cwb[0:1] * _shift_rows(gate, 1) + cwb[1:2] * gate + cwb[2:3] * _shift_rows(gate, -1)
        h_ref[:, cols] = (_silu(conv[HALO:HALO + tile] + cwb[3:4]) * val).astype(_BF16)

    nxt = up(0)
    for c in range(n_chunks):
        cur = nxt
        if c + 1 < n_chunks:
            nxt = up(c + 1)
        act(c, *cur)
    y = jnp.dot(h_ref[...], wd_ref[...], preferred_element_type=_F32)
    o_ref[0] = x + mod_ref[0, 5:6, :] * y


def _ffn(x, mod, gain, w_up, conv_wb, w_down):
    b, n, d = x.shape
    tile = TOKEN_TILE
    prev_spec, main_spec, next_spec = _halo_specs(tile, HALO, n, d)
    const2 = lambda a: _resident(a.shape, lambda bi, i: (0, 0))
    return pl.pallas_call(
        _ffn_kernel,
        grid=(b, n // tile),
        in_specs=[
            prev_spec, main_spec, next_spec,
            pl.BlockSpec((1, N_MOD, d), lambda bi, i: (bi, 0, 0)),
            _resident((1, d), lambda bi, i: (0, 0)),
            const2(w_up), const2(conv_wb), const2(w_down),
        ],
        out_specs=main_spec,
        out_shape=jax.ShapeDtypeStruct((b, n, d), _F32),
        scratch_shapes=[pltpu.VMEM((tile, D_FF), _BF16)],
        compiler_params=_compiler_params(2),
        name="ffn",
    )(x, x, x, mod, gain, w_up, conv_wb, w_down)


def _ffn_weights(w_up, conv_w, conv_b, w_down):
    taps_bias = jnp.concatenate([conv_w, conv_b[None, :], jnp.zeros((SUBLANES - 4, D_FF), _F32)], axis=0)
    return w_up.astype(_BF16), taps_bias, w_down.astype(_BF16)


def _pool_kernel(xp_ref, x_ref, xn_ref, mod_ref, gain_ref, pw_ref, ps_ref, o_ref):
    tile = x_ref.shape[1]
    halo = xp_ref.shape[1]
    n_total = tile * pl.num_programs(1)
    gain, shift, scale = gain_ref[...], mod_ref[0, 0:1, :], mod_ref[0, 1:2, :]
    x = x_ref[0]
    prev_ok, next_ok = _edge_flags()
    a_prev = _norm_modulate(xp_ref[0], gain, shift, scale) * prev_ok
    a_main = _norm_modulate(x, gain, shift, scale)
    a_next = _norm_modulate(xn_ref[0], gain, shift, scale) * next_ok
    a_ext = jnp.concatenate([a_prev, a_main, a_next], axis=0)
    t = pl.program_id(1) * tile + lax.broadcasted_iota(jnp.int32, (tile, 1), 0)

    mixed = []
    for gi, w in enumerate(POOL_WINDOWS):
        a = a_ext[:, gi * POOL_GROUP:(gi + 1) * POOL_GROUP]
        s = _shift_rows(a, 1) + a
        k = 2
        while k < w:
            s = _shift_rows(s, k // 2) + _shift_rows(s, -(k // 2))
            k *= 2
        count = jnp.minimum(t + w - w // 2, n_total) - jnp.maximum(t - w // 2, 0)
        pooled = s[halo:halo + tile] / count.astype(_F32) - a[halo:halo + tile]
        mixed.append(jnp.dot(pooled.astype(_BF16), pw_ref[gi], preferred_element_type=_F32))
    y = jnp.concatenate(mixed, axis=-1) * ps_ref[...]
    o_ref[0] = x + mod_ref[0, 2:3, :] * y


def _pool(x, mod, gain, pool_w, pool_scale):
    b, n, d = x.shape
    tile = TOKEN_TILE
    prev_spec, main_spec, next_spec = _halo_specs(tile, POOL_HALO, n, d)
    return pl.pallas_call(
        _pool_kernel,
        grid=(b, n // tile),
        in_specs=[
            prev_spec, main_spec, next_spec,
            pl.BlockSpec((1, N_MOD, d), lambda bi, i: (bi, 0, 0)),
            _resident((1, d), lambda bi, i: (0, 0)),
            _resident(pool_w.shape, lambda bi, i: (0, 0, 0)),
            _resident((1, d), lambda bi, i: (0, 0)),
        ],
        out_specs=main_spec,
        out_shape=jax.ShapeDtypeStruct((b, n, d), _F32),
        compiler_params=_compiler_params(2),
        name="pool",
    )(x, x, x, mod, gain, pool_w, pool_scale)


def kernel(x, c, ctx, c_ctx, ada_w, ada_b, mix_norm, ffn_norm, even_w_in, even_q_gain, even_k_gain, even_conv_w,
           even_w_out, odd_pool_w, odd_pool_scale, ffn_w_up, ffn_conv_w, ffn_conv_b, ffn_w_down):
    b, n, d = x.shape
    depth = ada_w.shape[0]
    assert depth == 2 and d == D_MODEL and n % TOKEN_TILE == 0 and n % Q_TILE == 0

    cond = jnp.concatenate([c, c_ctx[None, :], jnp.zeros((SUBLANES - b - 1, d), _F32)], axis=0)
    mods = _adaln(cond.T, ada_w, ada_b).reshape(depth, SUBLANES, N_MOD, d)
    row = lambda v: v.reshape(1, -1)

    w_in = even_w_in[0].astype(_BF16)
    cos, sin = _rope_tables(n)
    k_ctx, vt_ctx = _ctx_kv(ctx, mods[0, b:b + 1], row(mix_norm[0]),
                            w_in[:, ATTN_WIDTH:ATTN_WIDTH + 2 * KV_WIDTH], row(even_k_gain[0]))
    q, k, vt, conv = _in_proj(x, mods[0, :b], row(mix_norm[0]), w_in, row(even_q_gain[0]), row(even_k_gain[0]),
                              even_conv_w[0], cos, sin)
    k_all = jnp.concatenate([k_ctx, k], axis=1)
    vt_all = jnp.concatenate([vt_ctx, vt], axis=2)
    x = _attn_out(q, k_all, vt_all, conv, x, mods[0, :b], even_w_out[0].astype(_BF16))
    x = _ffn(x, mods[0, :b], row(ffn_norm[0]),
             *_ffn_weights(ffn_w_up[0], ffn_conv_w[0], ffn_conv_b[0], ffn_w_down[0]))

    x = _pool(x, mods[1, :b], row(mix_norm[1]), odd_pool_w[0].astype(_BF16), row(odd_pool_scale[0]))
    x = _ffn(x, mods[1, :b], row(ffn_norm[1]),
             *_ffn_weights(ffn_w_up[1], ffn_conv_w[1], ffn_conv_b[1], ffn_w_down[1]))
    return x
```

```python
import jax
import jax.numpy as jnp
from jax import lax
from jax.experimental import pallas as pl
from jax.experimental.pallas import tpu as pltpu

D_MODEL = 1024
GRID_W = 64
HEAD_DIM = 128
N_Q_HEADS = 4
N_KV_HEADS = 2
Q_PER_KV = N_Q_HEADS // N_KV_HEADS
ATTN_WIDTH = N_Q_HEADS * HEAD_DIM
KV_WIDTH = N_KV_HEADS * HEAD_DIM
CONV_WIDTH = D_MODEL - ATTN_WIDTH
IN_PROJ_WIDTH = ATTN_WIDTH + 2 * KV_WIDTH + 3 * CONV_WIDTH
ROPE_THETA = 10000.0
ROPE_FREQS = HEAD_DIM // 4
SCORE_SCALE_LOG2 = HEAD_DIM ** -0.5 * 1.4426950408889634
POOL_WINDOWS = (2, 4, 8, 16)
POOL_GROUP = D_MODEL // len(POOL_WINDOWS)
D_FF = 2816
N_MOD = 6
EPS = 1e-6

SUBLANES = 8
VMEM_LIMIT_BYTES = 56 * 1024 * 1024

HALO = SUBLANES
POOL_HALO = 2 * SUBLANES
TOKEN_TILE = 512
Q_TILE = 512
KV_TILE = 768
ONES_ROWS = 16
FF_CHUNK = 256
ADALN_COLS = 1536

_BF16 = jnp.bfloat16
_F32 = jnp.float32


def _compiler_params(n_axes):
    return pltpu.CompilerParams(dimension_semantics=("arbitrary",) * n_axes,
                                vmem_limit_bytes=VMEM_LIMIT_BYTES)


def _resident(shape, index_map):
    return pl.BlockSpec(shape, index_map, pipeline_mode=pl.Buffered(1))


def _layer_spec(arr, layer):
    zeros = (0,) * (arr.ndim - 1)
    return _resident((None,) + arr.shape[1:], lambda *_: (layer,) + zeros)


def _mod_spec(d, layer, row_of):
    return pl.BlockSpec((None, None, N_MOD, d), lambda *ids: (layer, row_of(*ids), 0, 0))


def _norm_modulate(x, gain, shift, scale):
    y = x * lax.rsqrt(jnp.mean(x * x, axis=-1, keepdims=True) + EPS)
    return (y * gain) * (1.0 + scale) + shift


def _silu(x):
    return x * (1.0 / (1.0 + jnp.exp(-x)))


def _shift_rows(u, k):
    return pltpu.roll(u, k % u.shape[0], axis=0)


def _adaln_kernel(ct_ref, w_ref, b_ref, o_ref):
    s = _silu(ct_ref[...])
    w = w_ref[0]
    n_vec = 3
    rows = [jnp.sum(s[:, r:r + 1] * w, axis=0, keepdims=True) + b_ref[0] for r in range(n_vec)]
    rows.append(jnp.zeros((SUBLANES - n_vec, w.shape[1]), _F32))
    o_ref[0] = jnp.concatenate(rows, axis=0)


def _adaln(cond_t, ada_w, ada_b):
    depth, d, width = ada_w.shape
    return pl.pallas_call(
        _adaln_kernel,
        grid=(depth, width // ADALN_COLS),
        in_specs=[
            pl.BlockSpec((d, SUBLANES), lambda l, j: (0, 0)),
            pl.BlockSpec((1, d, ADALN_COLS), lambda l, j: (l, 0, j)),
            pl.BlockSpec((1, 1, ADALN_COLS), lambda l, j: (l, 0, j)),
        ],
        out_specs=pl.BlockSpec((1, SUBLANES, ADALN_COLS), lambda l, j: (l, 0, j)),
        out_shape=jax.ShapeDtypeStruct((depth, SUBLANES, width), _F32),
        compiler_params=_compiler_params(2),
        name="adaln",
    )(cond_t, ada_w, ada_b.reshape(depth, 1, width))


def _head_rms(xh, gain):
    return xh * lax.rsqrt(jnp.mean(xh * xh, axis=-1, keepdims=True) + EPS) * gain


def _rope(xh, cos, sin_signed, first_half):
    partner = jnp.where(first_half, pltpu.roll(xh, HEAD_DIM - ROPE_FREQS, axis=1), pltpu.roll(xh, ROPE_FREQS, axis=1))
    return xh * cos + partner * sin_signed


def _rope_axis_tables(n_rows):
    axis_dim = HEAD_DIM // 2
    inv_freq = jnp.power(ROPE_THETA, -jnp.arange(0, axis_dim, 2, dtype=_F32) / axis_dim)
    ang_r = jnp.arange(n_rows, dtype=_F32)[:, None] * inv_freq
    ang_c = jnp.arange(GRID_W, dtype=_F32)[:, None] * inv_freq
    zr, zc = jnp.zeros((n_rows, axis_dim), _F32), jnp.zeros((GRID_W, axis_dim), _F32)
    return (jnp.concatenate([jnp.cos(ang_r), jnp.cos(ang_r), zr], axis=1),
            jnp.concatenate([-jnp.sin(ang_r), jnp.sin(ang_r), zr], axis=1),
            jnp.concatenate([zc, jnp.cos(ang_c), jnp.cos(ang_c)], axis=1),
            jnp.concatenate([zc, -jnp.sin(ang_c), jnp.sin(ang_c)], axis=1))


def _tile_table(row_ref, col_ref):
    col = col_ref[...]
    return jnp.concatenate([jnp.broadcast_to(row_ref[r:r + 1, :], col.shape) + col for r in range(row_ref.shape[0])],
                           axis=0)


def _halo_specs(tile, halo, n, d):
    per_tile = tile // halo
    n_halo_blocks = n // halo
    prev_spec = pl.BlockSpec((1, halo, d), lambda b, i: (b, jnp.maximum(i * per_tile - 1, 0), 0))
    main_spec = pl.BlockSpec((1, tile, d), lambda b, i: (b, i, 0))
    next_spec = pl.BlockSpec((1, halo, d), lambda b, i: (b, jnp.minimum((i + 1) * per_tile, n_halo_blocks - 1), 0))
    return prev_spec, main_spec, next_spec


def _edge_flags():
    i = pl.program_id(1)
    return (i > 0).astype(_F32), (i < pl.num_programs(1) - 1).astype(_F32)


def _in_proj_kernel(xp_ref, x_ref, xn_ref, mod_ref, gain_ref, w_ref, qg_ref, kg_ref, cw_ref,
                    rcos_ref, rsin_ref, ccos_ref, csin_ref, qt_ref, k_ref, vt_ref, conv_ref):
    tile = x_ref.shape[1]
    gain, shift, scale = gain_ref[...], mod_ref[0:1, :], mod_ref[1:2, :]
    prev_ok, next_ok = _edge_flags()
    a_main = _norm_modulate(x_ref[0], gain, shift, scale)
    a_prev = _norm_modulate(xp_ref[0], gain, shift, scale) * prev_ok
    a_next = _norm_modulate(xn_ref[0], gain, shift, scale) * next_ok
    a_ext = jnp.concatenate([a_prev, a_main, a_next], axis=0).astype(_BF16)
    a_mb = a_main.astype(_BF16)

    cos, sin = _tile_table(rcos_ref, ccos_ref), _tile_table(rsin_ref, csin_ref)
    lane = lax.broadcasted_iota(jnp.int32, (1, HEAD_DIM), 1)
    first_half = (lane % (2 * ROPE_FREQS)) < ROPE_FREQS

    o_k = ATTN_WIDTH
    o_v = o_k + KV_WIDTH
    o_b = o_v + KV_WIDTH
    o_c = o_b + CONV_WIDTH

    q = jnp.dot(a_mb, w_ref[:, 0:o_k], preferred_element_type=_F32)
    for h in range(N_Q_HEADS):
        sl = slice(h * HEAD_DIM, (h + 1) * HEAD_DIM)
        qh = _rope(_head_rms(q[:, sl], qg_ref[...]), cos, sin, first_half)
        qt_ref[0, sl, :] = (qh * SCORE_SCALE_LOG2).T.astype(_BF16)

    k = jnp.dot(a_mb, w_ref[:, o_k:o_v], preferred_element_type=_F32)
    for h in range(N_KV_HEADS):
        sl = slice(h * HEAD_DIM, (h + 1) * HEAD_DIM)
        k_ref[0, :, sl] = _rope(_head_rms(k[:, sl], kg_ref[...]), cos, sin, first_half).astype(_BF16)

    vt_ref[0] = jnp.dot(a_mb, w_ref[:, o_v:o_b], preferred_element_type=_F32).T.astype(_BF16)

    gate_b = jnp.dot(a_mb, w_ref[:, o_b:o_c], preferred_element_type=_F32)
    cx = jnp.dot(a_ext, w_ref[:, o_c:], preferred_element_type=_F32)
    u = cx[:, :CONV_WIDTH] * cx[:, CONV_WIDTH:]
    cw = cw_ref[...]
    conv = cw[0:1] * _shift_rows(u, 1) + cw[1:2] * u + cw[2:3] * _shift_rows(u, -1)
    conv_ref[0] = (gate_b * conv[HALO:HALO + tile]).astype(_BF16)


def _in_proj(x, mods, gains, w_in, q_gain, k_gain, conv_w, n_keys):
    b, n, d = x.shape
    tile = TOKEN_TILE
    rows_per_tile = tile // GRID_W
    assert tile % GRID_W == 0
    prev_spec, main_spec, next_spec = _halo_specs(tile, HALO, n, d)
    rope = _rope_axis_tables(n // GRID_W)
    row_table = pl.BlockSpec((rows_per_tile, HEAD_DIM), lambda bi, i: (i, 0))
    col_table = _resident((GRID_W, HEAD_DIM), lambda bi, i: (0, 0))
    return pl.pallas_call(
        _in_proj_kernel,
        grid=(b, n // tile),
        in_specs=[
            prev_spec, main_spec, next_spec,
            _mod_spec(d, 0, lambda bi, i: bi),
            _layer_spec(gains, 0),
            _layer_spec(w_in, 0),
            _layer_spec(q_gain, 0), _layer_spec(k_gain, 0),
            _layer_spec(conv_w, 0),
            row_table, row_table, col_table, col_table,
        ],
        out_specs=[pl.BlockSpec((1, ATTN_WIDTH, tile), lambda bi, i: (bi, 0, i)),
                   pl.BlockSpec((1, tile, KV_WIDTH), lambda bi, i: (bi, i, 0)),
                   pl.BlockSpec((1, KV_WIDTH, tile), lambda bi, i: (bi, 0, i)),
                   pl.BlockSpec((1, tile, CONV_WIDTH), lambda bi, i: (bi, i, 0))],
        out_shape=[jax.ShapeDtypeStruct((b, ATTN_WIDTH, n), _BF16), jax.ShapeDtypeStruct((b, n_keys, KV_WIDTH), _BF16),
                   jax.ShapeDtypeStruct((b, KV_WIDTH, n_keys), _BF16), jax.ShapeDtypeStruct((b, n, CONV_WIDTH), _BF16)],
        compiler_params=_compiler_params(2),
        name="in_proj",
    )(x, x, x, mods, gains, w_in, q_gain, k_gain, conv_w, rope[0], rope[1], rope[2], rope[3])


def _ctx_kv_kernel(ctx_ref, mod_ref, gain_ref, w_ref, kg_ref, k_in_ref, vt_in_ref, k_ref, vt_ref):
    del k_in_ref, vt_in_ref
    a = _norm_modulate(ctx_ref[0], gain_ref[...], mod_ref[0:1, :], mod_ref[1:2, :]).astype(_BF16)
    kv = jnp.dot(a, w_ref[...], preferred_element_type=_F32)
    for h in range(N_KV_HEADS):
        sl = slice(h * HEAD_DIM, (h + 1) * HEAD_DIM)
        k_ref[0, :, sl] = _head_rms(kv[:, sl], kg_ref[...]).astype(_BF16)
    vt_ref[0] = kv[:, KV_WIDTH:].T.astype(_BF16)


def _ctx_kv(ctx, mods, gains, w_in, k_gain, k_all, vt_all, n_latent):
    b, lc, d = ctx.shape
    assert n_latent % lc == 0 and 2 * KV_WIDTH == ATTN_WIDTH
    ctx_block = n_latent // lc
    return pl.pallas_call(
        _ctx_kv_kernel,
        grid=(b,),
        in_specs=[
            pl.BlockSpec((1, lc, d), lambda i: (i, 0, 0)),
            _mod_spec(d, 0, lambda i: b),
            _layer_spec(gains, 0),
            _resident((None, d, 2 * KV_WIDTH), lambda i: (0, 0, 1)),
            _layer_spec(k_gain, 0),
            pl.BlockSpec(memory_space=pl.ANY), pl.BlockSpec(memory_space=pl.ANY),
        ],
        out_specs=[pl.BlockSpec((1, lc, KV_WIDTH), lambda i: (i, ctx_block, 0)),
                   pl.BlockSpec((1, KV_WIDTH, lc), lambda i: (i, 0, ctx_block))],
        out_shape=[jax.ShapeDtypeStruct(k_all.shape, _BF16), jax.ShapeDtypeStruct(vt_all.shape, _BF16)],
        input_output_aliases={5: 0, 6: 1},
        compiler_params=_compiler_params(1),
        name="ctx_kv",
    )(ctx, mods, gains, w_in, k_gain, k_all, vt_all)


def _attn_kernel(qt_ref, k_ref, vt_ref, conv_ref, x_ref, mod_ref, wo_ref, o_ref,
                 s_ref, cmax_ref, m_ref, acc_ref, attn_ref):
    n_chunks = k_ref.shape[1] // KV_TILE

    def keys_of(j):
        if isinstance(j, int):
            return slice(j * KV_TILE, (j + 1) * KV_TILE)
        return pl.ds(pl.multiple_of(j * KV_TILE, KV_TILE), KV_TILE)

    def score_stage(j, head):
        g = head // Q_PER_KV
        kch = k_ref[0, keys_of(j), g * HEAD_DIM:(g + 1) * HEAD_DIM]
        qt = qt_ref[0, head * HEAD_DIM:(head + 1) * HEAD_DIM, :]
        s = jnp.dot(kch, qt, preferred_element_type=_F32)
        s_ref[head % 2] = s
        cmax_ref[head % 2] = jnp.max(s, axis=0, keepdims=True)

    def value_stage(j, head):
        g = head // Q_PER_KV
        vtch = vt_ref[0, g * HEAD_DIM:(g + 1) * HEAD_DIM, keys_of(j)]
        v_aug = jnp.concatenate([vtch, jnp.ones((ONES_ROWS, KV_TILE), _BF16)], axis=0)
        m_prev = m_ref[head]
        m_new = jnp.maximum(m_prev, cmax_ref[head % 2])
        alpha = jnp.exp2(m_prev - m_new)
        p = jnp.exp2(s_ref[head % 2] - m_new).astype(_BF16)
        acc_ref[head] = alpha * acc_ref[head] + jnp.dot(v_aug, p, preferred_element_type=_F32)
        m_ref[head] = m_new

    def chunk(j, last):
        for head in range(N_Q_HEADS):
            if head + 1 < N_Q_HEADS:
                score_stage(j, head + 1)
            elif not last:
                score_stage(j + 1, 0)
            value_stage(j, head)

    m_ref[...] = jnp.full(m_ref.shape, -jnp.inf, _F32)
    acc_ref[...] = jnp.zeros(acc_ref.shape, _F32)
    score_stage(0, 0)

    def body(j, carry):
        chunk(j, last=False)
        return carry

    lax.fori_loop(0, n_chunks - 1, body, 0)
    chunk(n_chunks - 1, last=True)
    for head in range(N_Q_HEADS):
        acc = acc_ref[head]
        out_t = acc[:HEAD_DIM] / acc[HEAD_DIM:HEAD_DIM + 1]
        attn_ref[:, head * HEAD_DIM:(head + 1) * HEAD_DIM] = out_t.T.astype(_BF16)

    mixed = jnp.concatenate([attn_ref[...], conv_ref[0]], axis=-1)
    y = jnp.dot(mixed, wo_ref[...], preferred_element_type=_F32)
    o_ref[0] = x_ref[0] + mod_ref[2:3, :] * y


def _attn_out(qt, k_all, vt_all, conv, x, mods, w_out):
    b, n, d = x.shape
    n_keys = k_all.shape[1]
    assert n_keys % KV_TILE == 0
    tq = Q_TILE
    per_batch = lambda rows, width: pl.BlockSpec((1, rows, width), lambda bi, i: (bi, 0, 0))
    tile = lambda width: pl.BlockSpec((1, tq, width), lambda bi, i: (bi, i, 0))
    return pl.pallas_call(
        _attn_kernel,
        grid=(b, n // tq),
        in_specs=[
            pl.BlockSpec((1, ATTN_WIDTH, tq), lambda bi, i: (bi, 0, i)),
            per_batch(n_keys, KV_WIDTH), per_batch(KV_WIDTH, n_keys),
            tile(CONV_WIDTH),
            tile(d),
            _mod_spec(d, 0, lambda bi, i: bi),
            _layer_spec(w_out, 0),
        ],
        out_specs=tile(d),
        out_shape=jax.ShapeDtypeStruct((b, n, d), _F32),
        scratch_shapes=[
            pltpu.VMEM((2, KV_TILE, tq), _F32),
            pltpu.VMEM((2, 1, tq), _F32),
            pltpu.VMEM((N_Q_HEADS, 1, tq), _F32),
            pltpu.VMEM((N_Q_HEADS, HEAD_DIM + ONES_ROWS, tq), _F32),
            pltpu.VMEM((tq, ATTN_WIDTH), _BF16),
        ],
        compiler_params=_compiler_params(2),
        name="attn_out",
    )(qt, k_all, vt_all, conv, x, mods, w_out)


def _ffn_kernel(xp_ref, x_ref, xn_ref, mod_ref, gain_ref, wup_ref, cwb_ref, wd_ref, o_ref, h_ref):
    tile = x_ref.shape[1]
    gain, shift, scale = gain_ref[...], mod_ref[3:4, :], mod_ref[4:5, :]
    x = x_ref[0]
    prev_ok, next_ok = _edge_flags()
    f_main = _norm_modulate(x, gain, shift, scale)
    f_prev = _norm_modulate(xp_ref[0], gain, shift, scale) * prev_ok
    f_next = _norm_modulate(xn_ref[0], gain, shift, scale) * next_ok
    f_ext = jnp.concatenate([f_prev, f_main, f_next], axis=0).astype(_BF16)
    f_mb = f_main.astype(_BF16)
    n_chunks = D_FF // FF_CHUNK

    def up(c):
        cols = slice(c * FF_CHUNK, (c + 1) * FF_CHUNK)
        val_cols = slice(D_FF + c * FF_CHUNK, D_FF + (c + 1) * FF_CHUNK)
        return (jnp.dot(f_ext, wup_ref[:, cols], preferred_element_type=_F32),
                jnp.dot(f_mb, wup_ref[:, val_cols], preferred_element_type=_F32))

    def act(c, gate, val):
        cols = slice(c * FF_CHUNK, (c + 1) * FF_CHUNK)
        cwb = cwb_ref[:, cols]
        conv = cwb[0:1] * _shift_rows(gate, 1) + cwb[1:2] * gate + cwb[2:3] * _shift_rows(gate, -1)
        h_ref[:, cols] = (_silu(conv[HALO:HALO + tile] + cwb[3:4]) * val).astype(_BF16)

    nxt = up(0)
    for c in range(n_chunks):
        cur = nxt
        if c + 1 < n_chunks:
            nxt = up(c + 1)
        act(c, *cur)
    y = jnp.dot(h_ref[...], wd_ref[...], preferred_element_type=_F32)
    o_ref[0] = x + mod_ref[5:6, :] * y


def _ffn(x, mods, gains, w_up, conv_wb, w_down, layer):
    b, n, d = x.shape
    tile = TOKEN_TILE
    prev_spec, main_spec, next_spec = _halo_specs(tile, HALO, n, d)
    return pl.pallas_call(
        _ffn_kernel,
        grid=(b, n // tile),
        in_specs=[
            prev_spec, main_spec, next_spec,
            _mod_spec(d, layer, lambda bi, i: bi),
            _layer_spec(gains, layer),
            _layer_spec(w_up, layer), _layer_spec(conv_wb, layer), _layer_spec(w_down, layer),
        ],
        out_specs=main_spec,
        out_shape=jax.ShapeDtypeStruct((b, n, d), _F32),
        scratch_shapes=[pltpu.VMEM((tile, D_FF), _BF16)],
        compiler_params=_compiler_params(2),
        name="ffn",
    )(x, x, x, mods, gains, w_up, conv_wb, w_down)


def _pool_kernel(xp_ref, x_ref, xn_ref, mod_ref, gain_ref, pw_ref, ps_ref, o_ref):
    tile = x_ref.shape[1]
    halo = xp_ref.shape[1]
    n_total = tile * pl.num_programs(1)
    gain, shift, scale = gain_ref[...], mod_ref[0:1, :], mod_ref[1:2, :]
    x = x_ref[0]
    prev_ok, next_ok = _edge_flags()
    a_prev = _norm_modulate(xp_ref[0], gain, shift, scale) * prev_ok
    a_main = _norm_modulate(x, gain, shift, scale)
    a_next = _norm_modulate(xn_ref[0], gain, shift, scale) * next_ok
    a_ext = jnp.concatenate([a_prev, a_main, a_next], axis=0)
    t = pl.program_id(1) * tile + lax.broadcasted_iota(jnp.int32, (tile, 1), 0)

    mixed = []
    for gi, w in enumerate(POOL_WINDOWS):
        a = a_ext[:, gi * POOL_GROUP:(gi + 1) * POOL_GROUP]
        s = _shift_rows(a, 1) + a
        k = 2
        while k < w:
            s = _shift_rows(s, k // 2) + _shift_rows(s, -(k // 2))
            k *= 2
        count = jnp.minimum(t + w - w // 2, n_total) - jnp.maximum(t - w // 2, 0)
        pooled = s[halo:halo + tile] / count.astype(_F32) - a[halo:halo + tile]
        mixed.append(jnp.dot(pooled.astype(_BF16), pw_ref[gi], preferred_element_type=_F32))
    y = jnp.concatenate(mixed, axis=-1) * ps_ref[...]
    o_ref[0] = x + mod_ref[2:3, :] * y


def _pool(x, mods, gains, pool_w, pool_scale, layer):
    b, n, d = x.shape
    tile = TOKEN_TILE
    prev_spec, main_spec, next_spec = _halo_specs(tile, POOL_HALO, n, d)
    return pl.pallas_call(
        _pool_kernel,
        grid=(b, n // tile),
        in_specs=[
            prev_spec, main_spec, next_spec,
            _mod_spec(d, layer, lambda bi, i: bi),
            _layer_spec(gains, layer),
            _layer_spec(pool_w, layer // 2),
            _layer_spec(pool_scale, layer // 2),
        ],
        out_specs=main_spec,
        out_shape=jax.ShapeDtypeStruct((b, n, d), _F32),
        compiler_params=_compiler_params(2),
        name="pool",
    )(x, x, x, mods, gains, pool_w, pool_scale)


def kernel(x, c, ctx, c_ctx, ada_w, ada_b, mix_norm, ffn_norm, even_w_in, even_q_gain, even_k_gain, even_conv_w,
           even_w_out, odd_pool_w, odd_pool_scale, ffn_w_up, ffn_conv_w, ffn_conv_b, ffn_w_down):
    b, n, d = x.shape
    depth = ada_w.shape[0]
    lc = ctx.shape[1]
    assert depth == 2 and d == D_MODEL and n % TOKEN_TILE == 0 and n % Q_TILE == 0

    cond = jnp.concatenate([c, c_ctx[None, :], jnp.zeros((SUBLANES - b - 1, d), _F32)], axis=0)
    mods = _adaln(cond.T, ada_w, ada_b).reshape(depth, SUBLANES, N_MOD, d)

    rows = lambda v: v.reshape(v.shape[0], 1, v.shape[-1])
    mix_gain, ffn_gain = rows(mix_norm), rows(ffn_norm)
    w_in, w_out, pool_w = even_w_in.astype(_BF16), even_w_out.astype(_BF16), odd_pool_w.astype(_BF16)
    w_up, w_down = ffn_w_up.astype(_BF16), ffn_w_down.astype(_BF16)
    conv_wb = jnp.concatenate([ffn_conv_w, ffn_conv_b[:, None, :],
                               jnp.zeros((depth, SUBLANES - ffn_conv_w.shape[1] - 1, D_FF), _F32)], axis=1)

    qt, k_all, vt_all, conv = _in_proj(x, mods, mix_gain, w_in, rows(even_q_gain), rows(even_k_gain),
                                       even_conv_w, n + lc)
    k_all, vt_all = _ctx_kv(ctx, mods, mix_gain, w_in, rows(even_k_gain), k_all, vt_all, n)
    x = _attn_out(qt, k_all, vt_all, conv, x, mods, w_out)
    x = _ffn(x, mods, ffn_gain, w_up, conv_wb, w_down, 0)

    x = _pool(x, mods, mix_gain, pool_w, rows(odd_pool_scale), 1)
    x = _ffn(x, mods, ffn_gain, w_up, conv_wb, w_down, 1)
    return x
```

```python
import jax
import jax.numpy as jnp
from jax import lax
from jax.experimental import pallas as pl
from jax.experimental.pallas import tpu as pltpu

D_MODEL = 1024
GRID_W = 64
HEAD_DIM = 128
N_Q_HEADS = 4
N_KV_HEADS = 2
Q_PER_KV = N_Q_HEADS // N_KV_HEADS
ATTN_WIDTH = N_Q_HEADS * HEAD_DIM
KV_WIDTH = N_KV_HEADS * HEAD_DIM
CONV_WIDTH = D_MODEL - ATTN_WIDTH
IN_PROJ_WIDTH = ATTN_WIDTH + 2 * KV_WIDTH + 3 * CONV_WIDTH
ROPE_THETA = 10000.0
ROPE_FREQS = HEAD_DIM // 4
SCORE_SCALE_LOG2 = HEAD_DIM ** -0.5 * 1.4426950408889634
POOL_WINDOWS = (2, 4, 8, 16)
POOL_GROUP = D_MODEL // len(POOL_WINDOWS)
D_FF = 2816
N_MOD = 6
EPS = 1e-6

SUBLANES = 8
VMEM_LIMIT_BYTES = 56 * 1024 * 1024

HALO = SUBLANES
POOL_HALO = 2 * SUBLANES
TOKEN_TILE = 512
Q_TILE = 512
KV_TILE = 768
ONES_ROWS = 16
FF_CHUNK = 256
ADALN_COLS = 1536

_BF16 = jnp.bfloat16
_F32 = jnp.float32


def _compiler_params(n_axes):
    return pltpu.CompilerParams(dimension_semantics=("arbitrary",) * n_axes,
                                vmem_limit_bytes=VMEM_LIMIT_BYTES)


def _resident(shape, index_map):
    return pl.BlockSpec(shape, index_map, pipeline_mode=pl.Buffered(1))


def _layer_spec(arr, layer):
    zeros = (0,) * (arr.ndim - 1)
    return _resident((None,) + arr.shape[1:], lambda *_: (layer,) + zeros)


def _mod_spec(d, layer, row_of):
    return pl.BlockSpec((None, None, N_MOD, d), lambda *ids: (layer, row_of(*ids), 0, 0))


def _norm_modulate(x, gain, shift, scale):
    y = x * lax.rsqrt(jnp.mean(x * x, axis=-1, keepdims=True) + EPS)
    return y * (gain * (1.0 + scale)) + shift


def _silu(x):
    return x * (1.0 / (1.0 + jnp.exp(-x)))


def _shift_rows(u, k):
    return pltpu.roll(u, k % u.shape[0], axis=0)


def _adaln_kernel(ct_ref, w_ref, b_ref, o_ref):
    s = _silu(ct_ref[...])
    w = w_ref[0]
    n_vec = 3
    rows = [jnp.sum(s[:, r:r + 1] * w, axis=0, keepdims=True) + b_ref[0] for r in range(n_vec)]
    rows.append(jnp.zeros((SUBLANES - n_vec, w.shape[1]), _F32))
    o_ref[0] = jnp.concatenate(rows, axis=0)


def _adaln(cond_t, ada_w, ada_b):
    depth, d, width = ada_w.shape
    return pl.pallas_call(
        _adaln_kernel,
        grid=(depth, width // ADALN_COLS),
        in_specs=[
            pl.BlockSpec((d, SUBLANES), lambda l, j: (0, 0)),
            pl.BlockSpec((1, d, ADALN_COLS), lambda l, j: (l, 0, j)),
            pl.BlockSpec((1, 1, ADALN_COLS), lambda l, j: (l, 0, j)),
        ],
        out_specs=pl.BlockSpec((1, SUBLANES, ADALN_COLS), lambda l, j: (l, 0, j)),
        out_shape=jax.ShapeDtypeStruct((depth, SUBLANES, width), _F32),
        compiler_params=_compiler_params(2),
        name="adaln",
    )(cond_t, ada_w, ada_b.reshape(depth, 1, width))


def _head_rms(xh, gain):
    return xh * lax.rsqrt(jnp.mean(xh * xh, axis=-1, keepdims=True) + EPS) * gain


def _rope(xh, cos, sin_signed, first_half):
    partner = jnp.where(first_half, pltpu.roll(xh, HEAD_DIM - ROPE_FREQS, axis=1), pltpu.roll(xh, ROPE_FREQS, axis=1))
    return xh * cos + partner * sin_signed


def _rope_axis_tables(n_rows):
    axis_dim = HEAD_DIM // 2
    inv_freq = jnp.power(ROPE_THETA, -jnp.arange(0, axis_dim, 2, dtype=_F32) / axis_dim)
    ang_r = jnp.arange(n_rows, dtype=_F32)[:, None] * inv_freq
    ang_c = jnp.arange(GRID_W, dtype=_F32)[:, None] * inv_freq
    zr, zc = jnp.zeros((n_rows, axis_dim), _F32), jnp.zeros((GRID_W, axis_dim), _F32)
    return (jnp.concatenate([jnp.cos(ang_r), jnp.cos(ang_r), zr], axis=1),
            jnp.concatenate([-jnp.sin(ang_r), jnp.sin(ang_r), zr], axis=1),
            jnp.concatenate([zc, jnp.cos(ang_c), jnp.cos(ang_c)], axis=1),
            jnp.concatenate([zc, -jnp.sin(ang_c), jnp.sin(ang_c)], axis=1))


def _tile_table(row_ref, col_ref):
    col = col_ref[...]
    return jnp.concatenate([jnp.broadcast_to(row_ref[r:r + 1, :], col.shape) + col for r in range(row_ref.shape[0])],
                           axis=0)


def _halo_specs(tile, halo, n, d):
    per_tile = tile // halo
    n_halo_blocks = n // halo
    prev_spec = pl.BlockSpec((1, halo, d), lambda b, i: (b, jnp.maximum(i * per_tile - 1, 0), 0))
    main_spec = pl.BlockSpec((1, tile, d), lambda b, i: (b, i, 0))
    next_spec = pl.BlockSpec((1, halo, d), lambda b, i: (b, jnp.minimum((i + 1) * per_tile, n_halo_blocks - 1), 0))
    return prev_spec, main_spec, next_spec


def _edge_flags():
    i = pl.program_id(1)
    return (i > 0).astype(_F32), (i < pl.num_programs(1) - 1).astype(_F32)


def _in_proj_kernel(xp_ref, x_ref, xn_ref, mod_ref, gain_ref, w_ref, qg_ref, kg_ref, cw_ref,
                    rcos_ref, rsin_ref, ccos_ref, csin_ref, qt_ref, k_ref, vt_ref, conv_ref):
    tile = x_ref.shape[1]
    gain, shift, scale = gain_ref[...], mod_ref[0:1, :], mod_ref[1:2, :]
    prev_ok, next_ok = _edge_flags()
    a_main = _norm_modulate(x_ref[0], gain, shift, scale)
    a_prev = _norm_modulate(xp_ref[0], gain, shift, scale) * prev_ok
    a_next = _norm_modulate(xn_ref[0], gain, shift, scale) * next_ok
    a_ext = jnp.concatenate([a_prev, a_main, a_next], axis=0).astype(_BF16)
    a_mb = a_main.astype(_BF16)

    cos, sin = _tile_table(rcos_ref, ccos_ref), _tile_table(rsin_ref, csin_ref)
    lane = lax.broadcasted_iota(jnp.int32, (1, HEAD_DIM), 1)
    first_half = (lane % (2 * ROPE_FREQS)) < ROPE_FREQS

    o_k = ATTN_WIDTH
    o_v = o_k + KV_WIDTH
    o_b = o_v + KV_WIDTH
    o_c = o_b + CONV_WIDTH

    q = jnp.dot(a_mb, w_ref[:, 0:o_k], preferred_element_type=_F32)
    k = jnp.dot(a_mb, w_ref[:, o_k:o_v], preferred_element_type=_F32)
    for h in range(N_Q_HEADS):
        sl = slice(h * HEAD_DIM, (h + 1) * HEAD_DIM)
        qh = _rope(_head_rms(q[:, sl], qg_ref[...]), cos, sin, first_half)
        qt_ref[0, sl, :] = (qh * SCORE_SCALE_LOG2).T.astype(_BF16)

    v = jnp.dot(a_mb, w_ref[:, o_v:o_b], preferred_element_type=_F32)
    for h in range(N_KV_HEADS):
        sl = slice(h * HEAD_DIM, (h + 1) * HEAD_DIM)
        k_ref[0, :, sl] = _rope(_head_rms(k[:, sl], kg_ref[...]), cos, sin, first_half).astype(_BF16)

    cx = jnp.dot(a_ext, w_ref[:, o_c:], preferred_element_type=_F32)
    vt_ref[0] = v.T.astype(_BF16)
    gate_b = jnp.dot(a_mb, w_ref[:, o_b:o_c], preferred_element_type=_F32)
    u = cx[:, :CONV_WIDTH] * cx[:, CONV_WIDTH:]
    cw = cw_ref[...]
    conv = cw[0:1] * _shift_rows(u, 1) + cw[1:2] * u + cw[2:3] * _shift_rows(u, -1)
    conv_ref[0] = (gate_b * conv[HALO:HALO + tile]).astype(_BF16)


def _in_proj(x, mods, gains, w_in, q_gain, k_gain, conv_w, n_keys):
    b, n, d = x.shape
    tile = TOKEN_TILE
    rows_per_tile = tile // GRID_W
    assert tile % GRID_W == 0
    prev_spec, main_spec, next_spec = _halo_specs(tile, HALO, n, d)
    rope = _rope_axis_tables(n // GRID_W)
    row_table = pl.BlockSpec((rows_per_tile, HEAD_DIM), lambda bi, i: (i, 0))
    col_table = _resident((GRID_W, HEAD_DIM), lambda bi, i: (0, 0))
    return pl.pallas_call(
        _in_proj_kernel,
        grid=(b, n // tile),
        in_specs=[
            prev_spec, main_spec, next_spec,
            _mod_spec(d, 0, lambda bi, i: bi),
            _layer_spec(gains, 0),
            _layer_spec(w_in, 0),
            _layer_spec(q_gain, 0), _layer_spec(k_gain, 0),
            _layer_spec(conv_w, 0),
            row_table, row_table, col_table, col_table,
        ],
        out_specs=[pl.BlockSpec((1, ATTN_WIDTH, tile), lambda bi, i: (bi, 0, i)),
                   pl.BlockSpec((1, tile, KV_WIDTH), lambda bi, i: (bi, i, 0)),
                   pl.BlockSpec((1, KV_WIDTH, tile), lambda bi, i: (bi, 0, i)),
                   pl.BlockSpec((1, tile, CONV_WIDTH), lambda bi, i: (bi, i, 0))],
        out_shape=[jax.ShapeDtypeStruct((b, ATTN_WIDTH, n), _BF16), jax.ShapeDtypeStruct((b, n_keys, KV_WIDTH), _BF16),
                   jax.ShapeDtypeStruct((b, KV_WIDTH, n_keys), _BF16), jax.ShapeDtypeStruct((b, n, CONV_WIDTH), _BF16)],
        compiler_params=_compiler_params(2),
        name="in_proj",
    )(x, x, x, mods, gains, w_in, q_gain, k_gain, conv_w, rope[0], rope[1], rope[2], rope[3])


def _ctx_kv_kernel(ctx_ref, mod_ref, gain_ref, w_ref, kg_ref, k_in_ref, vt_in_ref, k_ref, vt_ref):
    del k_in_ref, vt_in_ref
    a = _norm_modulate(ctx_ref[0], gain_ref[...], mod_ref[0:1, :], mod_ref[1:2, :]).astype(_BF16)
    kv = jnp.dot(a, w_ref[...], preferred_element_type=_F32)
    for h in range(N_KV_HEADS):
        sl = slice(h * HEAD_DIM, (h + 1) * HEAD_DIM)
        k_ref[0, :, sl] = _head_rms(kv[:, sl], kg_ref[...]).astype(_BF16)
    vt_ref[0] = kv[:, KV_WIDTH:].T.astype(_BF16)


def _ctx_kv(ctx, mods, gains, w_in, k_gain, k_all, vt_all, n_latent):
    b, lc, d = ctx.shape
    assert n_latent % lc == 0 and 2 * KV_WIDTH == ATTN_WIDTH
    ctx_block = n_latent // lc
    return pl.pallas_call(
        _ctx_kv_kernel,
        grid=(b,),
        in_specs=[
            pl.BlockSpec((1, lc, d), lambda i: (i, 0, 0)),
            _mod_spec(d, 0, lambda i: b),
            _layer_spec(gains, 0),
            _resident((None, d, 2 * KV_WIDTH), lambda i: (0, 0, 1)),
            _layer_spec(k_gain, 0),
            pl.BlockSpec(memory_space=pl.ANY), pl.BlockSpec(memory_space=pl.ANY),
        ],
        out_specs=[pl.BlockSpec((1, lc, KV_WIDTH), lambda i: (i, ctx_block, 0)),
                   pl.BlockSpec((1, KV_WIDTH, lc), lambda i: (i, 0, ctx_block))],
        out_shape=[jax.ShapeDtypeStruct(k_all.shape, _BF16), jax.ShapeDtypeStruct(vt_all.shape, _BF16)],
        input_output_aliases={5: 0, 6: 1},
        compiler_params=_compiler_params(1),
        name="ctx_kv",
    )(ctx, mods, gains, w_in, k_gain, k_all, vt_all)


def _attn_kernel(qt_ref, k_ref, vt_ref, conv_ref, x_ref, mod_ref, wo_ref, o_ref,
                 s_ref, cmax_ref, m_ref, acc_ref, attn_ref):
    n_chunks = k_ref.shape[1] // KV_TILE

    def keys_of(j):
        if isinstance(j, int):
            return slice(j * KV_TILE, (j + 1) * KV_TILE)
        return pl.ds(pl.multiple_of(j * KV_TILE, KV_TILE), KV_TILE)

    def score_stage(j, head):
        g = head // Q_PER_KV
        kch = k_ref[0, keys_of(j), g * HEAD_DIM:(g + 1) * HEAD_DIM]
        qt = qt_ref[0, head * HEAD_DIM:(head + 1) * HEAD_DIM, :]
        s = jnp.dot(kch, qt, preferred_element_type=_F32)
        s_ref[head % 2] = s
        cmax_ref[head % 2] = jnp.max(s, axis=0, keepdims=True)

    def value_stage(j, head):
        g = head // Q_PER_KV
        vtch = vt_ref[0, g * HEAD_DIM:(g + 1) * HEAD_DIM, keys_of(j)]
        v_aug = jnp.concatenate([vtch, jnp.ones((ONES_ROWS, KV_TILE), _BF16)], axis=0)
        m_prev = m_ref[head]
        m_new = jnp.maximum(m_prev, cmax_ref[head % 2])
        alpha = jnp.exp2(m_prev - m_new)
        p = jnp.exp2(s_ref[head % 2] - m_new).astype(_BF16)
        acc_ref[head] = alpha * acc_ref[head] + jnp.dot(v_aug, p, preferred_element_type=_F32)
        m_ref[head] = m_new

    def chunk(j, last):
        for head in range(N_Q_HEADS):
            if head + 1 < N_Q_HEADS:
                score_stage(j, head + 1)
            elif not last:
                score_stage(j + 1, 0)
            value_stage(j, head)

    m_ref[...] = jnp.full(m_ref.shape, -jnp.inf, _F32)
    acc_ref[...] = jnp.zeros(acc_ref.shape, _F32)
    score_stage(0, 0)

    def body(j, carry):
        chunk(j, last=False)
        return carry

    lax.fori_loop(0, n_chunks - 1, body, 0, unroll=5)
    chunk(n_chunks - 1, last=True)
    for head in range(N_Q_HEADS):
        acc = acc_ref[head]
        out_t = acc[:HEAD_DIM] / acc[HEAD_DIM:HEAD_DIM + 1]
        attn_ref[:, head * HEAD_DIM:(head + 1) * HEAD_DIM] = out_t.T.astype(_BF16)

    mixed = jnp.concatenate([attn_ref[...], conv_ref[0]], axis=-1)
    y = jnp.dot(mixed, wo_ref[...], preferred_element_type=_F32)
    o_ref[0] = x_ref[0] + mod_ref[2:3, :] * y


def _attn_out(qt, k_all, vt_all, conv, x, mods, w_out):
    b, n, d = x.shape
    n_keys = k_all.shape[1]
    assert n_keys % KV_TILE == 0
    tq = Q_TILE
    per_batch = lambda rows, width: pl.BlockSpec((1, rows, width), lambda bi, i: (bi, 0, 0))
    tile = lambda width: pl.BlockSpec((1, tq, width), lambda bi, i: (bi, i, 0))
    return pl.pallas_call(
        _attn_kernel,
        grid=(b, n // tq),
        in_specs=[
            pl.BlockSpec((1, ATTN_WIDTH, tq), lambda bi, i: (bi, 0, i)),
            per_batch(n_keys, KV_WIDTH), per_batch(KV_WIDTH, n_keys),
            tile(CONV_WIDTH),
            tile(d),
            _mod_spec(d, 0, lambda bi, i: bi),
            _layer_spec(w_out, 0),
        ],
        out_specs=tile(d),
        out_shape=jax.ShapeDtypeStruct((b, n, d), _F32),
        scratch_shapes=[
            pltpu.VMEM((2, KV_TILE, tq), _F32),
            pltpu.VMEM((2, 1, tq), _F32),
            pltpu.VMEM((N_Q_HEADS, 1, tq), _F32),
            pltpu.VMEM((N_Q_HEADS, HEAD_DIM + ONES_ROWS, tq), _F32),
            pltpu.VMEM((tq, ATTN_WIDTH), _BF16),
        ],
        compiler_params=_compiler_params(2),
        name="attn_out",
    )(qt, k_all, vt_all, conv, x, mods, w_out)


def _ffn_kernel(xp_ref, x_ref, xn_ref, mod_ref, gain_ref, wup_ref, cwb_ref, wd_ref, o_ref, h_ref):
    tile = x_ref.shape[1]
    gain, shift, scale = gain_ref[...], mod_ref[3:4, :], mod_ref[4:5, :]
    x = x_ref[0]
    prev_ok, next_ok = _edge_flags()
    f_main = _norm_modulate(x, gain, shift, scale)
    f_prev = _norm_modulate(xp_ref[0], gain, shift, scale) * prev_ok
    f_next = _norm_modulate(xn_ref[0], gain, shift, scale) * next_ok
    f_ext = jnp.concatenate([f_prev, f_main, f_next], axis=0).astype(_BF16)
    f_mb = f_main.astype(_BF16)
    n_chunks = D_FF // FF_CHUNK

    def up(c):
        cols = slice(c * FF_CHUNK, (c + 1) * FF_CHUNK)
        val_cols = slice(D_FF + c * FF_CHUNK, D_FF + (c + 1) * FF_CHUNK)
        return (jnp.dot(f_ext, wup_ref[:, cols], preferred_element_type=_F32),
                jnp.dot(f_mb, wup_ref[:, val_cols], preferred_element_type=_F32))

    def act(c, gate, val):
        cols = slice(c * FF_CHUNK, (c + 1) * FF_CHUNK)
        cwb = cwb_ref[:, cols]
        conv = cwb[0:1] * _shift_rows(gate, 1) + cwb[1:2] * gate + cwb[2:3] * _shift_rows(gate, -1)
        h_ref[:, cols] = (_silu(conv[HALO:HALO + tile] + cwb[3:4]) * val).astype(_BF16)

    nxt = up(0)
    for c in range(n_chunks):
        cur = nxt
        if c + 1 < n_chunks:
            nxt = up(c + 1)
        act(c, *cur)
    y = jnp.dot(h_ref[...], wd_ref[...], preferred_element_type=_F32)
    o_ref[0] = x + mod_ref[5:6, :] * y


def _ffn(x, mods, gains, w_up, conv_wb, w_down, layer):
    b, n, d = x.shape
    tile = TOKEN_TILE
    prev_spec, main_spec, next_spec = _halo_specs(tile, HALO, n, d)
    return pl.pallas_call(
        _ffn_kernel,
        grid=(b, n // tile),
        in_specs=[
            prev_spec, main_spec, next_spec,
            _mod_spec(d, layer, lambda bi, i: bi),
            _layer_spec(gains, layer),
            _layer_spec(w_up, layer), _layer_spec(conv_wb, layer), _layer_spec(w_down, layer),
        ],
        out_specs=main_spec,
        out_shape=jax.ShapeDtypeStruct((b, n, d), _F32),
        scratch_shapes=[pltpu.VMEM((tile, D_FF), _BF16)],
        compiler_params=_compiler_params(2),
        name="ffn",
    )(x, x, x, mods, gains, w_up, conv_wb, w_down)


def _pool_kernel(xp_ref, x_ref, xn_ref, mod_ref, gain_ref, pw_ref, ps_ref, o_ref):
    tile = x_ref.shape[1]
    halo = xp_ref.shape[1]
    n_total = tile * pl.num_programs(1)
    gain, shift, scale = gain_ref[...], mod_ref[0:1, :], mod_ref[1:2, :]
    x = x_ref[0]
    prev_ok, next_ok = _edge_flags()
    a_prev = _norm_modulate(xp_ref[0], gain, shift, scale) * prev_ok
    a_main = _norm_modulate(x, gain, shift, scale)
    a_next = _norm_modulate(xn_ref[0], gain, shift, scale) * next_ok
    a_ext = jnp.concatenate([a_prev, a_main, a_next], axis=0)
    t = pl.program_id(1) * tile + lax.broadcasted_iota(jnp.int32, (tile, 1), 0)

    mixed = []
    for gi, w in enumerate(POOL_WINDOWS):
        a = a_ext[:, gi * POOL_GROUP:(gi + 1) * POOL_GROUP]
        s = _shift_rows(a, 1) + a
        k = 2
        while k < w:
            s = _shift_rows(s, k // 2) + _shift_rows(s, -(k // 2))
            k *= 2
        count = jnp.minimum(t + w - w // 2, n_total) - jnp.maximum(t - w // 2, 0)
        pooled = s[halo:halo + tile] * (1.0 / count.astype(_F32)) - a[halo:halo + tile]
        mixed.append(jnp.dot(pooled.astype(_BF16), pw_ref[gi], preferred_element_type=_F32))
    o_ref[0] = x + jnp.concatenate(mixed, axis=-1) * (ps_ref[...] * mod_ref[2:3, :])


def _pool(x, mods, gains, pool_w, pool_scale, layer):
    b, n, d = x.shape
    tile = TOKEN_TILE
    prev_spec, main_spec, next_spec = _halo_specs(tile, POOL_HALO, n, d)
    return pl.pallas_call(
        _pool_kernel,
        grid=(b, n // tile),
        in_specs=[
            prev_spec, main_spec, next_spec,
            _mod_spec(d, layer, lambda bi, i: bi),
            _layer_spec(gains, layer),
            _layer_spec(pool_w, layer // 2),
            _layer_spec(pool_scale, layer // 2),
        ],
        out_specs=main_spec,
        out_shape=jax.ShapeDtypeStruct((b, n, d), _F32),
        compiler_params=_compiler_params(2),
        name="pool",
    )(x, x, x, mods, gains, pool_w, pool_scale)


def kernel(x, c, ctx, c_ctx, ada_w, ada_b, mix_norm, ffn_norm, even_w_in, even_q_gain, even_k_gain, even_conv_w,
           even_w_out, odd_pool_w, odd_pool_scale, ffn_w_up, ffn_conv_w, ffn_conv_b, ffn_w_down):
    b, n, d = x.shape
    depth = ada_w.shape[0]
    lc = ctx.shape[1]
    assert depth == 2 and d == D_MODEL and n % TOKEN_TILE == 0 and n % Q_TILE == 0

    cond = jnp.concatenate([c, c_ctx[None, :], jnp.zeros((SUBLANES - b - 1, d), _F32)], axis=0)
    mods = _adaln(cond.T, ada_w, ada_b).reshape(depth, SUBLANES, N_MOD, d)

    rows = lambda v: v.reshape(v.shape[0], 1, v.shape[-1])
    mix_gain, ffn_gain = rows(mix_norm), rows(ffn_norm)
    w_in, w_out, pool_w = even_w_in.astype(_BF16), even_w_out.astype(_BF16), odd_pool_w.astype(_BF16)
    w_up, w_down = ffn_w_up.astype(_BF16), ffn_w_down.astype(_BF16)
    conv_wb = jnp.concatenate([ffn_conv_w, ffn_conv_b[:, None, :],
                               jnp.zeros((depth, SUBLANES - ffn_conv_w.shape[1] - 1, D_FF), _F32)], axis=1)

    qt, k_all, vt_all, conv = _in_proj(x, mods, mix_gain, w_in, rows(even_q_gain), rows(even_k_gain),
                                       even_conv_w, n + lc)
    k_all, vt_all = _ctx_kv(ctx, mods, mix_gain, w_in, rows(even_k_gain), k_all, vt_all, n)
    x = _attn_out(qt, k_all, vt_all, conv, x, mods, w_out)
    x = _ffn(x, mods, ffn_gain, w_up, conv_wb, w_down, 0)

    x = _pool(x, mods, mix_gain, pool_w, rows(odd_pool_scale), 1)
    x = _ffn(x, mods, ffn_gain, w_up, conv_wb, w_down, 1)
    return x
```

```python
import jax
import jax.numpy as jnp
from jax import lax
from jax.experimental import pallas as pl
from jax.experimental.pallas import tpu as pltpu

D_MODEL = 1024
GRID_W = 64
HEAD_DIM = 128
N_Q_HEADS = 4
N_KV_HEADS = 2
Q_PER_KV = N_Q_HEADS // N_KV_HEADS
ATTN_WIDTH = N_Q_HEADS * HEAD_DIM
KV_WIDTH = N_KV_HEADS * HEAD_DIM
CONV_WIDTH = D_MODEL - ATTN_WIDTH
IN_PROJ_WIDTH = ATTN_WIDTH + 2 * KV_WIDTH + 3 * CONV_WIDTH
ROPE_THETA = 10000.0
ROPE_FREQS = HEAD_DIM // 4
SCORE_SCALE_LOG2 = HEAD_DIM ** -0.5 * 1.4426950408889634
POOL_WINDOWS = (2, 4, 8, 16)
POOL_GROUP = D_MODEL // len(POOL_WINDOWS)
D_FF = 2816
N_MOD = 6
EPS = 1e-6

SUBLANES = 8
VMEM_LIMIT_BYTES = 56 * 1024 * 1024

HALO = SUBLANES
POOL_HALO = 2 * SUBLANES
TOKEN_TILE = 512
Q_TILE = 512
KV_TILE = 768
ONES_ROWS = 16
FF_CHUNK = 256
ADALN_COLS = 1536

_BF16 = jnp.bfloat16
_F32 = jnp.float32


def _compiler_params(n_axes):
    return pltpu.CompilerParams(dimension_semantics=("arbitrary",) * n_axes,
                                vmem_limit_bytes=VMEM_LIMIT_BYTES)


def _resident(shape, index_map):
    return pl.BlockSpec(shape, index_map, pipeline_mode=pl.Buffered(1))


def _layer_spec(arr, layer):
    zeros = (0,) * (arr.ndim - 1)
    return _resident((None,) + arr.shape[1:], lambda *_: (layer,) + zeros)


def _mod_spec(d, layer, row_of):
    return pl.BlockSpec((None, None, N_MOD, d), lambda *ids: (layer, row_of(*ids), 0, 0))


def _norm_modulate(x, gain, shift, scale):
    y = x * lax.rsqrt(jnp.mean(x * x, axis=-1, keepdims=True) + EPS)
    return y * (gain * (1.0 + scale)) + shift


def _mm(a, w):
    return lax.dot_general(a, w, (((1,), (0,)), ((), ())), preferred_element_type=_F32)


def _silu(x):
    return x * (1.0 / (1.0 + jnp.exp(-x)))


def _shift_rows(u, k):
    return pltpu.roll(u, k % u.shape[0], axis=0)


def _adaln_kernel(ct_ref, w_ref, b_ref, o_ref):
    s = _silu(ct_ref[...])
    w = w_ref[0]
    n_vec = 3
    rows = [jnp.sum(s[:, r:r + 1] * w, axis=0, keepdims=True) + b_ref[0] for r in range(n_vec)]
    rows.append(jnp.zeros((SUBLANES - n_vec, w.shape[1]), _F32))
    o_ref[0] = jnp.concatenate(rows, axis=0)


def _adaln(cond_t, ada_w, ada_b):
    depth, d, width = ada_w.shape
    return pl.pallas_call(
        _adaln_kernel,
        grid=(depth, width // ADALN_COLS),
        in_specs=[
            pl.BlockSpec((d, SUBLANES), lambda l, j: (0, 0)),
            pl.BlockSpec((1, d, ADALN_COLS), lambda l, j: (l, 0, j)),
            pl.BlockSpec((1, 1, ADALN_COLS), lambda l, j: (l, 0, j)),
        ],
        out_specs=pl.BlockSpec((1, SUBLANES, ADALN_COLS), lambda l, j: (l, 0, j)),
        out_shape=jax.ShapeDtypeStruct((depth, SUBLANES, width), _F32),
        compiler_params=_compiler_params(2),
        name="adaln",
    )(cond_t, ada_w, ada_b.reshape(depth, 1, width))


def _head_rms(xh, gain):
    return xh * lax.rsqrt(jnp.mean(xh * xh, axis=-1, keepdims=True) + EPS) * gain


def _rope(xh, cos, sin_signed, first_half):
    partner = jnp.where(first_half, pltpu.roll(xh, HEAD_DIM - ROPE_FREQS, axis=1), pltpu.roll(xh, ROPE_FREQS, axis=1))
    return xh * cos + partner * sin_signed


def _rope_axis_tables(n_rows):
    axis_dim = HEAD_DIM // 2
    inv_freq = jnp.power(ROPE_THETA, -jnp.arange(0, axis_dim, 2, dtype=_F32) / axis_dim)
    ang_r = jnp.arange(n_rows, dtype=_F32)[:, None] * inv_freq
    ang_c = jnp.arange(GRID_W, dtype=_F32)[:, None] * inv_freq
    zr, zc = jnp.zeros((n_rows, axis_dim), _F32), jnp.zeros((GRID_W, axis_dim), _F32)
    return (jnp.concatenate([jnp.cos(ang_r), jnp.cos(ang_r), zr], axis=1),
            jnp.concatenate([-jnp.sin(ang_r), jnp.sin(ang_r), zr], axis=1),
            jnp.concatenate([zc, jnp.cos(ang_c), jnp.cos(ang_c)], axis=1),
            jnp.concatenate([zc, -jnp.sin(ang_c), jnp.sin(ang_c)], axis=1))


def _tile_table(row_ref, col_ref):
    col = col_ref[...]
    return jnp.concatenate([jnp.broadcast_to(row_ref[r:r + 1, :], col.shape) + col for r in range(row_ref.shape[0])],
                           axis=0)


def _halo_specs(tile, halo, n, d):
    per_tile = tile // halo
    n_halo_blocks = n // halo
    prev_spec = pl.BlockSpec((1, halo, d), lambda b, i: (b, jnp.maximum(i * per_tile - 1, 0), 0))
    main_spec = pl.BlockSpec((1, tile, d), lambda b, i: (b, i, 0))
    next_spec = pl.BlockSpec((1, halo, d), lambda b, i: (b, jnp.minimum((i + 1) * per_tile, n_halo_blocks - 1), 0))
    return prev_spec, main_spec, next_spec


def _edge_flags():
    i = pl.program_id(1)
    return (i > 0).astype(_F32), (i < pl.num_programs(1) - 1).astype(_F32)


def _in_proj_kernel(xp_ref, x_ref, xn_ref, mod_ref, gain_ref, w_ref, qg_ref, kg_ref, cw_ref,
                    rcos_ref, rsin_ref, ccos_ref, csin_ref, qt_ref, k_ref, vt_ref, conv_ref):
    tile = x_ref.shape[1]
    gain, shift, scale = gain_ref[...], mod_ref[0:1, :], mod_ref[1:2, :]
    prev_ok, next_ok = _edge_flags()
    a_main = _norm_modulate(x_ref[0], gain, shift, scale)
    a_prev = _norm_modulate(xp_ref[0], gain, shift, scale) * prev_ok
    a_next = _norm_modulate(xn_ref[0], gain, shift, scale) * next_ok
    a_ext = jnp.concatenate([a_prev, a_main, a_next], axis=0).astype(_BF16)
    a_mb = a_main.astype(_BF16)

    cos, sin = _tile_table(rcos_ref, ccos_ref), _tile_table(rsin_ref, csin_ref)
    lane = lax.broadcasted_iota(jnp.int32, (1, HEAD_DIM), 1)
    first_half = (lane % (2 * ROPE_FREQS)) < ROPE_FREQS

    o_k = ATTN_WIDTH
    o_v = o_k + KV_WIDTH
    o_b = o_v + KV_WIDTH
    o_c = o_b + CONV_WIDTH

    q = _mm(a_mb, w_ref[:, 0:o_k])
    k = _mm(a_mb, w_ref[:, o_k:o_v])
    for h in range(N_Q_HEADS):
        sl = slice(h * HEAD_DIM, (h + 1) * HEAD_DIM)
        qh = _rope(_head_rms(q[:, sl], qg_ref[...]), cos, sin, first_half)
        qt_ref[0, sl, :] = (qh * SCORE_SCALE_LOG2).T.astype(_BF16)

    v = _mm(a_mb, w_ref[:, o_v:o_b])
    for h in range(N_KV_HEADS):
        sl = slice(h * HEAD_DIM, (h + 1) * HEAD_DIM)
        k_ref[0, :, sl] = _rope(_head_rms(k[:, sl], kg_ref[...]), cos, sin, first_half).astype(_BF16)

    cx = _mm(a_ext, w_ref[:, o_c:])
    vt_ref[0] = v.T.astype(_BF16)
    gate_b = _mm(a_mb, w_ref[:, o_b:o_c])
    u = cx[:, :CONV_WIDTH] * cx[:, CONV_WIDTH:]
    cw = cw_ref[...]
    conv = cw[0:1] * _shift_rows(u, 1) + cw[1:2] * u + cw[2:3] * _shift_rows(u, -1)
    conv_ref[0] = (gate_b * conv[HALO:HALO + tile]).astype(_BF16)


def _in_proj(x, mods, gains, w_in, q_gain, k_gain, conv_w, n_keys):
    b, n, d = x.shape
    tile = TOKEN_TILE
    rows_per_tile = tile // GRID_W
    assert tile % GRID_W == 0
    prev_spec, main_spec, next_spec = _halo_specs(tile, HALO, n, d)
    rope = _rope_axis_tables(n // GRID_W)
    row_table = pl.BlockSpec((rows_per_tile, HEAD_DIM), lambda bi, i: (i, 0))
    col_table = _resident((GRID_W, HEAD_DIM), lambda bi, i: (0, 0))
    return pl.pallas_call(
        _in_proj_kernel,
        grid=(b, n // tile),
        in_specs=[
            prev_spec, main_spec, next_spec,
            _mod_spec(d, 0, lambda bi, i: bi),
            _layer_spec(gains, 0),
            _layer_spec(w_in, 0),
            _layer_spec(q_gain, 0), _layer_spec(k_gain, 0),
            _layer_spec(conv_w, 0),
            row_table, row_table, col_table, col_table,
        ],
        out_specs=[pl.BlockSpec((1, ATTN_WIDTH, tile), lambda bi, i: (bi, 0, i)),
                   pl.BlockSpec((1, tile, KV_WIDTH), lambda bi, i: (bi, i, 0)),
                   pl.BlockSpec((1, KV_WIDTH, tile), lambda bi, i: (bi, 0, i)),
                   pl.BlockSpec((1, tile, CONV_WIDTH), lambda bi, i: (bi, i, 0))],
        out_shape=[jax.ShapeDtypeStruct((b, ATTN_WIDTH, n), _BF16), jax.ShapeDtypeStruct((b, n_keys, KV_WIDTH), _BF16),
                   jax.ShapeDtypeStruct((b, KV_WIDTH, n_keys), _BF16), jax.ShapeDtypeStruct((b, n, CONV_WIDTH), _BF16)],
        compiler_params=_compiler_params(2),
        name="in_proj",
    )(x, x, x, mods, gains, w_in, q_gain, k_gain, conv_w, rope[0], rope[1], rope[2], rope[3])


def _ctx_kv_kernel(ctx_ref, mod_ref, gain_ref, w_ref, kg_ref, k_in_ref, vt_in_ref, k_ref, vt_ref):
    del k_in_ref, vt_in_ref
    a = _norm_modulate(ctx_ref[0], gain_ref[...], mod_ref[0:1, :], mod_ref[1:2, :]).astype(_BF16)
    kv = _mm(a, w_ref[...])
    for h in range(N_KV_HEADS):
        sl = slice(h * HEAD_DIM, (h + 1) * HEAD_DIM)
        k_ref[0, :, sl] = _head_rms(kv[:, sl], kg_ref[...]).astype(_BF16)
    vt_ref[0] = kv[:, KV_WIDTH:].T.astype(_BF16)


def _ctx_kv(ctx, mods, gains, w_in, k_gain, k_all, vt_all, n_latent):
    b, lc, d = ctx.shape
    assert n_latent % lc == 0 and 2 * KV_WIDTH == ATTN_WIDTH
    ctx_block = n_latent // lc
    return pl.pallas_call(
        _ctx_kv_kernel,
        grid=(b,),
        in_specs=[
            pl.BlockSpec((1, lc, d), lambda i: (i, 0, 0)),
            _mod_spec(d, 0, lambda i: b),
            _layer_spec(gains, 0),
            _resident((None, d, 2 * KV_WIDTH), lambda i: (0, 0, 1)),
            _layer_spec(k_gain, 0),
            pl.BlockSpec(memory_space=pl.ANY), pl.BlockSpec(memory_space=pl.ANY),
        ],
        out_specs=[pl.BlockSpec((1, lc, KV_WIDTH), lambda i: (i, ctx_block, 0)),
                   pl.BlockSpec((1, KV_WIDTH, lc), lambda i: (i, 0, ctx_block))],
        out_shape=[jax.ShapeDtypeStruct(k_all.shape, _BF16), jax.ShapeDtypeStruct(vt_all.shape, _BF16)],
        input_output_aliases={5: 0, 6: 1},
        compiler_params=_compiler_params(1),
        name="ctx_kv",
    )(ctx, mods, gains, w_in, k_gain, k_all, vt_all)


def _attn_kernel(qt_ref, k_ref, vt_ref, conv_ref, x_ref, mod_ref, wo_ref, o_ref,
                 s_ref, cmax_ref, m_ref, acc_ref, attn_ref):
    n_chunks = k_ref.shape[1] // KV_TILE

    def keys_of(j):
        if isinstance(j, int):
            return slice(j * KV_TILE, (j + 1) * KV_TILE)
        return pl.ds(pl.multiple_of(j * KV_TILE, KV_TILE), KV_TILE)

    def score_stage(j, head):
        g = head // Q_PER_KV
        kch = k_ref[0, keys_of(j), g * HEAD_DIM:(g + 1) * HEAD_DIM]
        qt = qt_ref[0, head * HEAD_DIM:(head + 1) * HEAD_DIM, :]
        s = jnp.dot(kch, qt, preferred_element_type=_F32)
        s_ref[head % 2] = s
        cmax_ref[head % 2] = jnp.max(s, axis=0, keepdims=True)

    def value_stage(j, head):
        g = head // Q_PER_KV
        vtch = vt_ref[0, g * HEAD_DIM:(g + 1) * HEAD_DIM, keys_of(j)]
        v_aug = jnp.concatenate([vtch, jnp.ones((ONES_ROWS, KV_TILE), _BF16)], axis=0)
        m_prev = m_ref[head]
        m_new = jnp.maximum(m_prev, cmax_ref[head % 2])
        alpha = jnp.exp2(m_prev - m_new)
        p = jnp.exp2(s_ref[head % 2] - m_new).astype(_BF16)
        acc_ref[head] = alpha * acc_ref[head] + jnp.dot(v_aug, p, preferred_element_type=_F32)
        m_ref[head] = m_new

    def chunk(j, last):
        for head in range(N_Q_HEADS):
            if head + 1 < N_Q_HEADS:
                score_stage(j, head + 1)
            elif not last:
                score_stage(j + 1, 0)
            value_stage(j, head)

    m_ref[...] = jnp.full(m_ref.shape, -jnp.inf, _F32)
    acc_ref[...] = jnp.zeros(acc_ref.shape, _F32)
    score_stage(0, 0)

    def body(j, carry):
        chunk(j, last=False)
        return carry

    lax.fori_loop(0, n_chunks - 1, body, 0, unroll=5)
    chunk(n_chunks - 1, last=True)
    for head in range(N_Q_HEADS):
        acc = acc_ref[head]
        out_t = acc[:HEAD_DIM] / acc[HEAD_DIM:HEAD_DIM + 1]
        attn_ref[:, head * HEAD_DIM:(head + 1) * HEAD_DIM] = out_t.T.astype(_BF16)

    mixed = jnp.concatenate([attn_ref[...], conv_ref[0]], axis=-1)
    y = _mm(mixed, wo_ref[...])
    o_ref[0] = x_ref[0] + mod_ref[2:3, :] * y


def _attn_out(qt, k_all, vt_all, conv, x, mods, w_out):
    b, n, d = x.shape
    n_keys = k_all.shape[1]
    assert n_keys % KV_TILE == 0
    tq = Q_TILE
    per_batch = lambda rows, width: pl.BlockSpec((1, rows, width), lambda bi, i: (bi, 0, 0))
    tile = lambda width: pl.BlockSpec((1, tq, width), lambda bi, i: (bi, i, 0))
    return pl.pallas_call(
        _attn_kernel,
        grid=(b, n // tq),
        in_specs=[
            pl.BlockSpec((1, ATTN_WIDTH, tq), lambda bi, i: (bi, 0, i)),
            per_batch(n_keys, KV_WIDTH), per_batch(KV_WIDTH, n_keys),
            tile(CONV_WIDTH),
            tile(d),
            _mod_spec(d, 0, lambda bi, i: bi),
            _layer_spec(w_out, 0),
        ],
        out_specs=tile(d),
        out_shape=jax.ShapeDtypeStruct((b, n, d), _F32),
        scratch_shapes=[
            pltpu.VMEM((2, KV_TILE, tq), _F32),
            pltpu.VMEM((2, 1, tq), _F32),
            pltpu.VMEM((N_Q_HEADS, 1, tq), _F32),
            pltpu.VMEM((N_Q_HEADS, HEAD_DIM + ONES_ROWS, tq), _F32),
            pltpu.VMEM((tq, ATTN_WIDTH), _BF16),
        ],
        compiler_params=_compiler_params(2),
        name="attn_out",
    )(qt, k_all, vt_all, conv, x, mods, w_out)


def _ffn_kernel(xp_ref, x_ref, xn_ref, mod_ref, gain_ref, wup_ref, cwb_ref, wd_ref, o_ref, h_ref):
    tile = x_ref.shape[1]
    gain, shift, scale = gain_ref[...], mod_ref[3:4, :], mod_ref[4:5, :]
    x = x_ref[0]
    prev_ok, next_ok = _edge_flags()
    f_main = _norm_modulate(x, gain, shift, scale)
    f_prev = _norm_modulate(xp_ref[0], gain, shift, scale) * prev_ok
    f_next = _norm_modulate(xn_ref[0], gain, shift, scale) * next_ok
    f_ext = jnp.concatenate([f_prev, f_main, f_next], axis=0).astype(_BF16)
    f_mb = f_main.astype(_BF16)
    n_chunks = D_FF // FF_CHUNK

    def up(c):
        cols = slice(c * FF_CHUNK, (c + 1) * FF_CHUNK)
        val_cols = slice(D_FF + c * FF_CHUNK, D_FF + (c + 1) * FF_CHUNK)
        return _mm(f_ext, wup_ref[:, cols]), _mm(f_mb, wup_ref[:, val_cols])

    def act(c, gate, val):
        cols = slice(c * FF_CHUNK, (c + 1) * FF_CHUNK)
        cwb = cwb_ref[:, cols]
        conv = cwb[0:1] * _shift_rows(gate, 1) + cwb[1:2] * gate + cwb[2:3] * _shift_rows(gate, -1)
        h_ref[:, cols] = (_silu(conv[HALO:HALO + tile] + cwb[3:4]) * val).astype(_BF16)

    nxt = up(0)
    for c in range(n_chunks):
        cur = nxt
        if c + 1 < n_chunks:
            nxt = up(c + 1)
        act(c, *cur)
    y = _mm(h_ref[...], wd_ref[...])
    o_ref[0] = x + mod_ref[5:6, :] * y


def _ffn(x, mods, gains, w_up, conv_wb, w_down, layer):
    b, n, d = x.shape
    tile = TOKEN_TILE
    prev_spec, main_spec, next_spec = _halo_specs(tile, HALO, n, d)
    return pl.pallas_call(
        _ffn_kernel,
        grid=(b, n // tile),
        in_specs=[
            prev_spec, main_spec, next_spec,
            _mod_spec(d, layer, lambda bi, i: bi),
            _layer_spec(gains, layer),
            _layer_spec(w_up, layer), _layer_spec(conv_wb, layer), _layer_spec(w_down, layer),
        ],
        out_specs=main_spec,
        out_shape=jax.ShapeDtypeStruct((b, n, d), _F32),
        scratch_shapes=[pltpu.VMEM((tile, D_FF), _BF16)],
        compiler_params=_compiler_params(2),
        name="ffn",
    )(x, x, x, mods, gains, w_up, conv_wb, w_down)


def _pool_kernel(xp_ref, x_ref, xn_ref, mod_ref, gain_ref, pw_ref, ps_ref, o_ref):
    tile = x_ref.shape[1]
    halo = xp_ref.shape[1]
    n_total = tile * pl.num_programs(1)
    gain, shift, scale = gain_ref[...], mod_ref[0:1, :], mod_ref[1:2, :]
    x = x_ref[0]
    prev_ok, next_ok = _edge_flags()
    a_prev = _norm_modulate(xp_ref[0], gain, shift, scale) * prev_ok
    a_main = _norm_modulate(x, gain, shift, scale)
    a_next = _norm_modulate(xn_ref[0], gain, shift, scale) * next_ok
    a_ext = jnp.concatenate([a_prev, a_main, a_next], axis=0)
    t = pl.program_id(1) * tile + lax.broadcasted_iota(jnp.int32, (tile, 1), 0)

    mixed = []
    for gi, w in enumerate(POOL_WINDOWS):
        a = a_ext[:, gi * POOL_GROUP:(gi + 1) * POOL_GROUP]
        s = _shift_rows(a, 1) + a
        k = 2
        while k < w:
            s = _shift_rows(s, k // 2) + _shift_rows(s, -(k // 2))
            k *= 2
        count = jnp.minimum(t + w - w // 2, n_total) - jnp.maximum(t - w // 2, 0)
        pooled = s[halo:halo + tile] * (1.0 / count.astype(_F32)) - a[halo:halo + tile]
        mixed.append(_mm(pooled.astype(_BF16), pw_ref[gi]))
    o_ref[0] = x + jnp.concatenate(mixed, axis=-1) * (ps_ref[...] * mod_ref[2:3, :])


def _pool(x, mods, gains, pool_w, pool_scale, layer):
    b, n, d = x.shape
    tile = TOKEN_TILE
    prev_spec, main_spec, next_spec = _halo_specs(tile, POOL_HALO, n, d)
    return pl.pallas_call(
        _pool_kernel,
        grid=(b, n // tile),
        in_specs=[
            prev_spec, main_spec, next_spec,
            _mod_spec(d, layer, lambda bi, i: bi),
            _layer_spec(gains, layer),
            _layer_spec(pool_w, layer // 2),
            _layer_spec(pool_scale, layer // 2),
        ],
        out_specs=main_spec,
        out_shape=jax.ShapeDtypeStruct((b, n, d), _F32),
        compiler_params=_compiler_params(2),
        name="pool",
    )(x, x, x, mods, gains, pool_w, pool_scale)


def kernel(x, c, ctx, c_ctx, ada_w, ada_b, mix_norm, ffn_norm, even_w_in, even_q_gain, even_k_gain, even_conv_w,
           even_w_out, odd_pool_w, odd_pool_scale, ffn_w_up, ffn_conv_w, ffn_conv_b, ffn_w_down):
    b, n, d = x.shape
    depth = ada_w.shape[0]
    lc = ctx.shape[1]
    assert depth == 2 and d == D_MODEL and n % TOKEN_TILE == 0 and n % Q_TILE == 0

    cond = jnp.concatenate([c, c_ctx[None, :], jnp.zeros((SUBLANES - b - 1, d), _F32)], axis=0)
    mods = _adaln(cond.T, ada_w, ada_b).reshape(depth, SUBLANES, N_MOD, d)

    rows = lambda v: v.reshape(v.shape[0], 1, v.shape[-1])
    mix_gain, ffn_gain = rows(mix_norm), rows(ffn_norm)
    w_in, w_out, pool_w, w_up, w_down = even_w_in, even_w_out, odd_pool_w, ffn_w_up, ffn_w_down
    conv_wb = jnp.concatenate([ffn_conv_w, ffn_conv_b[:, None, :],
                               jnp.zeros((depth, SUBLANES - ffn_conv_w.shape[1] - 1, D_FF), _F32)], axis=1)

    qt, k_all, vt_all, conv = _in_proj(x, mods, mix_gain, w_in, rows(even_q_gain), rows(even_k_gain),
                                       even_conv_w, n + lc)
    k_all, vt_all = _ctx_kv(ctx, mods, mix_gain, w_in, rows(even_k_gain), k_all, vt_all, n)
    x = _attn_out(qt, k_all, vt_all, conv, x, mods, w_out)
    x = _ffn(x, mods, ffn_gain, w_up, conv_wb, w_down, 0)

    x = _pool(x, mods, mix_gain, pool_w, rows(odd_pool_scale), 1)
    x = _ffn(x, mods, ffn_gain, w_up, conv_wb, w_down, 1)
    return x
```

```python
import jax
import jax.numpy as jnp
from jax import lax
from jax.experimental import pallas as pl
from jax.experimental.pallas import tpu as pltpu

D_MODEL = 1024
GRID_W = 64
HEAD_DIM = 128
N_Q_HEADS = 4
N_KV_HEADS = 2
Q_PER_KV = N_Q_HEADS // N_KV_HEADS
ATTN_WIDTH = N_Q_HEADS * HEAD_DIM
KV_WIDTH = N_KV_HEADS * HEAD_DIM
CONV_WIDTH = D_MODEL - ATTN_WIDTH
IN_PROJ_WIDTH = ATTN_WIDTH + 2 * KV_WIDTH + 3 * CONV_WIDTH
ROPE_THETA = 10000.0
ROPE_FREQS = HEAD_DIM // 4
SCORE_SCALE_LOG2 = HEAD_DIM ** -0.5 * 1.4426950408889634
POOL_WINDOWS = (2, 4, 8, 16)
POOL_GROUP = D_MODEL // len(POOL_WINDOWS)
D_FF = 2816
N_MOD = 6
EPS = 1e-6

SUBLANES = 8
VMEM_LIMIT_BYTES = 56 * 1024 * 1024

HALO = SUBLANES
POOL_HALO = 2 * SUBLANES
TOKEN_TILE = 512
POOL_TILE = 1024
Q_TILE = 512
KV_TILE = 768
ONES_ROWS = 16
FF_CHUNK = 256
ADALN_COLS = 1536

_BF16 = jnp.bfloat16
_F32 = jnp.float32


def _compiler_params(n_axes):
    return pltpu.CompilerParams(dimension_semantics=("arbitrary",) * n_axes,
                                vmem_limit_bytes=VMEM_LIMIT_BYTES)


def _resident(shape, index_map):
    return pl.BlockSpec(shape, index_map, pipeline_mode=pl.Buffered(1))


def _layer_spec(arr, layer):
    zeros = (0,) * (arr.ndim - 1)
    return _resident((None,) + arr.shape[1:], lambda *_: (layer,) + zeros)


def _mod_spec(d, layer, row_of):
    return pl.BlockSpec((None, None, N_MOD, d), lambda *ids: (layer, row_of(*ids), 0, 0))


def _norm_modulate(x, gain, shift, scale):
    y = x * lax.rsqrt(jnp.mean(x * x, axis=-1, keepdims=True) + EPS)
    return y * (gain * (1.0 + scale)) + shift


def _mm(a, w):
    return lax.dot_general(a, w, (((1,), (0,)), ((), ())), preferred_element_type=_F32)


def _silu(x):
    return x * (1.0 / (1.0 + jnp.exp(-x)))


def _shift_rows(u, k):
    return pltpu.roll(u, k % u.shape[0], axis=0)


def _adaln_kernel(ct_ref, w_ref, b_ref, o_ref):
    s = _silu(ct_ref[...])
    w = w_ref[0]
    n_vec = 3
    rows = [jnp.sum(s[:, r:r + 1] * w, axis=0, keepdims=True) + b_ref[0] for r in range(n_vec)]
    rows.append(jnp.zeros((SUBLANES - n_vec, w.shape[1]), _F32))
    o_ref[0] = jnp.concatenate(rows, axis=0)


def _adaln(cond_t, ada_w, ada_b):
    depth, d, width = ada_w.shape
    return pl.pallas_call(
        _adaln_kernel,
        grid=(depth, width // ADALN_COLS),
        in_specs=[
            pl.BlockSpec((d, SUBLANES), lambda l, j: (0, 0)),
            pl.BlockSpec((1, d, ADALN_COLS), lambda l, j: (l, 0, j)),
            pl.BlockSpec((1, 1, ADALN_COLS), lambda l, j: (l, 0, j)),
        ],
        out_specs=pl.BlockSpec((1, SUBLANES, ADALN_COLS), lambda l, j: (l, 0, j)),
        out_shape=jax.ShapeDtypeStruct((depth, SUBLANES, width), _F32),
        compiler_params=_compiler_params(2),
        name="adaln",
    )(cond_t, ada_w, ada_b.reshape(depth, 1, width))


def _head_rms(xh, gain):
    return xh * lax.rsqrt(jnp.mean(xh * xh, axis=-1, keepdims=True) + EPS) * gain


def _rope(xh, cos, sin_signed, first_half):
    partner = jnp.where(first_half, pltpu.roll(xh, HEAD_DIM - ROPE_FREQS, axis=1), pltpu.roll(xh, ROPE_FREQS, axis=1))
    return xh * cos + partner * sin_signed


def _rope_axis_tables(n_rows):
    axis_dim = HEAD_DIM // 2
    inv_freq = jnp.power(ROPE_THETA, -jnp.arange(0, axis_dim, 2, dtype=_F32) / axis_dim)
    ang_r = jnp.arange(n_rows, dtype=_F32)[:, None] * inv_freq
    ang_c = jnp.arange(GRID_W, dtype=_F32)[:, None] * inv_freq
    zr, zc = jnp.zeros((n_rows, axis_dim), _F32), jnp.zeros((GRID_W, axis_dim), _F32)
    return (jnp.concatenate([jnp.cos(ang_r), jnp.cos(ang_r), zr], axis=1),
            jnp.concatenate([-jnp.sin(ang_r), jnp.sin(ang_r), zr], axis=1),
            jnp.concatenate([zc, jnp.cos(ang_c), jnp.cos(ang_c)], axis=1),
            jnp.concatenate([zc, -jnp.sin(ang_c), jnp.sin(ang_c)], axis=1))


def _tile_table(row_ref, col_ref):
    col = col_ref[...]
    return jnp.concatenate([jnp.broadcast_to(row_ref[r:r + 1, :], col.shape) + col for r in range(row_ref.shape[0])],
                           axis=0)


def _halo_specs(tile, halo, n, d):
    per_tile = tile // halo
    n_halo_blocks = n // halo
    prev_spec = pl.BlockSpec((1, halo, d), lambda b, i: (b, jnp.maximum(i * per_tile - 1, 0), 0))
    main_spec = pl.BlockSpec((1, tile, d), lambda b, i: (b, i, 0))
    next_spec = pl.BlockSpec((1, halo, d), lambda b, i: (b, jnp.minimum((i + 1) * per_tile, n_halo_blocks - 1), 0))
    return prev_spec, main_spec, next_spec


def _edge_flags():
    i = pl.program_id(1)
    return (i > 0).astype(_F32), (i < pl.num_programs(1) - 1).astype(_F32)


def _in_proj_kernel(xp_ref, x_ref, xn_ref, mod_ref, gain_ref, w_ref, qg_ref, kg_ref, cw_ref,
                    rcos_ref, rsin_ref, ccos_ref, csin_ref, qt_ref, k_ref, vt_ref, conv_ref):
    tile = x_ref.shape[1]
    gain, shift, scale = gain_ref[...], mod_ref[0:1, :], mod_ref[1:2, :]
    prev_ok, next_ok = _edge_flags()
    a_main = _norm_modulate(x_ref[0], gain, shift, scale)
    a_prev = _norm_modulate(xp_ref[0], gain, shift, scale) * prev_ok
    a_next = _norm_modulate(xn_ref[0], gain, shift, scale) * next_ok
    a_ext = jnp.concatenate([a_prev, a_main, a_next], axis=0).astype(_BF16)
    a_mb = a_main.astype(_BF16)

    cos, sin = _tile_table(rcos_ref, ccos_ref), _tile_table(rsin_ref, csin_ref)
    lane = lax.broadcasted_iota(jnp.int32, (1, HEAD_DIM), 1)
    first_half = (lane % (2 * ROPE_FREQS)) < ROPE_FREQS

    o_k = ATTN_WIDTH
    o_v = o_k + KV_WIDTH
    o_b = o_v + KV_WIDTH
    o_c = o_b + CONV_WIDTH

    q = _mm(a_mb, w_ref[:, 0:o_k])
    k = _mm(a_mb, w_ref[:, o_k:o_v])
    for h in range(N_Q_HEADS):
        sl = slice(h * HEAD_DIM, (h + 1) * HEAD_DIM)
        qh = _rope(_head_rms(q[:, sl], qg_ref[...]), cos, sin, first_half)
        qt_ref[0, sl, :] = (qh * SCORE_SCALE_LOG2).T.astype(_BF16)

    v = _mm(a_mb, w_ref[:, o_v:o_b])
    for h in range(N_KV_HEADS):
        sl = slice(h * HEAD_DIM, (h + 1) * HEAD_DIM)
        k_ref[0, :, sl] = _rope(_head_rms(k[:, sl], kg_ref[...]), cos, sin, first_half).astype(_BF16)

    cx = _mm(a_ext, w_ref[:, o_c:])
    vt_ref[0] = v.T.astype(_BF16)
    gate_b = _mm(a_mb, w_ref[:, o_b:o_c])
    u = cx[:, :CONV_WIDTH] * cx[:, CONV_WIDTH:]
    cw = cw_ref[...]
    conv = cw[0:1] * _shift_rows(u, 1) + cw[1:2] * u + cw[2:3] * _shift_rows(u, -1)
    conv_ref[0] = (gate_b * conv[HALO:HALO + tile]).astype(_BF16)


def _in_proj(x, mods, gains, w_in, q_gain, k_gain, conv_w, n_keys):
    b, n, d = x.shape
    tile = TOKEN_TILE
    rows_per_tile = tile // GRID_W
    assert tile % GRID_W == 0
    prev_spec, main_spec, next_spec = _halo_specs(tile, HALO, n, d)
    rope = _rope_axis_tables(n // GRID_W)
    row_table = pl.BlockSpec((rows_per_tile, HEAD_DIM), lambda bi, i: (i, 0))
    col_table = _resident((GRID_W, HEAD_DIM), lambda bi, i: (0, 0))
    return pl.pallas_call(
        _in_proj_kernel,
        grid=(b, n // tile),
        in_specs=[
            prev_spec, main_spec, next_spec,
            _mod_spec(d, 0, lambda bi, i: bi),
            _layer_spec(gains, 0),
            _layer_spec(w_in, 0),
            _layer_spec(q_gain, 0), _layer_spec(k_gain, 0),
            _layer_spec(conv_w, 0),
            row_table, row_table, col_table, col_table,
        ],
        out_specs=[pl.BlockSpec((1, ATTN_WIDTH, tile), lambda bi, i: (bi, 0, i)),
                   pl.BlockSpec((1, tile, KV_WIDTH), lambda bi, i: (bi, i, 0)),
                   pl.BlockSpec((1, KV_WIDTH, tile), lambda bi, i: (bi, 0, i)),
                   pl.BlockSpec((1, tile, CONV_WIDTH), lambda bi, i: (bi, i, 0))],
        out_shape=[jax.ShapeDtypeStruct((b, ATTN_WIDTH, n), _BF16), jax.ShapeDtypeStruct((b, n_keys, KV_WIDTH), _BF16),
                   jax.ShapeDtypeStruct((b, KV_WIDTH, n_keys), _BF16), jax.ShapeDtypeStruct((b, n, CONV_WIDTH), _BF16)],
        compiler_params=_compiler_params(2),
        name="in_proj",
    )(x, x, x, mods, gains, w_in, q_gain, k_gain, conv_w, rope[0], rope[1], rope[2], rope[3])


def _ctx_kv_kernel(ctx_ref, mod_ref, gain_ref, w_ref, kg_ref, k_in_ref, vt_in_ref, k_ref, vt_ref):
    del k_in_ref, vt_in_ref
    a = _norm_modulate(ctx_ref[0], gain_ref[...], mod_ref[0:1, :], mod_ref[1:2, :]).astype(_BF16)
    kv = _mm(a, w_ref[...])
    for h in range(N_KV_HEADS):
        sl = slice(h * HEAD_DIM, (h + 1) * HEAD_DIM)
        k_ref[0, :, sl] = _head_rms(kv[:, sl], kg_ref[...]).astype(_BF16)
    vt_ref[0] = kv[:, KV_WIDTH:].T.astype(_BF16)


def _ctx_kv(ctx, mods, gains, w_in, k_gain, k_all, vt_all, n_latent):
    b, lc, d = ctx.shape
    assert n_latent % lc == 0 and 2 * KV_WIDTH == ATTN_WIDTH
    ctx_block = n_latent // lc
    return pl.pallas_call(
        _ctx_kv_kernel,
        grid=(b,),
        in_specs=[
            pl.BlockSpec((1, lc, d), lambda i: (i, 0, 0)),
            _mod_spec(d, 0, lambda i: b),
            _layer_spec(gains, 0),
            _resident((None, d, 2 * KV_WIDTH), lambda i: (0, 0, 1)),
            _layer_spec(k_gain, 0),
            pl.BlockSpec(memory_space=pl.ANY), pl.BlockSpec(memory_space=pl.ANY),
        ],
        out_specs=[pl.BlockSpec((1, lc, KV_WIDTH), lambda i: (i, ctx_block, 0)),
                   pl.BlockSpec((1, KV_WIDTH, lc), lambda i: (i, 0, ctx_block))],
        out_shape=[jax.ShapeDtypeStruct(k_all.shape, _BF16), jax.ShapeDtypeStruct(vt_all.shape, _BF16)],
        input_output_aliases={5: 0, 6: 1},
        compiler_params=_compiler_params(1),
        name="ctx_kv",
    )(ctx, mods, gains, w_in, k_gain, k_all, vt_all)


def _attn_kernel(qt_ref, k_ref, vt_ref, conv_ref, x_ref, mod_ref, wo_ref, o_ref,
                 s_ref, cmax_ref, m_ref, acc_ref, attn_ref):
    n_chunks = k_ref.shape[1] // KV_TILE

    def keys_of(j):
        if isinstance(j, int):
            return slice(j * KV_TILE, (j + 1) * KV_TILE)
        return pl.ds(pl.multiple_of(j * KV_TILE, KV_TILE), KV_TILE)

    def score_stage(j, head):
        g = head // Q_PER_KV
        kch = k_ref[0, keys_of(j), g * HEAD_DIM:(g + 1) * HEAD_DIM]
        qt = qt_ref[0, head * HEAD_DIM:(head + 1) * HEAD_DIM, :]
        s = jnp.dot(kch, qt, preferred_element_type=_F32)
        s_ref[head % 2] = s
        cmax_ref[head % 2] = jnp.max(s, axis=0, keepdims=True)

    def value_stage(j, head):
        g = head // Q_PER_KV
        vtch = vt_ref[0, g * HEAD_DIM:(g + 1) * HEAD_DIM, keys_of(j)]
        v_aug = jnp.concatenate([vtch, jnp.ones((ONES_ROWS, KV_TILE), _BF16)], axis=0)
        m_prev = m_ref[head]
        m_new = jnp.maximum(m_prev, cmax_ref[head % 2])
        alpha = jnp.exp2(m_prev - m_new)
        p = jnp.exp2(s_ref[head % 2] - m_new).astype(_BF16)
        acc_ref[head] = alpha * acc_ref[head] + jnp.dot(v_aug, p, preferred_element_type=_F32)
        m_ref[head] = m_new

    def chunk(j, last):
        for head in range(N_Q_HEADS):
            if head + 1 < N_Q_HEADS:
                score_stage(j, head + 1)
            elif not last:
                score_stage(j + 1, 0)
            value_stage(j, head)

    m_ref[...] = jnp.full(m_ref.shape, -jnp.inf, _F32)
    acc_ref[...] = jnp.zeros(acc_ref.shape, _F32)
    score_stage(0, 0)

    def body(j, carry):
        chunk(j, last=False)
        return carry

    lax.fori_loop(0, n_chunks - 1, body, 0, unroll=5)
    chunk(n_chunks - 1, last=True)
    for head in range(N_Q_HEADS):
        acc = acc_ref[head]
        out_t = acc[:HEAD_DIM] / acc[HEAD_DIM:HEAD_DIM + 1]
        attn_ref[:, head * HEAD_DIM:(head + 1) * HEAD_DIM] = out_t.T.astype(_BF16)

    mixed = jnp.concatenate([attn_ref[...], conv_ref[0]], axis=-1)
    y = _mm(mixed, wo_ref[...])
    o_ref[0] = x_ref[0] + mod_ref[2:3, :] * y


def _attn_out(qt, k_all, vt_all, conv, x, mods, w_out):
    b, n, d = x.shape
    n_keys = k_all.shape[1]
    assert n_keys % KV_TILE == 0
    tq = Q_TILE
    per_batch = lambda rows, width: pl.BlockSpec((1, rows, width), lambda bi, i: (bi, 0, 0))
    tile = lambda width: pl.BlockSpec((1, tq, width), lambda bi, i: (bi, i, 0))
    return pl.pallas_call(
        _attn_kernel,
        grid=(b, n // tq),
        in_specs=[
            pl.BlockSpec((1, ATTN_WIDTH, tq), lambda bi, i: (bi, 0, i)),
            per_batch(n_keys, KV_WIDTH), per_batch(KV_WIDTH, n_keys),
            tile(CONV_WIDTH),
            tile(d),
            _mod_spec(d, 0, lambda bi, i: bi),
            _layer_spec(w_out, 0),
        ],
        out_specs=tile(d),
        out_shape=jax.ShapeDtypeStruct((b, n, d), _F32),
        scratch_shapes=[
            pltpu.VMEM((2, KV_TILE, tq), _F32),
            pltpu.VMEM((2, 1, tq), _F32),
            pltpu.VMEM((N_Q_HEADS, 1, tq), _F32),
            pltpu.VMEM((N_Q_HEADS, HEAD_DIM + ONES_ROWS, tq), _F32),
            pltpu.VMEM((tq, ATTN_WIDTH), _BF16),
        ],
        compiler_params=_compiler_params(2),
        name="attn_out",
    )(qt, k_all, vt_all, conv, x, mods, w_out)


def _ffn_kernel(xp_ref, x_ref, xn_ref, mod_ref, gain_ref, wup_ref, cwb_ref, wd_ref, o_ref, h_ref):
    tile = x_ref.shape[1]
    gain, shift, scale = gain_ref[...], mod_ref[3:4, :], mod_ref[4:5, :]
    x = x_ref[0]
    prev_ok, next_ok = _edge_flags()
    f_main = _norm_modulate(x, gain, shift, scale)
    f_prev = _norm_modulate(xp_ref[0], gain, shift, scale) * prev_ok
    f_next = _norm_modulate(xn_ref[0], gain, shift, scale) * next_ok
    f_ext = jnp.concatenate([f_prev, f_main, f_next], axis=0).astype(_BF16)
    f_mb = f_main.astype(_BF16)
    n_chunks = D_FF // FF_CHUNK

    def up(c):
        cols = slice(c * FF_CHUNK, (c + 1) * FF_CHUNK)
        val_cols = slice(D_FF + c * FF_CHUNK, D_FF + (c + 1) * FF_CHUNK)
        return _mm(f_ext, wup_ref[:, cols]), _mm(f_mb, wup_ref[:, val_cols])

    def act(c, gate, val):
        cols = slice(c * FF_CHUNK, (c + 1) * FF_CHUNK)
        cwb = cwb_ref[:, cols]
        conv = cwb[0:1] * _shift_rows(gate, 1) + cwb[1:2] * gate + cwb[2:3] * _shift_rows(gate, -1)
        h_ref[:, cols] = (_silu(conv[HALO:HALO + tile] + cwb[3:4]) * val).astype(_BF16)

    nxt = up(0)
    for c in range(n_chunks):
        cur = nxt
        if c + 1 < n_chunks:
            nxt = up(c + 1)
        act(c, *cur)
    y = _mm(h_ref[...], wd_ref[...])
    o_ref[0] = x + mod_ref[5:6, :] * y


def _ffn(x, mods, gains, w_up, conv_wb, w_down, layer):
    b, n, d = x.shape
    tile = TOKEN_TILE
    prev_spec, main_spec, next_spec = _halo_specs(tile, HALO, n, d)
    return pl.pallas_call(
        _ffn_kernel,
        grid=(b, n // tile),
        in_specs=[
            prev_spec, main_spec, next_spec,
            _mod_spec(d, layer, lambda bi, i: bi),
            _layer_spec(gains, layer),
            _layer_spec(w_up, layer), _layer_spec(conv_wb, layer), _layer_spec(w_down, layer),
        ],
        out_specs=main_spec,
        out_shape=jax.ShapeDtypeStruct((b, n, d), _F32),
        scratch_shapes=[pltpu.VMEM((tile, D_FF), _BF16)],
        compiler_params=_compiler_params(2),
        name="ffn",
    )(x, x, x, mods, gains, w_up, conv_wb, w_down)


def _pool_kernel(xp_ref, x_ref, xn_ref, mod_ref, gain_ref, pw_ref, ps_ref, o_ref):
    tile = x_ref.shape[1]
    halo = xp_ref.shape[1]
    n_total = tile * pl.num_programs(1)
    gain, shift, scale = gain_ref[...], mod_ref[0:1, :], mod_ref[1:2, :]
    x = x_ref[0]
    prev_ok, next_ok = _edge_flags()
    a_prev = _norm_modulate(xp_ref[0], gain, shift, scale) * prev_ok
    a_main = _norm_modulate(x, gain, shift, scale)
    a_next = _norm_modulate(xn_ref[0], gain, shift, scale) * next_ok
    a_ext = jnp.concatenate([a_prev, a_main, a_next], axis=0)
    t = pl.program_id(1) * tile + lax.broadcasted_iota(jnp.int32, (tile, 1), 0)

    mixed = []
    for gi, w in enumerate(POOL_WINDOWS):
        a = a_ext[:, gi * POOL_GROUP:(gi + 1) * POOL_GROUP]
        s = a
        k = 1
        while 2 * k < w:
            s = s + _shift_rows(s, -k)
            k *= 2
        s = _shift_rows(s, k) + s
        count = jnp.minimum(t + w - w // 2, n_total) - jnp.maximum(t - w // 2, 0)
        pooled = s[halo:halo + tile] * (1.0 / count.astype(_F32)) - a[halo:halo + tile]
        mixed.append(_mm(pooled.astype(_BF16), pw_ref[gi]))
    o_ref[0] = x + jnp.concatenate(mixed, axis=-1) * (ps_ref[...] * mod_ref[2:3, :])


def _pool(x, mods, gains, pool_w, pool_scale, layer):
    b, n, d = x.shape
    tile = POOL_TILE
    prev_spec, main_spec, next_spec = _halo_specs(tile, POOL_HALO, n, d)
    return pl.pallas_call(
        _pool_kernel,
        grid=(b, n // tile),
        in_specs=[
            prev_spec, main_spec, next_spec,
            _mod_spec(d, layer, lambda bi, i: bi),
            _layer_spec(gains, layer),
            _layer_spec(pool_w, layer // 2),
            _layer_spec(pool_scale, layer // 2),
        ],
        out_specs=main_spec,
        out_shape=jax.ShapeDtypeStruct((b, n, d), _F32),
        compiler_params=_compiler_params(2),
        name="pool",
    )(x, x, x, mods, gains, pool_w, pool_scale)


def kernel(x, c, ctx, c_ctx, ada_w, ada_b, mix_norm, ffn_norm, even_w_in, even_q_gain, even_k_gain, even_conv_w,
           even_w_out, odd_pool_w, odd_pool_scale, ffn_w_up, ffn_conv_w, ffn_conv_b, ffn_w_down):
    b, n, d = x.shape
    depth = ada_w.shape[0]
    lc = ctx.shape[1]
    assert depth == 2 and d == D_MODEL and n % TOKEN_TILE == 0 and n % Q_TILE == 0 and n % POOL_TILE == 0

    cond = jnp.concatenate([c, c_ctx[None, :], jnp.zeros((SUBLANES - b - 1, d), _F32)], axis=0)
    mods = _adaln(cond.T, ada_w, ada_b).reshape(depth, SUBLANES, N_MOD, d)

    rows = lambda v: v.reshape(v.shape[0], 1, v.shape[-1])
    mix_gain, ffn_gain = rows(mix_norm), rows(ffn_norm)
    w_in, w_out, pool_w, w_up, w_down = even_w_in, even_w_out, odd_pool_w, ffn_w_up, ffn_w_down
    conv_wb = jnp.concatenate([ffn_conv_w, ffn_conv_b[:, None, :],
                               jnp.zeros((depth, SUBLANES - ffn_conv_w.shape[1] - 1, D_FF), _F32)], axis=1)

    qt, k_all, vt_all, conv = _in_proj(x, mods, mix_gain, w_in, rows(even_q_gain), rows(even_k_gain),
                                       even_conv_w, n + lc)
    k_all, vt_all = _ctx_kv(ctx, mods, mix_gain, w_in, rows(even_k_gain), k_all, vt_all, n)
    x = _attn_out(qt, k_all, vt_all, conv, x, mods, w_out)
    x = _ffn(x, mods, ffn_gain, w_up, conv_wb, w_down, 0)

    x = _pool(x, mods, mix_gain, pool_w, rows(odd_pool_scale), 1)
    x = _ffn(x, mods, ffn_gain, w_up, conv_wb, w_down, 1)
    return x
```

```python
import jax
import jax.numpy as jnp
from jax import lax
from jax.experimental import pallas as pl
from jax.experimental.pallas import tpu as pltpu

D_MODEL = 1024
GRID_W = 64
HEAD_DIM = 128
N_Q_HEADS = 4
N_KV_HEADS = 2
Q_PER_KV = N_Q_HEADS // N_KV_HEADS
ATTN_WIDTH = N_Q_HEADS * HEAD_DIM
KV_WIDTH = N_KV_HEADS * HEAD_DIM
CONV_WIDTH = D_MODEL - ATTN_WIDTH
IN_PROJ_WIDTH = ATTN_WIDTH + 2 * KV_WIDTH + 3 * CONV_WIDTH
ROPE_THETA = 10000.0
ROPE_FREQS = HEAD_DIM // 4
SCORE_SCALE_LOG2 = HEAD_DIM ** -0.5 * 1.4426950408889634
POOL_WINDOWS = (2, 4, 8, 16)
POOL_GROUP = D_MODEL // len(POOL_WINDOWS)
D_FF = 2816
N_MOD = 6
EPS = 1e-6

SUBLANES = 8
VMEM_LIMIT_BYTES = 56 * 1024 * 1024

HALO = SUBLANES
POOL_HALO = 2 * SUBLANES
TOKEN_TILE = 512
WIDE_TILE = 1024
Q_TILE = 512
KV_TILE = 768
ONES_ROWS = 16
FF_CHUNK = 256
ADALN_COLS = 3072

_BF16 = jnp.bfloat16
_F32 = jnp.float32


def _compiler_params(n_axes):
    return pltpu.CompilerParams(dimension_semantics=("arbitrary",) * n_axes,
                                vmem_limit_bytes=VMEM_LIMIT_BYTES)


def _resident(shape, index_map):
    return pl.BlockSpec(shape, index_map, pipeline_mode=pl.Buffered(1))


def _layer_spec(arr, layer):
    zeros = (0,) * (arr.ndim - 1)
    return _resident((None,) + arr.shape[1:], lambda *_: (layer,) + zeros)


def _mod_spec(d, layer, row_of):
    return pl.BlockSpec((None, None, N_MOD, d), lambda *ids: (layer, row_of(*ids), 0, 0))


def _norm_modulate(x, gain, shift, scale):
    y = x * lax.rsqrt(jnp.mean(x * x, axis=-1, keepdims=True) + EPS)
    return y * (gain * (1.0 + scale)) + shift


def _mm(a, w):
    return lax.dot_general(a, w, (((1,), (0,)), ((), ())), preferred_element_type=_F32)


def _silu(x):
    return x * (1.0 / (1.0 + jnp.exp(-x)))


def _shift_rows(u, k):
    return pltpu.roll(u, k % u.shape[0], axis=0)


def _adaln_kernel(ct_ref, w_ref, b_ref, o_ref):
    s = _silu(ct_ref[...])
    w = w_ref[0]
    n_vec = 3
    rows = [jnp.sum(s[:, r:r + 1] * w, axis=0, keepdims=True) + b_ref[0] for r in range(n_vec)]
    rows.append(jnp.zeros((SUBLANES - n_vec, w.shape[1]), _F32))
    o_ref[0] = jnp.concatenate(rows, axis=0)


def _adaln(cond_t, ada_w, ada_b):
    depth, d, width = ada_w.shape
    return pl.pallas_call(
        _adaln_kernel,
        grid=(depth, width // ADALN_COLS),
        in_specs=[
            pl.BlockSpec((d, SUBLANES), lambda l, j: (0, 0)),
            pl.BlockSpec((1, d, ADALN_COLS), lambda l, j: (l, 0, j)),
            pl.BlockSpec((1, 1, ADALN_COLS), lambda l, j: (l, 0, j)),
        ],
        out_specs=pl.BlockSpec((1, SUBLANES, ADALN_COLS), lambda l, j: (l, 0, j)),
        out_shape=jax.ShapeDtypeStruct((depth, SUBLANES, width), _F32),
        compiler_params=_compiler_params(2),
        name="adaln",
    )(cond_t, ada_w, ada_b.reshape(depth, 1, width))


def _head_rms(xh, gain):
    return xh * lax.rsqrt(jnp.mean(xh * xh, axis=-1, keepdims=True) + EPS) * gain


def _rope(xh, cos, sin_signed, first_half):
    partner = jnp.where(first_half, pltpu.roll(xh, HEAD_DIM - ROPE_FREQS, axis=1), pltpu.roll(xh, ROPE_FREQS, axis=1))
    return xh * cos + partner * sin_signed


def _rope_axis_tables(n_rows):
    axis_dim = HEAD_DIM // 2
    inv_freq = jnp.power(ROPE_THETA, -jnp.arange(0, axis_dim, 2, dtype=_F32) / axis_dim)
    ang_r = jnp.arange(n_rows, dtype=_F32)[:, None] * inv_freq
    ang_c = jnp.arange(GRID_W, dtype=_F32)[:, None] * inv_freq
    zr, zc = jnp.zeros((n_rows, axis_dim), _F32), jnp.zeros((GRID_W, axis_dim), _F32)
    return (jnp.concatenate([jnp.cos(ang_r), jnp.cos(ang_r), zr], axis=1),
            jnp.concatenate([-jnp.sin(ang_r), jnp.sin(ang_r), zr], axis=1),
            jnp.concatenate([zc, jnp.cos(ang_c), jnp.cos(ang_c)], axis=1),
            jnp.concatenate([zc, -jnp.sin(ang_c), jnp.sin(ang_c)], axis=1))


def _tile_table(row_ref, col_ref):
    col = col_ref[...]
    return jnp.concatenate([jnp.broadcast_to(row_ref[r:r + 1, :], col.shape) + col for r in range(row_ref.shape[0])],
                           axis=0)


def _halo_specs(tile, halo, n, d):
    per_tile = tile // halo
    n_halo_blocks = n // halo
    prev_spec = pl.BlockSpec((1, halo, d), lambda b, i: (b, jnp.maximum(i * per_tile - 1, 0), 0))
    main_spec = pl.BlockSpec((1, tile, d), lambda b, i: (b, i, 0))
    next_spec = pl.BlockSpec((1, halo, d), lambda b, i: (b, jnp.minimum((i + 1) * per_tile, n_halo_blocks - 1), 0))
    return prev_spec, main_spec, next_spec


def _edge_flags():
    i = pl.program_id(1)
    return (i > 0).astype(_F32), (i < pl.num_programs(1) - 1).astype(_F32)


def _in_proj_kernel(xp_ref, x_ref, xn_ref, mod_ref, gain_ref, w_ref, qg_ref, kg_ref, cw_ref,
                    rcos_ref, rsin_ref, ccos_ref, csin_ref, qt_ref, k_ref, vt_ref, conv_ref):
    tile = x_ref.shape[1]
    gain, shift, scale = gain_ref[...], mod_ref[0:1, :], mod_ref[1:2, :]
    prev_ok, next_ok = _edge_flags()
    a_main = _norm_modulate(x_ref[0], gain, shift, scale)
    a_prev = _norm_modulate(xp_ref[0], gain, shift, scale) * prev_ok
    a_next = _norm_modulate(xn_ref[0], gain, shift, scale) * next_ok
    a_ext = jnp.concatenate([a_prev, a_main, a_next], axis=0).astype(_BF16)
    a_mb = a_main.astype(_BF16)

    cos, sin = _tile_table(rcos_ref, ccos_ref), _tile_table(rsin_ref, csin_ref)
    lane = lax.broadcasted_iota(jnp.int32, (1, HEAD_DIM), 1)
    first_half = (lane % (2 * ROPE_FREQS)) < ROPE_FREQS

    o_k = ATTN_WIDTH
    o_v = o_k + KV_WIDTH
    o_b = o_v + KV_WIDTH
    o_c = o_b + CONV_WIDTH

    q = _mm(a_mb, w_ref[:, 0:o_k])
    k = _mm(a_mb, w_ref[:, o_k:o_v])
    for h in range(N_Q_HEADS):
        sl = slice(h * HEAD_DIM, (h + 1) * HEAD_DIM)
        qh = _rope(_head_rms(q[:, sl], qg_ref[...]), cos, sin, first_half)
        qt_ref[0, sl, :] = (qh * SCORE_SCALE_LOG2).T.astype(_BF16)

    v = _mm(a_mb, w_ref[:, o_v:o_b])
    for h in range(N_KV_HEADS):
        sl = slice(h * HEAD_DIM, (h + 1) * HEAD_DIM)
        k_ref[0, :, sl] = _rope(_head_rms(k[:, sl], kg_ref[...]), cos, sin, first_half).astype(_BF16)

    cx = _mm(a_ext, w_ref[:, o_c:])
    vt_ref[0] = v.T.astype(_BF16)
    gate_b = _mm(a_mb, w_ref[:, o_b:o_c])
    u = cx[:, :CONV_WIDTH] * cx[:, CONV_WIDTH:]
    cw = cw_ref[...]
    conv = cw[0:1] * _shift_rows(u, 1) + cw[1:2] * u + cw[2:3] * _shift_rows(u, -1)
    conv_ref[0] = (gate_b * conv[HALO:HALO + tile]).astype(_BF16)


def _in_proj(x, mods, gains, w_in, q_gain, k_gain, conv_w, n_keys):
    b, n, d = x.shape
    tile = WIDE_TILE
    rows_per_tile = tile // GRID_W
    assert tile % GRID_W == 0
    prev_spec, main_spec, next_spec = _halo_specs(tile, HALO, n, d)
    rope = _rope_axis_tables(n // GRID_W)
    row_table = pl.BlockSpec((rows_per_tile, HEAD_DIM), lambda bi, i: (i, 0))
    col_table = _resident((GRID_W, HEAD_DIM), lambda bi, i: (0, 0))
    return pl.pallas_call(
        _in_proj_kernel,
        grid=(b, n // tile),
        in_specs=[
            prev_spec, main_spec, next_spec,
            _mod_spec(d, 0, lambda bi, i: bi),
            _layer_spec(gains, 0),
            _layer_spec(w_in, 0),
            _layer_spec(q_gain, 0), _layer_spec(k_gain, 0),
            _layer_spec(conv_w, 0),
            row_table, row_table, col_table, col_table,
        ],
        out_specs=[pl.BlockSpec((1, ATTN_WIDTH, tile), lambda bi, i: (bi, 0, i)),
                   pl.BlockSpec((1, tile, KV_WIDTH), lambda bi, i: (bi, i, 0)),
                   pl.BlockSpec((1, KV_WIDTH, tile), lambda bi, i: (bi, 0, i)),
                   pl.BlockSpec((1, tile, CONV_WIDTH), lambda bi, i: (bi, i, 0))],
        out_shape=[jax.ShapeDtypeStruct((b, ATTN_WIDTH, n), _BF16), jax.ShapeDtypeStruct((b, n_keys, KV_WIDTH), _BF16),
                   jax.ShapeDtypeStruct((b, KV_WIDTH, n_keys), _BF16), jax.ShapeDtypeStruct((b, n, CONV_WIDTH), _BF16)],
        compiler_params=_compiler_params(2),
        name="in_proj",
    )(x, x, x, mods, gains, w_in, q_gain, k_gain, conv_w, rope[0], rope[1], rope[2], rope[3])


def _ctx_kv_kernel(ctx_ref, mod_ref, gain_ref, w_ref, kg_ref, k_in_ref, vt_in_ref, k_ref, vt_ref):
    del k_in_ref, vt_in_ref
    a = _norm_modulate(ctx_ref[0], gain_ref[...], mod_ref[0:1, :], mod_ref[1:2, :]).astype(_BF16)
    kv = _mm(a, w_ref[...])
    for h in range(N_KV_HEADS):
        sl = slice(h * HEAD_DIM, (h + 1) * HEAD_DIM)
        k_ref[0, :, sl] = _head_rms(kv[:, sl], kg_ref[...]).astype(_BF16)
    vt_ref[0] = kv[:, KV_WIDTH:].T.astype(_BF16)


def _ctx_kv(ctx, mods, gains, w_in, k_gain, k_all, vt_all, n_latent):
    b, lc, d = ctx.shape
    assert n_latent % lc == 0 and 2 * KV_WIDTH == ATTN_WIDTH
    ctx_block = n_latent // lc
    return pl.pallas_call(
        _ctx_kv_kernel,
        grid=(b,),
        in_specs=[
            pl.BlockSpec((1, lc, d), lambda i: (i, 0, 0)),
            _mod_spec(d, 0, lambda i: b),
            _layer_spec(gains, 0),
            _resident((None, d, 2 * KV_WIDTH), lambda i: (0, 0, 1)),
            _layer_spec(k_gain, 0),
            pl.BlockSpec(memory_space=pl.ANY), pl.BlockSpec(memory_space=pl.ANY),
        ],
        out_specs=[pl.BlockSpec((1, lc, KV_WIDTH), lambda i: (i, ctx_block, 0)),
                   pl.BlockSpec((1, KV_WIDTH, lc), lambda i: (i, 0, ctx_block))],
        out_shape=[jax.ShapeDtypeStruct(k_all.shape, _BF16), jax.ShapeDtypeStruct(vt_all.shape, _BF16)],
        input_output_aliases={5: 0, 6: 1},
        compiler_params=_compiler_params(1),
        name="ctx_kv",
    )(ctx, mods, gains, w_in, k_gain, k_all, vt_all)


def _attn_kernel(qt_ref, k_ref, vt_ref, conv_ref, x_ref, mod_ref, wo_ref, o_ref,
                 s_ref, cmax_ref, m_ref, acc_ref, attn_ref):
    n_chunks = k_ref.shape[1] // KV_TILE

    def keys_of(j):
        if isinstance(j, int):
            return slice(j * KV_TILE, (j + 1) * KV_TILE)
        return pl.ds(pl.multiple_of(j * KV_TILE, KV_TILE), KV_TILE)

    def score_stage(j, head):
        g = head // Q_PER_KV
        kch = k_ref[0, keys_of(j), g * HEAD_DIM:(g + 1) * HEAD_DIM]
        qt = qt_ref[0, head * HEAD_DIM:(head + 1) * HEAD_DIM, :]
        s = jnp.dot(kch, qt, preferred_element_type=_F32)
        s_ref[head % 2] = s
        cmax_ref[head % 2] = jnp.max(s, axis=0, keepdims=True)

    def value_stage(j, head):
        g = head // Q_PER_KV
        vtch = vt_ref[0, g * HEAD_DIM:(g + 1) * HEAD_DIM, keys_of(j)]
        v_aug = jnp.concatenate([vtch, jnp.ones((ONES_ROWS, KV_TILE), _BF16)], axis=0)
        m_prev = m_ref[head]
        m_new = jnp.maximum(m_prev, cmax_ref[head % 2])
        alpha = jnp.exp2(m_prev - m_new)
        p = jnp.exp2(s_ref[head % 2] - m_new).astype(_BF16)
        acc_ref[head] = alpha * acc_ref[head] + jnp.dot(v_aug, p, preferred_element_type=_F32)
        m_ref[head] = m_new

    def chunk(j, last):
        for head in range(N_Q_HEADS):
            if head + 1 < N_Q_HEADS:
                score_stage(j, head + 1)
            elif not last:
                score_stage(j + 1, 0)
            value_stage(j, head)

    m_ref[...] = jnp.full(m_ref.shape, -jnp.inf, _F32)
    acc_ref[...] = jnp.zeros(acc_ref.shape, _F32)
    score_stage(0, 0)

    def body(j, carry):
        chunk(j, last=False)
        return carry

    lax.fori_loop(0, n_chunks - 1, body, 0, unroll=5)
    chunk(n_chunks - 1, last=True)
    for head in range(N_Q_HEADS):
        acc = acc_ref[head]
        out_t = acc[:HEAD_DIM] / acc[HEAD_DIM:HEAD_DIM + 1]
        attn_ref[:, head * HEAD_DIM:(head + 1) * HEAD_DIM] = out_t.T.astype(_BF16)

    mixed = jnp.concatenate([attn_ref[...], conv_ref[0]], axis=-1)
    y = _mm(mixed, wo_ref[...])
    o_ref[0] = x_ref[0] + mod_ref[2:3, :] * y


def _attn_out(qt, k_all, vt_all, conv, x, mods, w_out):
    b, n, d = x.shape
    n_keys = k_all.shape[1]
    assert n_keys % KV_TILE == 0
    tq = Q_TILE
    per_batch = lambda rows, width: pl.BlockSpec((1, rows, width), lambda bi, i: (bi, 0, 0))
    tile = lambda width: pl.BlockSpec((1, tq, width), lambda bi, i: (bi, i, 0))
    return pl.pallas_call(
        _attn_kernel,
        grid=(b, n // tq),
        in_specs=[
            pl.BlockSpec((1, ATTN_WIDTH, tq), lambda bi, i: (bi, 0, i)),
            per_batch(n_keys, KV_WIDTH), per_batch(KV_WIDTH, n_keys),
            tile(CONV_WIDTH),
            tile(d),
            _mod_spec(d, 0, lambda bi, i: bi),
            _layer_spec(w_out, 0),
        ],
        out_specs=tile(d),
        out_shape=jax.ShapeDtypeStruct((b, n, d), _F32),
        scratch_shapes=[
            pltpu.VMEM((2, KV_TILE, tq), _F32),
            pltpu.VMEM((2, 1, tq), _F32),
            pltpu.VMEM((N_Q_HEADS, 1, tq), _F32),
            pltpu.VMEM((N_Q_HEADS, HEAD_DIM + ONES_ROWS, tq), _F32),
            pltpu.VMEM((tq, ATTN_WIDTH), _BF16),
        ],
        compiler_params=_compiler_params(2),
        name="attn_out",
    )(qt, k_all, vt_all, conv, x, mods, w_out)


def _ffn_kernel(xp_ref, x_ref, xn_ref, mod_ref, gain_ref, wup_ref, cwb_ref, wd_ref, o_ref, h_ref):
    tile = x_ref.shape[1]
    gain, shift, scale = gain_ref[...], mod_ref[3:4, :], mod_ref[4:5, :]
    x = x_ref[0]
    prev_ok, next_ok = _edge_flags()
    f_main = _norm_modulate(x, gain, shift, scale)
    f_prev = _norm_modulate(xp_ref[0], gain, shift, scale) * prev_ok
    f_next = _norm_modulate(xn_ref[0], gain, shift, scale) * next_ok
    f_ext = jnp.concatenate([f_prev, f_main, f_next], axis=0).astype(_BF16)
    f_mb = f_main.astype(_BF16)
    n_chunks = D_FF // FF_CHUNK

    def up(c):
        cols = slice(c * FF_CHUNK, (c + 1) * FF_CHUNK)
        val_cols = slice(D_FF + c * FF_CHUNK, D_FF + (c + 1) * FF_CHUNK)
        return _mm(f_ext, wup_ref[:, cols]), _mm(f_mb, wup_ref[:, val_cols])

    def act(c, gate, val):
        cols = slice(c * FF_CHUNK, (c + 1) * FF_CHUNK)
        cwb = cwb_ref[:, cols]
        conv = cwb[0:1] * _shift_rows(gate, 1) + cwb[1:2] * gate + cwb[2:3] * _shift_rows(gate, -1)
        h_ref[:, cols] = (_silu(conv[HALO:HALO + tile] + cwb[3:4]) * val).astype(_BF16)

    nxt = up(0)
    for c in range(n_chunks):
        cur = nxt
        if c + 1 < n_chunks:
            nxt = up(c + 1)
        act(c, *cur)
    y = _mm(h_ref[...], wd_ref[...])
    o_ref[0] = x + mod_ref[5:6, :] * y


def _ffn(x, mods, gains, w_up, conv_wb, w_down, layer):
    b, n, d = x.shape
    tile = TOKEN_TILE
    prev_spec, main_spec, next_spec = _halo_specs(tile, HALO, n, d)
    return pl.pallas_call(
        _ffn_kernel,
        grid=(b, n // tile),
        in_specs=[
            prev_spec, main_spec, next_spec,
            _mod_spec(d, layer, lambda bi, i: bi),
            _layer_spec(gains, layer),
            _layer_spec(w_up, layer), _layer_spec(conv_wb, layer), _layer_spec(w_down, layer),
        ],
        out_specs=main_spec,
        out_shape=jax.ShapeDtypeStruct((b, n, d), _F32),
        scratch_shapes=[pltpu.VMEM((tile, D_FF), _BF16)],
        compiler_params=_compiler_params(2),
        name="ffn",
    )(x, x, x, mods, gains, w_up, conv_wb, w_down)


def _pool_kernel(xp_ref, x_ref, xn_ref, mod_ref, gain_ref, pw_ref, ps_ref, o_ref):
    tile = x_ref.shape[1]
    halo = xp_ref.shape[1]
    n_total = tile * pl.num_programs(1)
    gain, shift, scale = gain_ref[...], mod_ref[0:1, :], mod_ref[1:2, :]
    x = x_ref[0]
    prev_ok, next_ok = _edge_flags()
    a_prev = _norm_modulate(xp_ref[0], gain, shift, scale) * prev_ok
    a_main = _norm_modulate(x, gain, shift, scale)
    a_next = _norm_modulate(xn_ref[0], gain, shift, scale) * next_ok
    a_ext = jnp.concatenate([a_prev, a_main, a_next], axis=0)
    t = pl.program_id(1) * tile + lax.broadcasted_iota(jnp.int32, (tile, 1), 0)

    mixed = []
    for gi, w in enumerate(POOL_WINDOWS):
        a = a_ext[:, gi * POOL_GROUP:(gi + 1) * POOL_GROUP]
        s = a
        k = 1
        while 2 * k < w:
            s = s + _shift_rows(s, -k)
            k *= 2
        s = _shift_rows(s, k) + s
        count = jnp.minimum(t + w - w // 2, n_total) - jnp.maximum(t - w // 2, 0)
        pooled = s[halo:halo + tile] * (1.0 / count.astype(_F32)) - a[halo:halo + tile]
        mixed.append(_mm(pooled.astype(_BF16), pw_ref[gi]))
    o_ref[0] = x + jnp.concatenate(mixed, axis=-1) * (ps_ref[...] * mod_ref[2:3, :])


def _pool(x, mods, gains, pool_w, pool_scale, layer):
    b, n, d = x.shape
    tile = WIDE_TILE
    prev_spec, main_spec, next_spec = _halo_specs(tile, POOL_HALO, n, d)
    return pl.pallas_call(
        _pool_kernel,
        grid=(b, n // tile),
        in_specs=[
            prev_spec, main_spec, next_spec,
            _mod_spec(d, layer, lambda bi, i: bi),
            _layer_spec(gains, layer),
            _layer_spec(pool_w, layer // 2),
            _layer_spec(pool_scale, layer // 2),
        ],
        out_specs=main_spec,
        out_shape=jax.ShapeDtypeStruct((b, n, d), _F32),
        compiler_params=_compiler_params(2),
        name="pool",
    )(x, x, x, mods, gains, pool_w, pool_scale)


def kernel(x, c, ctx, c_ctx, ada_w, ada_b, mix_norm, ffn_norm, even_w_in, even_q_gain, even_k_gain, even_conv_w,
           even_w_out, odd_pool_w, odd_pool_scale, ffn_w_up, ffn_conv_w, ffn_conv_b, ffn_w_down):
    b, n, d = x.shape
    depth = ada_w.shape[0]
    lc = ctx.shape[1]
    assert depth == 2 and d == D_MODEL and n % TOKEN_TILE == 0 and n % Q_TILE == 0 and n % WIDE_TILE == 0

    cond = jnp.concatenate([c, c_ctx[None, :], jnp.zeros((SUBLANES - b - 1, d), _F32)], axis=0)
    mods = _adaln(cond.T, ada_w, ada_b).reshape(depth, SUBLANES, N_MOD, d)

    rows = lambda v: v.reshape(v.shape[0], 1, v.shape[-1])
    mix_gain, ffn_gain = rows(mix_norm), rows(ffn_norm)
    w_in, w_out, pool_w, w_up, w_down = even_w_in, even_w_out, odd_pool_w, ffn_w_up, ffn_w_down
    conv_wb = jnp.concatenate([ffn_conv_w, ffn_conv_b[:, None, :],
                               jnp.zeros((depth, SUBLANES - ffn_conv_w.shape[1] - 1, D_FF), _F32)], axis=1)

    qt, k_all, vt_all, conv = _in_proj(x, mods, mix_gain, w_in, rows(even_q_gain), rows(even_k_gain),
                                       even_conv_w, n + lc)
    k_all, vt_all = _ctx_kv(ctx, mods, mix_gain, w_in, rows(even_k_gain), k_all, vt_all, n)
    x = _attn_out(qt, k_all, vt_all, conv, x, mods, w_out)
    x = _ffn(x, mods, ffn_gain, w_up, conv_wb, w_down, 0)

    x = _pool(x, mods, mix_gain, pool_w, rows(odd_pool_scale), 1)
    x = _ffn(x, mods, ffn_gain, w_up, conv_wb, w_down, 1)
    return x
```

```python
import jax
import jax.numpy as jnp
from jax import lax
from jax.experimental import pallas as pl
from jax.experimental.pallas import tpu as pltpu

D_MODEL = 1024
GRID_W = 64
HEAD_DIM = 128
N_Q_HEADS = 4
N_KV_HEADS = 2
Q_PER_KV = N_Q_HEADS // N_KV_HEADS
ATTN_WIDTH = N_Q_HEADS * HEAD_DIM
KV_WIDTH = N_KV_HEADS * HEAD_DIM
CONV_WIDTH = D_MODEL - ATTN_WIDTH
IN_PROJ_WIDTH = ATTN_WIDTH + 2 * KV_WIDTH + 3 * CONV_WIDTH
ROPE_THETA = 10000.0
ROPE_FREQS = HEAD_DIM // 4
SCORE_SCALE_LOG2 = HEAD_DIM ** -0.5 * 1.4426950408889634
POOL_WINDOWS = (2, 4, 8, 16)
POOL_GROUP = D_MODEL // len(POOL_WINDOWS)
D_FF = 2816
N_MOD = 6
EPS = 1e-6

SUBLANES = 8
VMEM_LIMIT_BYTES = 56 * 1024 * 1024

HALO = SUBLANES
POOL_HALO = 2 * SUBLANES
TOKEN_TILE = 512
WIDE_TILE = 1024
Q_TILE = 512
KV_TILE = 768
ONES_ROWS = 16
FF_CHUNK = 256
ADALN_COLS = 1536

_BF16 = jnp.bfloat16
_F32 = jnp.float32


def _compiler_params(n_axes):
    return pltpu.CompilerParams(dimension_semantics=("arbitrary",) * n_axes,
                                vmem_limit_bytes=VMEM_LIMIT_BYTES)


def _resident(shape, index_map):
    return pl.BlockSpec(shape, index_map, pipeline_mode=pl.Buffered(1))


def _layer_spec(arr, layer):
    zeros = (0,) * (arr.ndim - 1)
    return _resident((None,) + arr.shape[1:], lambda *_: (layer,) + zeros)


def _mod_spec(d, layer, row_of):
    return pl.BlockSpec((None, None, N_MOD, d), lambda *ids: (layer, row_of(*ids), 0, 0))


def _norm_modulate(x, gain, shift, scale):
    y = x * lax.rsqrt(jnp.mean(x * x, axis=-1, keepdims=True) + EPS)
    return y * (gain * (1.0 + scale)) + shift


def _mm(a, w):
    return lax.dot_general(a, w, (((1,), (0,)), ((), ())), preferred_element_type=_F32)


def _silu(x):
    return x * (1.0 / (1.0 + jnp.exp(-x)))


def _shift_rows(u, k):
    return pltpu.roll(u, k % u.shape[0], axis=0)


def _adaln_kernel(ct_ref, w_ref, b_ref, o_ref):
    s = _silu(ct_ref[...])
    w = w_ref[0]
    n_vec = 3
    rows = [jnp.sum(s[:, r:r + 1] * w, axis=0, keepdims=True) + b_ref[0] for r in range(n_vec)]
    rows.append(jnp.zeros((SUBLANES - n_vec, w.shape[1]), _F32))
    o_ref[0] = jnp.concatenate(rows, axis=0)


def _adaln(cond_t, ada_w, ada_b):
    depth, d, width = ada_w.shape
    return pl.pallas_call(
        _adaln_kernel,
        grid=(depth, width // ADALN_COLS),
        in_specs=[
            pl.BlockSpec((d, SUBLANES), lambda l, j: (0, 0)),
            pl.BlockSpec((1, d, ADALN_COLS), lambda l, j: (l, 0, j)),
            pl.BlockSpec((1, 1, ADALN_COLS), lambda l, j: (l, 0, j)),
        ],
        out_specs=pl.BlockSpec((1, SUBLANES, ADALN_COLS), lambda l, j: (l, 0, j)),
        out_shape=jax.ShapeDtypeStruct((depth, SUBLANES, width), _F32),
        compiler_params=_compiler_params(2),
        name="adaln",
    )(cond_t, ada_w, ada_b.reshape(depth, 1, width))


def _head_rms(xh, gain):
    return xh * lax.rsqrt(jnp.mean(xh * xh, axis=-1, keepdims=True) + EPS) * gain


def _rope(xh, cos, sin_signed, first_half):
    partner = jnp.where(first_half, pltpu.roll(xh, HEAD_DIM - ROPE_FREQS, axis=1), pltpu.roll(xh, ROPE_FREQS, axis=1))
    return xh * cos + partner * sin_signed


def _rope_axis_tables(n_rows):
    axis_dim = HEAD_DIM // 2
    inv_freq = jnp.power(ROPE_THETA, -jnp.arange(0, axis_dim, 2, dtype=_F32) / axis_dim)
    ang_r = jnp.arange(n_rows, dtype=_F32)[:, None] * inv_freq
    ang_c = jnp.arange(GRID_W, dtype=_F32)[:, None] * inv_freq
    zr, zc = jnp.zeros((n_rows, axis_dim), _F32), jnp.zeros((GRID_W, axis_dim), _F32)
    return (jnp.concatenate([jnp.cos(ang_r), jnp.cos(ang_r), zr], axis=1),
            jnp.concatenate([-jnp.sin(ang_r), jnp.sin(ang_r), zr], axis=1),
            jnp.concatenate([zc, jnp.cos(ang_c), jnp.cos(ang_c)], axis=1),
            jnp.concatenate([zc, -jnp.sin(ang_c), jnp.sin(ang_c)], axis=1))


def _tile_table(row_ref, col_ref):
    col = col_ref[...]
    return jnp.concatenate([jnp.broadcast_to(row_ref[r:r + 1, :], col.shape) + col for r in range(row_ref.shape[0])],
                           axis=0)


def _halo_specs(tile, halo, n, d):
    per_tile = tile // halo
    n_halo_blocks = n // halo
    prev_spec = pl.BlockSpec((1, halo, d), lambda b, i: (b, jnp.maximum(i * per_tile - 1, 0), 0))
    main_spec = pl.BlockSpec((1, tile, d), lambda b, i: (b, i, 0))
    next_spec = pl.BlockSpec((1, halo, d), lambda b, i: (b, jnp.minimum((i + 1) * per_tile, n_halo_blocks - 1), 0))
    return prev_spec, main_spec, next_spec


def _edge_flags():
    i = pl.program_id(1)
    return (i > 0).astype(_F32), (i < pl.num_programs(1) - 1).astype(_F32)


def _in_proj_kernel(xp_ref, x_ref, xn_ref, mod_ref, gain_ref, w_ref, qg_ref, kg_ref, cw_ref,
                    rcos_ref, rsin_ref, ccos_ref, csin_ref, qt_ref, k_ref, vt_ref, conv_ref,
                    aext_ref, amain_ref, qkv_ref, cx_ref, gb_ref):
    tile = x_ref.shape[1]
    o_k = ATTN_WIDTH
    o_v = o_k + KV_WIDTH
    o_b = o_v + KV_WIDTH
    o_c = o_b + CONV_WIDTH
    every_step = pl.program_id(1) < pl.num_programs(1)

    @pl.when(every_step)
    def _():
        gain, shift, scale = gain_ref[...], mod_ref[0:1, :], mod_ref[1:2, :]
        prev_ok, next_ok = _edge_flags()
        a_main = _norm_modulate(x_ref[0], gain, shift, scale)
        a_prev = _norm_modulate(xp_ref[0], gain, shift, scale) * prev_ok
        a_next = _norm_modulate(xn_ref[0], gain, shift, scale) * next_ok
        aext_ref[...] = jnp.concatenate([a_prev, a_main, a_next], axis=0).astype(_BF16)
        a_mb = a_main.astype(_BF16)
        amain_ref[...] = a_mb
        qkv_ref[...] = _mm(a_mb, w_ref[:, 0:o_b])

    cx_ref[...] = _mm(aext_ref[...], w_ref[:, o_c:])
    gb_ref[...] = _mm(amain_ref[...], w_ref[:, o_b:o_c])
    cos, sin = _tile_table(rcos_ref, ccos_ref), _tile_table(rsin_ref, csin_ref)
    lane = lax.broadcasted_iota(jnp.int32, (1, HEAD_DIM), 1)
    first_half = (lane % (2 * ROPE_FREQS)) < ROPE_FREQS
    for h in range(N_Q_HEADS):
        sl = slice(h * HEAD_DIM, (h + 1) * HEAD_DIM)
        qh = _rope(_head_rms(qkv_ref[:, sl], qg_ref[...]), cos, sin, first_half)
        qt_ref[0, sl, :] = (qh * SCORE_SCALE_LOG2).T.astype(_BF16)
    for h in range(N_KV_HEADS):
        sl = slice(h * HEAD_DIM, (h + 1) * HEAD_DIM)
        kh = qkv_ref[:, o_k + h * HEAD_DIM:o_k + (h + 1) * HEAD_DIM]
        k_ref[0, :, sl] = _rope(_head_rms(kh, kg_ref[...]), cos, sin, first_half).astype(_BF16)
    vt_ref[0] = qkv_ref[:, o_v:o_b].T.astype(_BF16)

    @pl.when(every_step)
    def _():
        cx = cx_ref[...]
        u = cx[:, :CONV_WIDTH] * cx[:, CONV_WIDTH:]
        cw = cw_ref[...]
        conv = cw[0:1] * _shift_rows(u, 1) + cw[1:2] * u + cw[2:3] * _shift_rows(u, -1)
        conv_ref[0] = (gb_ref[...] * conv[HALO:HALO + tile]).astype(_BF16)


def _in_proj(x, mods, gains, w_in, q_gain, k_gain, conv_w, n_keys):
    b, n, d = x.shape
    tile = WIDE_TILE
    rows_per_tile = tile // GRID_W
    assert tile % GRID_W == 0
    prev_spec, main_spec, next_spec = _halo_specs(tile, HALO, n, d)
    rope = _rope_axis_tables(n // GRID_W)
    row_table = pl.BlockSpec((rows_per_tile, HEAD_DIM), lambda bi, i: (i, 0))
    col_table = _resident((GRID_W, HEAD_DIM), lambda bi, i: (0, 0))
    return pl.pallas_call(
        _in_proj_kernel,
        grid=(b, n // tile),
        in_specs=[
            prev_spec, main_spec, next_spec,
            _mod_spec(d, 0, lambda bi, i: bi),
            _layer_spec(gains, 0),
            _layer_spec(w_in, 0),
            _layer_spec(q_gain, 0), _layer_spec(k_gain, 0),
            _layer_spec(conv_w, 0),
            row_table, row_table, col_table, col_table,
        ],
        out_specs=[pl.BlockSpec((1, ATTN_WIDTH, tile), lambda bi, i: (bi, 0, i)),
                   pl.BlockSpec((1, tile, KV_WIDTH), lambda bi, i: (bi, i, 0)),
                   pl.BlockSpec((1, KV_WIDTH, tile), lambda bi, i: (bi, 0, i)),
                   pl.BlockSpec((1, tile, CONV_WIDTH), lambda bi, i: (bi, i, 0))],
        out_shape=[jax.ShapeDtypeStruct((b, ATTN_WIDTH, n), _BF16), jax.ShapeDtypeStruct((b, n_keys, KV_WIDTH), _BF16),
                   jax.ShapeDtypeStruct((b, KV_WIDTH, n_keys), _BF16), jax.ShapeDtypeStruct((b, n, CONV_WIDTH), _BF16)],
        scratch_shapes=[
            pltpu.VMEM((tile + 2 * HALO, d), _BF16),
            pltpu.VMEM((tile, d), _BF16),
            pltpu.VMEM((tile, ATTN_WIDTH + 2 * KV_WIDTH), _F32),
            pltpu.VMEM((tile + 2 * HALO, 2 * CONV_WIDTH), _F32),
            pltpu.VMEM((tile, CONV_WIDTH), _F32),
        ],
        compiler_params=_compiler_params(2),
        name="in_proj",
    )(x, x, x, mods, gains, w_in, q_gain, k_gain, conv_w, rope[0], rope[1], rope[2], rope[3])


def _ctx_kv_kernel(ctx_ref, mod_ref, gain_ref, w_ref, kg_ref, k_in_ref, vt_in_ref, k_ref, vt_ref):
    del k_in_ref, vt_in_ref
    a = _norm_modulate(ctx_ref[0], gain_ref[...], mod_ref[0:1, :], mod_ref[1:2, :]).astype(_BF16)
    kv = _mm(a, w_ref[...])
    for h in range(N_KV_HEADS):
        sl = slice(h * HEAD_DIM, (h + 1) * HEAD_DIM)
        k_ref[0, :, sl] = _head_rms(kv[:, sl], kg_ref[...]).astype(_BF16)
    vt_ref[0] = kv[:, KV_WIDTH:].T.astype(_BF16)


def _ctx_kv(ctx, mods, gains, w_in, k_gain, k_all, vt_all, n_latent):
    b, lc, d = ctx.shape
    assert n_latent % lc == 0 and 2 * KV_WIDTH == ATTN_WIDTH
    ctx_block = n_latent // lc
    return pl.pallas_call(
        _ctx_kv_kernel,
        grid=(b,),
        in_specs=[
            pl.BlockSpec((1, lc, d), lambda i: (i, 0, 0)),
            _mod_spec(d, 0, lambda i: b),
            _layer_spec(gains, 0),
            _resident((None, d, 2 * KV_WIDTH), lambda i: (0, 0, 1)),
            _layer_spec(k_gain, 0),
            pl.BlockSpec(memory_space=pl.ANY), pl.BlockSpec(memory_space=pl.ANY),
        ],
        out_specs=[pl.BlockSpec((1, lc, KV_WIDTH), lambda i: (i, ctx_block, 0)),
                   pl.BlockSpec((1, KV_WIDTH, lc), lambda i: (i, 0, ctx_block))],
        out_shape=[jax.ShapeDtypeStruct(k_all.shape, _BF16), jax.ShapeDtypeStruct(vt_all.shape, _BF16)],
        input_output_aliases={5: 0, 6: 1},
        compiler_params=_compiler_params(1),
        name="ctx_kv",
    )(ctx, mods, gains, w_in, k_gain, k_all, vt_all)


def _attn_kernel(qt_ref, k_ref, vt_ref, conv_ref, x_ref, mod_ref, wo_ref, o_ref,
                 s_ref, cmax_ref, m_ref, acc_ref, attn_ref):
    n_chunks = k_ref.shape[1] // KV_TILE

    def keys_of(j):
        if isinstance(j, int):
            return slice(j * KV_TILE, (j + 1) * KV_TILE)
        return pl.ds(pl.multiple_of(j * KV_TILE, KV_TILE), KV_TILE)

    def score_stage(j, head):
        g = head // Q_PER_KV
        kch = k_ref[0, keys_of(j), g * HEAD_DIM:(g + 1) * HEAD_DIM]
        qt = qt_ref[0, head * HEAD_DIM:(head + 1) * HEAD_DIM, :]
        s = jnp.dot(kch, qt, preferred_element_type=_F32)
        s_ref[head % 2] = s
        cmax_ref[head % 2] = jnp.max(s, axis=0, keepdims=True)

    def value_stage(j, head):
        g = head // Q_PER_KV
        vtch = vt_ref[0, g * HEAD_DIM:(g + 1) * HEAD_DIM, keys_of(j)]
        v_aug = jnp.concatenate([vtch, jnp.ones((ONES_ROWS, KV_TILE), _BF16)], axis=0)
        m_prev = m_ref[head]
        m_new = jnp.maximum(m_prev, cmax_ref[head % 2])
        alpha = jnp.exp2(m_prev - m_new)
        p = jnp.exp2(s_ref[head % 2] - m_new).astype(_BF16)
        acc_ref[head] = alpha * acc_ref[head] + jnp.dot(v_aug, p, preferred_element_type=_F32)
        m_ref[head] = m_new

    def chunk(j, last):
        for head in range(N_Q_HEADS):
            if head + 1 < N_Q_HEADS:
                score_stage(j, head + 1)
            elif not last:
                score_stage(j + 1, 0)
            value_stage(j, head)

    m_ref[...] = jnp.full(m_ref.shape, -jnp.inf, _F32)
    acc_ref[...] = jnp.zeros(acc_ref.shape, _F32)
    score_stage(0, 0)

    def body(j, carry):
        chunk(j, last=False)
        return carry

    lax.fori_loop(0, n_chunks - 1, body, 0, unroll=5)
    chunk(n_chunks - 1, last=True)
    for head in range(N_Q_HEADS):
        acc = acc_ref[head]
        out_t = acc[:HEAD_DIM] / acc[HEAD_DIM:HEAD_DIM + 1]
        attn_ref[:, head * HEAD_DIM:(head + 1) * HEAD_DIM] = out_t.T.astype(_BF16)

    mixed = jnp.concatenate([attn_ref[...], conv_ref[0]], axis=-1)
    y = _mm(mixed, wo_ref[...])
    o_ref[0] = x_ref[0] + mod_ref[2:3, :] * y


def _attn_out(qt, k_all, vt_all, conv, x, mods, w_out):
    b, n, d = x.shape
    n_keys = k_all.shape[1]
    assert n_keys % KV_TILE == 0
    tq = Q_TILE
    per_batch = lambda rows, width: pl.BlockSpec((1, rows, width), lambda bi, i: (bi, 0, 0))
    tile = lambda width: pl.BlockSpec((1, tq, width), lambda bi, i: (bi, i, 0))
    return pl.pallas_call(
        _attn_kernel,
        grid=(b, n // tq),
        in_specs=[
            pl.BlockSpec((1, ATTN_WIDTH, tq), lambda bi, i: (bi, 0, i)),
            per_batch(n_keys, KV_WIDTH), per_batch(KV_WIDTH, n_keys),
            tile(CONV_WIDTH),
            tile(d),
            _mod_spec(d, 0, lambda bi, i: bi),
            _layer_spec(w_out, 0),
        ],
        out_specs=tile(d),
        out_shape=jax.ShapeDtypeStruct((b, n, d), _F32),
        scratch_shapes=[
            pltpu.VMEM((2, KV_TILE, tq), _F32),
            pltpu.VMEM((2, 1, tq), _F32),
            pltpu.VMEM((N_Q_HEADS, 1, tq), _F32),
            pltpu.VMEM((N_Q_HEADS, HEAD_DIM + ONES_ROWS, tq), _F32),
            pltpu.VMEM((tq, ATTN_WIDTH), _BF16),
        ],
        compiler_params=_compiler_params(2),
        name="attn_out",
    )(qt, k_all, vt_all, conv, x, mods, w_out)


def _ffn_kernel(xp_ref, x_ref, xn_ref, mod_ref, gain_ref, wup_ref, cwb_ref, wd_ref, o_ref, h_ref):
    tile = x_ref.shape[1]
    gain, shift, scale = gain_ref[...], mod_ref[3:4, :], mod_ref[4:5, :]
    x = x_ref[0]
    prev_ok, next_ok = _edge_flags()
    f_main = _norm_modulate(x, gain, shift, scale)
    f_prev = _norm_modulate(xp_ref[0], gain, shift, scale) * prev_ok
    f_next = _norm_modulate(xn_ref[0], gain, shift, scale) * next_ok
    f_ext = jnp.concatenate([f_prev, f_main, f_next], axis=0).astype(_BF16)
    f_mb = f_main.astype(_BF16)
    n_chunks = D_FF // FF_CHUNK

    def up(c):
        cols = slice(c * FF_CHUNK, (c + 1) * FF_CHUNK)
        val_cols = slice(D_FF + c * FF_CHUNK, D_FF + (c + 1) * FF_CHUNK)
        return _mm(f_ext, wup_ref[:, cols]), _mm(f_mb, wup_ref[:, val_cols])

    def act(c, gate, val):
        cols = slice(c * FF_CHUNK, (c + 1) * FF_CHUNK)
        cwb = cwb_ref[:, cols]
        conv = cwb[0:1] * _shift_rows(gate, 1) + cwb[1:2] * gate + cwb[2:3] * _shift_rows(gate, -1)
        h_ref[:, cols] = (_silu(conv[HALO:HALO + tile] + cwb[3:4]) * val).astype(_BF16)

    nxt = up(0)
    for c in range(n_chunks):
        cur = nxt
        if c + 1 < n_chunks:
            nxt = up(c + 1)
        act(c, *cur)
    y = _mm(h_ref[...], wd_ref[...])
    o_ref[0] = x + mod_ref[5:6, :] * y


def _ffn(x, mods, gains, w_up, conv_wb, w_down, layer):
    b, n, d = x.shape
    tile = TOKEN_TILE
    prev_spec, main_spec, next_spec = _halo_specs(tile, HALO, n, d)
    return pl.pallas_call(
        _ffn_kernel,
        grid=(b, n // tile),
        in_specs=[
            prev_spec, main_spec, next_spec,
            _mod_spec(d, layer, lambda bi, i: bi),
            _layer_spec(gains, layer),
            _layer_spec(w_up, layer), _layer_spec(conv_wb, layer), _layer_spec(w_down, layer),
        ],
        out_specs=main_spec,
        out_shape=jax.ShapeDtypeStruct((b, n, d), _F32),
        scratch_shapes=[pltpu.VMEM((tile, D_FF), _BF16)],
        compiler_params=_compiler_params(2),
        name="ffn",
    )(x, x, x, mods, gains, w_up, conv_wb, w_down)


def _pool_kernel(xp_ref, x_ref, xn_ref, mod_ref, gain_ref, pw_ref, ps_ref, o_ref):
    tile = x_ref.shape[1]
    halo = xp_ref.shape[1]
    n_total = tile * pl.num_programs(1)
    gain, shift, scale = gain_ref[...], mod_ref[0:1, :], mod_ref[1:2, :]
    x = x_ref[0]
    prev_ok, next_ok = _edge_flags()
    a_prev = _norm_modulate(xp_ref[0], gain, shift, scale) * prev_ok
    a_main = _norm_modulate(x, gain, shift, scale)
    a_next = _norm_modulate(xn_ref[0], gain, shift, scale) * next_ok
    a_ext = jnp.concatenate([a_prev, a_main, a_next], axis=0)
    t = pl.program_id(1) * tile + lax.broadcasted_iota(jnp.int32, (tile, 1), 0)

    mixed = []
    for gi, w in enumerate(POOL_WINDOWS):
        a = a_ext[:, gi * POOL_GROUP:(gi + 1) * POOL_GROUP]
        s = a
        k = 1
        while 2 * k < w:
            s = s + _shift_rows(s, -k)
            k *= 2
        s = _shift_rows(s, k) + s
        count = jnp.minimum(t + w - w // 2, n_total) - jnp.maximum(t - w // 2, 0)
        pooled = s[halo:halo + tile] * (1.0 / count.astype(_F32)) - a[halo:halo + tile]
        mixed.append(_mm(pooled.astype(_BF16), pw_ref[gi]))
    o_ref[0] = x + jnp.concatenate(mixed, axis=-1) * (ps_ref[...] * mod_ref[2:3, :])


def _pool(x, mods, gains, pool_w, pool_scale, layer):
    b, n, d = x.shape
    tile = WIDE_TILE
    prev_spec, main_spec, next_spec = _halo_specs(tile, POOL_HALO, n, d)
    return pl.pallas_call(
        _pool_kernel,
        grid=(b, n // tile),
        in_specs=[
            prev_spec, main_spec, next_spec,
            _mod_spec(d, layer, lambda bi, i: bi),
            _layer_spec(gains, layer),
            _layer_spec(pool_w, layer // 2),
            _layer_spec(pool_scale, layer // 2),
        ],
        out_specs=main_spec,
        out_shape=jax.ShapeDtypeStruct((b, n, d), _F32),
        compiler_params=_compiler_params(2),
        name="pool",
    )(x, x, x, mods, gains, pool_w, pool_scale)


def kernel(x, c, ctx, c_ctx, ada_w, ada_b, mix_norm, ffn_norm, even_w_in, even_q_gain, even_k_gain, even_conv_w,
           even_w_out, odd_pool_w, odd_pool_scale, ffn_w_up, ffn_conv_w, ffn_conv_b, ffn_w_down):
    b, n, d = x.shape
    depth = ada_w.shape[0]
    lc = ctx.shape[1]
    assert depth == 2 and d == D_MODEL and n % TOKEN_TILE == 0 and n % Q_TILE == 0 and n % WIDE_TILE == 0

    cond = jnp.concatenate([c, c_ctx[None, :], jnp.zeros((SUBLANES - b - 1, d), _F32)], axis=0)
    mods = _adaln(cond.T, ada_w, ada_b).reshape(depth, SUBLANES, N_MOD, d)

    rows = lambda v: v.reshape(v.shape[0], 1, v.shape[-1])
    mix_gain, ffn_gain = rows(mix_norm), rows(ffn_norm)
    w_in, w_out, pool_w, w_up, w_down = even_w_in, even_w_out, odd_pool_w, ffn_w_up, ffn_w_down
    conv_wb = jnp.concatenate([ffn_conv_w, ffn_conv_b[:, None, :],
                               jnp.zeros((depth, SUBLANES - ffn_conv_w.shape[1] - 1, D_FF), _F32)], axis=1)

    qt, k_all, vt_all, conv = _in_proj(x, mods, mix_gain, w_in, rows(even_q_gain), rows(even_k_gain),
                                       even_conv_w, n + lc)
    k_all, vt_all = _ctx_kv(ctx, mods, mix_gain, w_in, rows(even_k_gain), k_all, vt_all, n)
    x = _attn_out(qt, k_all, vt_all, conv, x, mods, w_out)
    x = _ffn(x, mods, ffn_gain, w_up, conv_wb, w_down, 0)

    x = _pool(x, mods, mix_gain, pool_w, rows(odd_pool_scale), 1)
    x = _ffn(x, mods, ffn_gain, w_up, conv_wb, w_down, 1)
    return x
```

```python
import jax
import jax.numpy as jnp
from jax import lax
from jax.experimental import pallas as pl
from jax.experimental.pallas import tpu as pltpu

D_MODEL = 1024
GRID_W = 64
HEAD_DIM = 128
N_Q_HEADS = 4
N_KV_HEADS = 2
Q_PER_KV = N_Q_HEADS // N_KV_HEADS
ATTN_WIDTH = N_Q_HEADS * HEAD_DIM
KV_WIDTH = N_KV_HEADS * HEAD_DIM
CONV_WIDTH = D_MODEL - ATTN_WIDTH
IN_PROJ_WIDTH = ATTN_WIDTH + 2 * KV_WIDTH + 3 * CONV_WIDTH
ROPE_THETA = 10000.0
ROPE_FREQS = HEAD_DIM // 4
SCORE_SCALE_LOG2 = HEAD_DIM ** -0.5 * 1.4426950408889634
POOL_WINDOWS = (2, 4, 8, 16)
POOL_GROUP = D_MODEL // len(POOL_WINDOWS)
D_FF = 2816
N_MOD = 6
N_COND = 3
EPS = 1e-6

SUBLANES = 8
VMEM_LIMIT_BYTES = 56 * 1024 * 1024

HALO = SUBLANES
POOL_HALO = 2 * SUBLANES
TOKEN_TILE = 1024
WIDE_TILE = 1024
Q_TILE = 512
KV_TILE = 768
ONES_ROWS = 16
FF_CHUNK = 256
ADALN_COLS = 1536

_BF16 = jnp.bfloat16
_F32 = jnp.float32


def _compiler_params(n_axes):
    return pltpu.CompilerParams(dimension_semantics=("arbitrary",) * n_axes,
                                vmem_limit_bytes=VMEM_LIMIT_BYTES)


def _resident(shape, index_map):
    return pl.BlockSpec(shape, index_map, pipeline_mode=pl.Buffered(1))


def _layer_spec(arr, layer):
    zeros = (0,) * (arr.ndim - 1)
    return _resident((None,) + arr.shape[1:], lambda *_: (layer,) + zeros)


def _mod_spec(d, layer, row_of):
    return pl.BlockSpec((None, None, N_MOD, d), lambda *ids: (layer, row_of(*ids), 0, 0))


def _cast_plan(w, layer, grid):
    rows, cols = w.shape[1:]
    n_steps = grid[0] * grid[1]
    block, shared = rows // n_steps, 1
    while block % (2 * SUBLANES):
        block, shared = 2 * block, 2 * shared
    assert rows % block == 0 and n_steps * block == rows * shared
    block_of = lambda b, i: (b * grid[1] + i) // shared
    return (pl.BlockSpec((None, block, cols), lambda b, i: (layer, block_of(b, i), 0)),
            pl.BlockSpec((block, cols), lambda b, i: (block_of(b, i), 0)),
            jax.ShapeDtypeStruct((rows, cols), _BF16))


def _norm_modulate(x, gain, shift, scale):
    y = x * lax.rsqrt(jnp.mean(x * x, axis=-1, keepdims=True) + EPS)
    return y * (gain * (1.0 + scale)) + shift


def _mm(a, w):
    return lax.dot_general(a, w, (((1,), (0,)), ((), ())), preferred_element_type=_F32)


def _silu(x):
    return x * (1.0 / (1.0 + jnp.exp(-x)))


def _shift_rows(u, k):
    return pltpu.roll(u, k % u.shape[0], axis=0)


def _adaln_kernel(ct_ref, w_ref, b_ref, o_ref):
    s = _silu(ct_ref[...])
    w = w_ref[0]
    rows = [jnp.sum(s[:, r:r + 1] * w, axis=0, keepdims=True) + b_ref[0] for r in range(N_COND)]
    rows.append(jnp.zeros((SUBLANES - N_COND, w.shape[1]), _F32))
    o_ref[0] = jnp.concatenate(rows, axis=0)


def _adaln(cond_t, ada_w, ada_b):
    depth, d, width = ada_w.shape
    return pl.pallas_call(
        _adaln_kernel,
        grid=(depth, width // ADALN_COLS),
        in_specs=[
            pl.BlockSpec((d, SUBLANES), lambda l, j: (0, 0)),
            pl.BlockSpec((1, d, ADALN_COLS), lambda l, j: (l, 0, j)),
            pl.BlockSpec((1, 1, ADALN_COLS), lambda l, j: (l, 0, j)),
        ],
        out_specs=pl.BlockSpec((1, SUBLANES, ADALN_COLS), lambda l, j: (l, 0, j)),
        out_shape=jax.ShapeDtypeStruct((depth, SUBLANES, width), _F32),
        compiler_params=_compiler_params(2),
        name="adaln",
    )(cond_t, ada_w, ada_b.reshape(depth, 1, width))


def _head_rms(xh, gain):
    return xh * lax.rsqrt(jnp.mean(xh * xh, axis=-1, keepdims=True) + EPS) * gain


def _rope(xh, cos, sin_signed, first_half):
    partner = jnp.where(first_half, pltpu.roll(xh, HEAD_DIM - ROPE_FREQS, axis=1), pltpu.roll(xh, ROPE_FREQS, axis=1))
    return xh * cos + partner * sin_signed


def _rope_axis_tables(n_rows):
    axis_dim = HEAD_DIM // 2
    inv_freq = jnp.power(ROPE_THETA, -jnp.arange(0, axis_dim, 2, dtype=_F32) / axis_dim)
    ang_r = jnp.arange(n_rows, dtype=_F32)[:, None] * inv_freq
    ang_c = jnp.arange(GRID_W, dtype=_F32)[:, None] * inv_freq
    zr, zc = jnp.zeros((n_rows, axis_dim), _F32), jnp.zeros((GRID_W, axis_dim), _F32)
    return (jnp.concatenate([jnp.cos(ang_r), jnp.cos(ang_r), zr], axis=1),
            jnp.concatenate([-jnp.sin(ang_r), jnp.sin(ang_r), zr], axis=1),
            jnp.concatenate([zc, jnp.cos(ang_c), jnp.cos(ang_c)], axis=1),
            jnp.concatenate([zc, -jnp.sin(ang_c), jnp.sin(ang_c)], axis=1))


def _tile_table(row_ref, col_ref):
    col = col_ref[...]
    return jnp.concatenate([jnp.broadcast_to(row_ref[r:r + 1, :], col.shape) + col for r in range(row_ref.shape[0])],
                           axis=0)


def _halo_specs(tile, halo, n, d):
    per_tile = tile // halo
    n_halo_blocks = n // halo
    prev_spec = pl.BlockSpec((1, halo, d), lambda b, i: (b, jnp.maximum(i * per_tile - 1, 0), 0))
    main_spec = pl.BlockSpec((1, tile, d), lambda b, i: (b, i, 0))
    next_spec = pl.BlockSpec((1, halo, d), lambda b, i: (b, jnp.minimum((i + 1) * per_tile, n_halo_blocks - 1), 0))
    return prev_spec, main_spec, next_spec


def _edge_flags():
    i = pl.program_id(1)
    return (i > 0).astype(_F32), (i < pl.num_programs(1) - 1).astype(_F32)


def _in_proj_kernel(xp_ref, x_ref, xn_ref, mod_ref, gain_ref, w_ref, qg_ref, kg_ref, cw_ref,
                    rcos_ref, rsin_ref, ccos_ref, csin_ref, qt_ref, k_ref, vt_ref, conv_ref,
                    aext_ref, amain_ref, qkv_ref, cx_ref, gb_ref):
    tile = x_ref.shape[1]
    o_k = ATTN_WIDTH
    o_v = o_k + KV_WIDTH
    o_b = o_v + KV_WIDTH
    o_c = o_b + CONV_WIDTH
    every_step = pl.program_id(1) < pl.num_programs(1)

    @pl.when(every_step)
    def _():
        gain, shift, scale = gain_ref[...], mod_ref[0:1, :], mod_ref[1:2, :]
        prev_ok, next_ok = _edge_flags()
        a_main = _norm_modulate(x_ref[0], gain, shift, scale)
        a_prev = _norm_modulate(xp_ref[0], gain, shift, scale) * prev_ok
        a_next = _norm_modulate(xn_ref[0], gain, shift, scale) * next_ok
        aext_ref[...] = jnp.concatenate([a_prev, a_main, a_next], axis=0).astype(_BF16)
        a_mb = a_main.astype(_BF16)
        amain_ref[...] = a_mb
        qkv_ref[...] = _mm(a_mb, w_ref[:, 0:o_b])

    cx_ref[...] = _mm(aext_ref[...], w_ref[:, o_c:])
    gb_ref[...] = _mm(amain_ref[...], w_ref[:, o_b:o_c])
    cos, sin = _tile_table(rcos_ref, ccos_ref), _tile_table(rsin_ref, csin_ref)
    lane = lax.broadcasted_iota(jnp.int32, (1, HEAD_DIM), 1)
    first_half = (lane % (2 * ROPE_FREQS)) < ROPE_FREQS
    for h in range(N_Q_HEADS):
        sl = slice(h * HEAD_DIM, (h + 1) * HEAD_DIM)
        qh = _rope(_head_rms(qkv_ref[:, sl], qg_ref[...]), cos, sin, first_half)
        qt_ref[0, sl, :] = (qh * SCORE_SCALE_LOG2).T.astype(_BF16)
    for h in range(N_KV_HEADS):
        sl = slice(h * HEAD_DIM, (h + 1) * HEAD_DIM)
        kh = qkv_ref[:, o_k + h * HEAD_DIM:o_k + (h + 1) * HEAD_DIM]
        k_ref[0, :, sl] = _rope(_head_rms(kh, kg_ref[...]), cos, sin, first_half).astype(_BF16)
    vt_ref[0] = qkv_ref[:, o_v:o_b].T.astype(_BF16)

    @pl.when(every_step)
    def _():
        cx = cx_ref[...]
        u = cx[:, :CONV_WIDTH] * cx[:, CONV_WIDTH:]
        cw = cw_ref[...]
        conv = cw[0:1] * _shift_rows(u, 1) + cw[1:2] * u + cw[2:3] * _shift_rows(u, -1)
        conv_ref[0] = (gb_ref[...] * conv[HALO:HALO + tile]).astype(_BF16)


def _in_proj(x, mods, gains, w_in, q_gain, k_gain, conv_w, n_keys):
    b, n, d = x.shape
    tile = WIDE_TILE
    rows_per_tile = tile // GRID_W
    assert tile % GRID_W == 0
    prev_spec, main_spec, next_spec = _halo_specs(tile, HALO, n, d)
    rope = _rope_axis_tables(n // GRID_W)
    row_table = pl.BlockSpec((rows_per_tile, HEAD_DIM), lambda bi, i: (i, 0))
    col_table = _resident((GRID_W, HEAD_DIM), lambda bi, i: (0, 0))
    return pl.pallas_call(
        _in_proj_kernel,
        grid=(b, n // tile),
        in_specs=[
            prev_spec, main_spec, next_spec,
            _mod_spec(d, 0, lambda bi, i: bi),
            _layer_spec(gains, 0),
            _layer_spec(w_in, 0),
            _layer_spec(q_gain, 0), _layer_spec(k_gain, 0),
            _layer_spec(conv_w, 0),
            row_table, row_table, col_table, col_table,
        ],
        out_specs=[pl.BlockSpec((1, ATTN_WIDTH, tile), lambda bi, i: (bi, 0, i)),
                   pl.BlockSpec((1, tile, KV_WIDTH), lambda bi, i: (bi, i, 0)),
                   pl.BlockSpec((1, KV_WIDTH, tile), lambda bi, i: (bi, 0, i)),
                   pl.BlockSpec((1, tile, CONV_WIDTH), lambda bi, i: (bi, i, 0))],
        out_shape=[jax.ShapeDtypeStruct((b, ATTN_WIDTH, n), _BF16), jax.ShapeDtypeStruct((b, n_keys, KV_WIDTH), _BF16),
                   jax.ShapeDtypeStruct((b, KV_WIDTH, n_keys), _BF16), jax.ShapeDtypeStruct((b, n, CONV_WIDTH), _BF16)],
        scratch_shapes=[
            pltpu.VMEM((tile + 2 * HALO, d), _BF16),
            pltpu.VMEM((tile, d), _BF16),
            pltpu.VMEM((tile, ATTN_WIDTH + 2 * KV_WIDTH), _F32),
            pltpu.VMEM((tile + 2 * HALO, 2 * CONV_WIDTH), _F32),
            pltpu.VMEM((tile, CONV_WIDTH), _F32),
        ],
        compiler_params=_compiler_params(2),
        name="in_proj",
    )(x, x, x, mods, gains, w_in, q_gain, k_gain, conv_w, rope[0], rope[1], rope[2], rope[3])


def _ctx_kv_kernel(ctx_ref, mod_ref, gain_ref, w_ref, kg_ref, k_in_ref, vt_in_ref, k_ref, vt_ref):
    del k_in_ref, vt_in_ref
    a = _norm_modulate(ctx_ref[0], gain_ref[...], mod_ref[0:1, :], mod_ref[1:2, :]).astype(_BF16)
    kv = _mm(a, w_ref[...])
    for h in range(N_KV_HEADS):
        sl = slice(h * HEAD_DIM, (h + 1) * HEAD_DIM)
        k_ref[0, :, sl] = _head_rms(kv[:, sl], kg_ref[...]).astype(_BF16)
    vt_ref[0] = kv[:, KV_WIDTH:].T.astype(_BF16)


def _ctx_kv(ctx, mods, gains, w_in, k_gain, k_all, vt_all, n_latent):
    b, lc, d = ctx.shape
    assert n_latent % lc == 0 and 2 * KV_WIDTH == ATTN_WIDTH
    ctx_block = n_latent // lc
    return pl.pallas_call(
        _ctx_kv_kernel,
        grid=(b,),
        in_specs=[
            pl.BlockSpec((1, lc, d), lambda i: (i, 0, 0)),
            _mod_spec(d, 0, lambda i: b),
            _layer_spec(gains, 0),
            _resident((None, d, 2 * KV_WIDTH), lambda i: (0, 0, 1)),
            _layer_spec(k_gain, 0),
            pl.BlockSpec(memory_space=pl.ANY), pl.BlockSpec(memory_space=pl.ANY),
        ],
        out_specs=[pl.BlockSpec((1, lc, KV_WIDTH), lambda i: (i, ctx_block, 0)),
                   pl.BlockSpec((1, KV_WIDTH, lc), lambda i: (i, 0, ctx_block))],
        out_shape=[jax.ShapeDtypeStruct(k_all.shape, _BF16), jax.ShapeDtypeStruct(vt_all.shape, _BF16)],
        input_output_aliases={5: 0, 6: 1},
        compiler_params=_compiler_params(1),
        name="ctx_kv",
    )(ctx, mods, gains, w_in, k_gain, k_all, vt_all)


def _attn_kernel(qt_ref, k_ref, vt_ref, conv_ref, x_ref, mod_ref, wo_ref, wup_f32_ref, wdown_f32_ref,
                 o_ref, wup_bf16_ref, wdown_bf16_ref, s_ref, cmax_ref, m_ref, acc_ref, attn_ref):
    n_chunks = k_ref.shape[1] // KV_TILE

    def keys_of(j):
        if isinstance(j, int):
            return slice(j * KV_TILE, (j + 1) * KV_TILE)
        return pl.ds(pl.multiple_of(j * KV_TILE, KV_TILE), KV_TILE)

    def score_stage(j, head):
        g = head // Q_PER_KV
        kch = k_ref[0, keys_of(j), g * HEAD_DIM:(g + 1) * HEAD_DIM]
        qt = qt_ref[0, head * HEAD_DIM:(head + 1) * HEAD_DIM, :]
        s = jnp.dot(kch, qt, preferred_element_type=_F32)
        s_ref[head % 2] = s
        cmax_ref[head % 2] = jnp.max(s, axis=0, keepdims=True)

    def value_stage(j, head):
        g = head // Q_PER_KV
        vtch = vt_ref[0, g * HEAD_DIM:(g + 1) * HEAD_DIM, keys_of(j)]
        v_aug = jnp.concatenate([vtch, jnp.ones((ONES_ROWS, KV_TILE), _BF16)], axis=0)
        m_prev = m_ref[head]
        m_new = jnp.maximum(m_prev, cmax_ref[head % 2])
        alpha = jnp.exp2(m_prev - m_new)
        p = jnp.exp2(s_ref[head % 2] - m_new).astype(_BF16)
        acc_ref[head] = alpha * acc_ref[head] + jnp.dot(v_aug, p, preferred_element_type=_F32)
        m_ref[head] = m_new

    def chunk(j, last):
        for head in range(N_Q_HEADS):
            if head + 1 < N_Q_HEADS:
                score_stage(j, head + 1)
            elif not last:
                score_stage(j + 1, 0)
            value_stage(j, head)

    m_ref[...] = jnp.full(m_ref.shape, -jnp.inf, _F32)
    acc_ref[...] = jnp.zeros(acc_ref.shape, _F32)
    score_stage(0, 0)

    def body(j, carry):
        chunk(j, last=False)
        return carry

    lax.fori_loop(0, n_chunks - 1, body, 0, unroll=5)
    chunk(n_chunks - 1, last=True)
    for head in range(N_Q_HEADS):
        acc = acc_ref[head]
        out_t = acc[:HEAD_DIM] / acc[HEAD_DIM:HEAD_DIM + 1]
        attn_ref[:, head * HEAD_DIM:(head + 1) * HEAD_DIM] = out_t.T.astype(_BF16)

    mixed = jnp.concatenate([attn_ref[...], conv_ref[0]], axis=-1)
    y = _mm(mixed, wo_ref[...])
    o_ref[0] = x_ref[0] + mod_ref[2:3, :] * y
    wup_bf16_ref[...] = wup_f32_ref[...].astype(_BF16)
    wdown_bf16_ref[...] = wdown_f32_ref[...].astype(_BF16)


def _attn_out(qt, k_all, vt_all, conv, x, mods, w_out, w_up, w_down, ffn_layer):
    b, n, d = x.shape
    n_keys = k_all.shape[1]
    assert n_keys % KV_TILE == 0
    tq = Q_TILE
    grid = (b, n // tq)
    per_batch = lambda rows, width: pl.BlockSpec((1, rows, width), lambda bi, i: (bi, 0, 0))
    tile = lambda width: pl.BlockSpec((1, tq, width), lambda bi, i: (bi, i, 0))
    up_in, up_out, up_shape = _cast_plan(w_up, ffn_layer, grid)
    down_in, down_out, down_shape = _cast_plan(w_down, ffn_layer, grid)
    return pl.pallas_call(
        _attn_kernel,
        grid=grid,
        in_specs=[
            pl.BlockSpec((1, ATTN_WIDTH, tq), lambda bi, i: (bi, 0, i)),
            per_batch(n_keys, KV_WIDTH), per_batch(KV_WIDTH, n_keys),
            tile(CONV_WIDTH),
            tile(d),
            _mod_spec(d, 0, lambda bi, i: bi),
            _layer_spec(w_out, 0),
            up_in, down_in,
        ],
        out_specs=[tile(d), up_out, down_out],
        out_shape=[jax.ShapeDtypeStruct((b, n, d), _F32), up_shape, down_shape],
        scratch_shapes=[
            pltpu.VMEM((2, KV_TILE, tq), _F32),
            pltpu.VMEM((2, 1, tq), _F32),
            pltpu.VMEM((N_Q_HEADS, 1, tq), _F32),
            pltpu.VMEM((N_Q_HEADS, HEAD_DIM + ONES_ROWS, tq), _F32),
            pltpu.VMEM((tq, ATTN_WIDTH), _BF16),
        ],
        compiler_params=_compiler_params(2),
        name="attn_out",
    )(qt, k_all, vt_all, conv, x, mods, w_out, w_up, w_down)


def _ffn_kernel(xp_ref, x_ref, xn_ref, mod_ref, gain_ref, wup_ref, cwb_ref, wd_ref, *rest):
    if len(rest) == 2:
        o_ref, h_ref = rest
    else:
        wup_f32_ref, wdown_f32_ref, o_ref, wup_bf16_ref, wdown_bf16_ref, h_ref = rest
        wup_bf16_ref[...] = wup_f32_ref[...].astype(_BF16)
        wdown_bf16_ref[...] = wdown_f32_ref[...].astype(_BF16)
    tile = x_ref.shape[1]
    gain, shift, scale = gain_ref[...], mod_ref[3:4, :], mod_ref[4:5, :]
    x = x_ref[0]
    prev_ok, next_ok = _edge_flags()
    f_main = _norm_modulate(x, gain, shift, scale)
    f_prev = _norm_modulate(xp_ref[0], gain, shift, scale) * prev_ok
    f_next = _norm_modulate(xn_ref[0], gain, shift, scale) * next_ok
    f_ext = jnp.concatenate([f_prev, f_main, f_next], axis=0).astype(_BF16)
    f_mb = f_main.astype(_BF16)
    n_chunks = D_FF // FF_CHUNK

    def up(c):
        cols = slice(c * FF_CHUNK, (c + 1) * FF_CHUNK)
        val_cols = slice(D_FF + c * FF_CHUNK, D_FF + (c + 1) * FF_CHUNK)
        return _mm(f_ext, wup_ref[:, cols]), _mm(f_mb, wup_ref[:, val_cols])

    def act(c, gate, val):
        cols = slice(c * FF_CHUNK, (c + 1) * FF_CHUNK)
        cwb = cwb_ref[:, cols]
        conv = cwb[0:1] * _shift_rows(gate, 1) + cwb[1:2] * gate + cwb[2:3] * _shift_rows(gate, -1)
        h_ref[:, cols] = (_silu(conv[HALO:HALO + tile] + cwb[3:4]) * val).astype(_BF16)

    nxt = up(0)
    for c in range(n_chunks):
        cur = nxt
        if c + 1 < n_chunks:
            nxt = up(c + 1)
        act(c, *cur)
    y = _mm(h_ref[...], wd_ref[...])
    o_ref[0] = x + mod_ref[5:6, :] * y


def _ffn(x, mods, gains, w_up_bf16, conv_wb, w_down_bf16, layer, next_weights=None):
    b, n, d = x.shape
    tile = TOKEN_TILE
    grid = (b, n // tile)
    prev_spec, main_spec, next_spec = _halo_specs(tile, HALO, n, d)
    whole = lambda a: _resident(a.shape, lambda bi, i: (0, 0))
    in_specs = [
        prev_spec, main_spec, next_spec,
        _mod_spec(d, layer, lambda bi, i: bi),
        _layer_spec(gains, layer),
        whole(w_up_bf16), _layer_spec(conv_wb, layer), whole(w_down_bf16),
    ]
    args = [x, x, x, mods, gains, w_up_bf16, conv_wb, w_down_bf16]
    out_specs, out_shape = [main_spec], [jax.ShapeDtypeStruct((b, n, d), _F32)]
    if next_weights is not None:
        for w in next_weights:
            cast_in, cast_out, cast_shape = _cast_plan(w, layer + 1, grid)
            in_specs.append(cast_in)
            out_specs.append(cast_out)
            out_shape.append(cast_shape)
        args += list(next_weights)
    return pl.pallas_call(
        _ffn_kernel,
        grid=grid,
        in_specs=in_specs,
        out_specs=out_specs,
        out_shape=out_shape,
        scratch_shapes=[pltpu.VMEM((tile, D_FF), _BF16)],
        compiler_params=_compiler_params(2),
        name="ffn",
    )(*args)


def _pool_kernel(xp_ref, x_ref, xn_ref, mod_ref, gain_ref, pw_ref, ps_ref, o_ref):
    tile = x_ref.shape[1]
    halo = xp_ref.shape[1]
    n_total = tile * pl.num_programs(1)
    gain, shift, scale = gain_ref[...], mod_ref[0:1, :], mod_ref[1:2, :]
    x = x_ref[0]
    prev_ok, next_ok = _edge_flags()
    a_prev = _norm_modulate(xp_ref[0], gain, shift, scale) * prev_ok
    a_main = _norm_modulate(x, gain, shift, scale)
    a_next = _norm_modulate(xn_ref[0], gain, shift, scale) * next_ok
    a_ext = jnp.concatenate([a_prev, a_main, a_next], axis=0)
    t = pl.program_id(1) * tile + lax.broadcasted_iota(jnp.int32, (tile, 1), 0)

    mixed = []
    for gi, w in enumerate(POOL_WINDOWS):
        a = a_ext[:, gi * POOL_GROUP:(gi + 1) * POOL_GROUP]
        s = a
        k = 1
        while 2 * k < w:
            s = s + _shift_rows(s, -k)
            k *= 2
        s = _shift_rows(s, k) + s
        count = jnp.minimum(t + w - w // 2, n_total) - jnp.maximum(t - w // 2, 0)
        pooled = s[halo:halo + tile] * (1.0 / count.astype(_F32)) - a[halo:halo + tile]
        mixed.append(_mm(pooled.astype(_BF16), pw_ref[gi]))
    o_ref[0] = x + jnp.concatenate(mixed, axis=-1) * (ps_ref[...] * mod_ref[2:3, :])


def _pool(x, mods, gains, pool_w, pool_scale, layer):
    b, n, d = x.shape
    tile = WIDE_TILE
    prev_spec, main_spec, next_spec = _halo_specs(tile, POOL_HALO, n, d)
    return pl.pallas_call(
        _pool_kernel,
        grid=(b, n // tile),
        in_specs=[
            prev_spec, main_spec, next_spec,
            _mod_spec(d, layer, lambda bi, i: bi),
            _layer_spec(gains, layer),
            _layer_spec(pool_w, layer // 2),
            _layer_spec(pool_scale, layer // 2),
        ],
        out_specs=main_spec,
        out_shape=jax.ShapeDtypeStruct((b, n, d), _F32),
        compiler_params=_compiler_params(2),
        name="pool",
    )(x, x, x, mods, gains, pool_w, pool_scale)


def kernel(x, c, ctx, c_ctx, ada_w, ada_b, mix_norm, ffn_norm, even_w_in, even_q_gain, even_k_gain, even_conv_w,
           even_w_out, odd_pool_w, odd_pool_scale, ffn_w_up, ffn_conv_w, ffn_conv_b, ffn_w_down):
    b, n, d = x.shape
    depth = ada_w.shape[0]
    lc = ctx.shape[1]
    assert depth == 2 and d == D_MODEL and b + 1 == N_COND
    assert n % TOKEN_TILE == 0 and n % Q_TILE == 0 and n % WIDE_TILE == 0

    cond = jnp.concatenate([c, c_ctx[None, :], jnp.zeros((SUBLANES - b - 1, d), _F32)], axis=0)
    mods = _adaln(cond.T, ada_w, ada_b).reshape(depth, SUBLANES, N_MOD, d)

    rows = lambda v: v.reshape(v.shape[0], 1, v.shape[-1])
    mix_gain, ffn_gain = rows(mix_norm), rows(ffn_norm)
    w_in, w_out, pool_w, w_up, w_down = even_w_in, even_w_out, odd_pool_w, ffn_w_up, ffn_w_down
    conv_wb = jnp.concatenate([ffn_conv_w, ffn_conv_b[:, None, :],
                               jnp.zeros((depth, SUBLANES - ffn_conv_w.shape[1] - 1, D_FF), _F32)], axis=1)

    qt, k_all, vt_all, conv = _in_proj(x, mods, mix_gain, w_in, rows(even_q_gain), rows(even_k_gain),
                                       even_conv_w, n + lc)
    k_all, vt_all = _ctx_kv(ctx, mods, mix_gain, w_in, rows(even_k_gain), k_all, vt_all, n)
    x, w_up0, w_down0 = _attn_out(qt, k_all, vt_all, conv, x, mods, w_out, w_up, w_down, 0)
    x, w_up1, w_down1 = _ffn(x, mods, ffn_gain, w_up0, conv_wb, w_down0, 0, next_weights=(w_up, w_down))

    x = _pool(x, mods, mix_gain, pool_w, rows(odd_pool_scale), 1)
    (x,) = _ffn(x, mods, ffn_gain, w_up1, conv_wb, w_down1, 1)
    return x
```

```python
import functools

import jax
import jax.numpy as jnp
import numpy as np
from jax import lax
from jax.experimental import pallas as pl
from jax.experimental.pallas import tpu as pltpu

D_MODEL = 1024
GRID_W = 64
HEAD_DIM = 128
N_Q_HEADS = 4
N_KV_HEADS = 2
Q_PER_KV = N_Q_HEADS // N_KV_HEADS
ATTN_WIDTH = N_Q_HEADS * HEAD_DIM
KV_WIDTH = N_KV_HEADS * HEAD_DIM
CONV_WIDTH = D_MODEL - ATTN_WIDTH
IN_PROJ_WIDTH = ATTN_WIDTH + 2 * KV_WIDTH + 3 * CONV_WIDTH
ROPE_THETA = 10000.0
ROPE_FREQS = HEAD_DIM // 4
SCORE_SCALE_LOG2 = HEAD_DIM ** -0.5 * 1.4426950408889634
POOL_WINDOWS = (2, 4, 8, 16)
POOL_GROUP = D_MODEL // len(POOL_WINDOWS)
D_FF = 2816
N_MOD = 6
N_COND = 3
EVEN_LAYER, ODD_LAYER = 0, 1
EPS = 1e-6

SUBLANES = 8
VMEM_LIMIT_BYTES = 56 * 1024 * 1024

HALO = SUBLANES
POOL_HALO = 2 * SUBLANES
TOKEN_TILE = 1024
WIDE_TILE = 1024
Q_TILE = 512
KV_TILE = 768
ONES_ROWS = 16
FF_CHUNK = 256
ADALN_COLS = 1536

_BF16 = jnp.bfloat16
_F32 = jnp.float32


def _compiler_params(n_axes):
    return pltpu.CompilerParams(dimension_semantics=("arbitrary",) * n_axes,
                                vmem_limit_bytes=VMEM_LIMIT_BYTES)


def _resident(shape, index_map):
    return pl.BlockSpec(shape, index_map, pipeline_mode=pl.Buffered(1))


def _layer_spec(arr, layer):
    if arr.ndim == 2:
        return _resident(arr.shape, lambda *_: (0, 0))
    zeros = (0,) * (arr.ndim - 1)
    return _resident((None,) + arr.shape[1:], lambda *_: (layer,) + zeros)


def _layer_row(ref, layer):
    return ref[layer:layer + 1, :]


def _mod_spec(d, layer, row_of):
    return pl.BlockSpec((None, None, N_MOD, d), lambda *ids: (layer, row_of(*ids), 0, 0))


def _cast_plan(w, layer, grid):
    rows, cols = w.shape[1:]
    n_steps = grid[0] * grid[1]
    block, shared = rows // n_steps, 1
    while block % (2 * SUBLANES):
        block, shared = 2 * block, 2 * shared
    assert rows % block == 0 and n_steps * block == rows * shared
    block_of = lambda b, i: (b * grid[1] + i) // shared
    return (pl.BlockSpec((None, block, cols), lambda b, i: (layer, block_of(b, i), 0)),
            pl.BlockSpec((block, cols), lambda b, i: (block_of(b, i), 0)),
            jax.ShapeDtypeStruct((rows, cols), _BF16))


def _norm_modulate(x, gain, shift, scale):
    y = x * lax.rsqrt(jnp.mean(x * x, axis=-1, keepdims=True) + EPS)
    return y * (gain * (1.0 + scale)) + shift


def _mm(a, w):
    return lax.dot_general(a, w, (((1,), (0,)), ((), ())), preferred_element_type=_F32)


def _silu(x):
    return x * (1.0 / (1.0 + jnp.exp(-x)))


def _shift_rows(u, k):
    return pltpu.roll(u, k % u.shape[0], axis=0)


def _adaln_kernel(ct_ref, w_ref, b_ref, o_ref):
    s = _silu(ct_ref[...])
    w = w_ref[0]
    rows = [jnp.sum(s[:, r:r + 1] * w, axis=0, keepdims=True) + b_ref[0] for r in range(N_COND)]
    rows.append(jnp.zeros((SUBLANES - N_COND, w.shape[1]), _F32))
    o_ref[0] = jnp.concatenate(rows, axis=0)


def _adaln(cond_t, ada_w, ada_b):
    depth, d, width = ada_w.shape
    return pl.pallas_call(
        _adaln_kernel,
        grid=(depth, width // ADALN_COLS),
        in_specs=[
            pl.BlockSpec((d, SUBLANES), lambda l, j: (0, 0)),
            pl.BlockSpec((1, d, ADALN_COLS), lambda l, j: (l, 0, j)),
            pl.BlockSpec((1, 1, ADALN_COLS), lambda l, j: (l, 0, j)),
        ],
        out_specs=pl.BlockSpec((1, SUBLANES, ADALN_COLS), lambda l, j: (l, 0, j)),
        out_shape=jax.ShapeDtypeStruct((depth, SUBLANES, width), _F32),
        compiler_params=_compiler_params(2),
        name="adaln",
    )(cond_t, ada_w, ada_b.reshape(depth, 1, width))


def _head_rms(xh, gain):
    return xh * lax.rsqrt(jnp.mean(xh * xh, axis=-1, keepdims=True) + EPS) * gain


def _rope(xh, cos, sin_signed, first_half):
    partner = jnp.where(first_half, pltpu.roll(xh, HEAD_DIM - ROPE_FREQS, axis=1), pltpu.roll(xh, ROPE_FREQS, axis=1))
    return xh * cos + partner * sin_signed


def _rope_axis_tables(n_rows):
    axis_dim = HEAD_DIM // 2
    f32 = np.float32
    inv_freq = np.power(f32(ROPE_THETA), -np.arange(0, axis_dim, 2, dtype=f32) / f32(axis_dim)).astype(f32)
    ang_r = np.arange(n_rows, dtype=f32)[:, None] * inv_freq
    ang_c = np.arange(GRID_W, dtype=f32)[:, None] * inv_freq
    zr, zc = np.zeros((n_rows, axis_dim), f32), np.zeros((GRID_W, axis_dim), f32)
    return (np.concatenate([np.cos(ang_r), np.cos(ang_r), zr], axis=1),
            np.concatenate([-np.sin(ang_r), np.sin(ang_r), zr], axis=1),
            np.concatenate([zc, np.cos(ang_c), np.cos(ang_c)], axis=1),
            np.concatenate([zc, -np.sin(ang_c), np.sin(ang_c)], axis=1))


def _tile_table(row_ref, col_ref):
    col = col_ref[...]
    return jnp.concatenate([jnp.broadcast_to(row_ref[r:r + 1, :], col.shape) + col for r in range(row_ref.shape[0])],
                           axis=0)


def _halo_specs(tile, halo, n, d):
    per_tile = tile // halo
    n_halo_blocks = n // halo
    prev_spec = pl.BlockSpec((1, halo, d), lambda b, i: (b, jnp.maximum(i * per_tile - 1, 0), 0))
    main_spec = pl.BlockSpec((1, tile, d), lambda b, i: (b, i, 0))
    next_spec = pl.BlockSpec((1, halo, d), lambda b, i: (b, jnp.minimum((i + 1) * per_tile, n_halo_blocks - 1), 0))
    return prev_spec, main_spec, next_spec


def _edge_flags():
    i = pl.program_id(1)
    return (i > 0).astype(_F32), (i < pl.num_programs(1) - 1).astype(_F32)


def _in_proj_kernel(xp_ref, x_ref, xn_ref, mod_ref, gain_ref, w_ref, qg_ref, kg_ref, cw_ref,
                    rcos_ref, rsin_ref, ccos_ref, csin_ref, qt_ref, k_ref, vt_ref, conv_ref,
                    aext_ref, amain_ref, qkv_ref, cx_ref, gb_ref):
    tile = x_ref.shape[1]
    o_k = ATTN_WIDTH
    o_v = o_k + KV_WIDTH
    o_b = o_v + KV_WIDTH
    o_c = o_b + CONV_WIDTH
    every_step = pl.program_id(1) < pl.num_programs(1)

    @pl.when(every_step)
    def _():
        gain, shift, scale = _layer_row(gain_ref, EVEN_LAYER), mod_ref[0:1, :], mod_ref[1:2, :]
        prev_ok, next_ok = _edge_flags()
        a_main = _norm_modulate(x_ref[0], gain, shift, scale)
        a_prev = _norm_modulate(xp_ref[0], gain, shift, scale) * prev_ok
        a_next = _norm_modulate(xn_ref[0], gain, shift, scale) * next_ok
        aext_ref[...] = jnp.concatenate([a_prev, a_main, a_next], axis=0).astype(_BF16)
        a_mb = a_main.astype(_BF16)
        amain_ref[...] = a_mb
        qkv_ref[...] = _mm(a_mb, w_ref[:, 0:o_b])

    cx_ref[...] = _mm(aext_ref[...], w_ref[:, o_c:])
    gb_ref[...] = _mm(amain_ref[...], w_ref[:, o_b:o_c])
    cos, sin = _tile_table(rcos_ref, ccos_ref), _tile_table(rsin_ref, csin_ref)
    lane = lax.broadcasted_iota(jnp.int32, (1, HEAD_DIM), 1)
    first_half = (lane % (2 * ROPE_FREQS)) < ROPE_FREQS
    for h in range(N_Q_HEADS):
        sl = slice(h * HEAD_DIM, (h + 1) * HEAD_DIM)
        qh = _rope(_head_rms(qkv_ref[:, sl], qg_ref[...]), cos, sin, first_half)
        qt_ref[0, sl, :] = (qh * SCORE_SCALE_LOG2).T.astype(_BF16)
    for h in range(N_KV_HEADS):
        sl = slice(h * HEAD_DIM, (h + 1) * HEAD_DIM)
        kh = qkv_ref[:, o_k + h * HEAD_DIM:o_k + (h + 1) * HEAD_DIM]
        k_ref[0, :, sl] = _rope(_head_rms(kh, kg_ref[...]), cos, sin, first_half).astype(_BF16)
    vt_ref[0] = qkv_ref[:, o_v:o_b].T.astype(_BF16)

    @pl.when(every_step)
    def _():
        cx = cx_ref[...]
        u = cx[:, :CONV_WIDTH] * cx[:, CONV_WIDTH:]
        cw = cw_ref[...]
        conv = cw[0:1] * _shift_rows(u, 1) + cw[1:2] * u + cw[2:3] * _shift_rows(u, -1)
        conv_ref[0] = (gb_ref[...] * conv[HALO:HALO + tile]).astype(_BF16)


def _in_proj(x, mods, gains, w_in, q_gain, k_gain, conv_w, n_keys):
    b, n, d = x.shape
    tile = WIDE_TILE
    rows_per_tile = tile // GRID_W
    assert tile % GRID_W == 0
    prev_spec, main_spec, next_spec = _halo_specs(tile, HALO, n, d)
    rope = _rope_axis_tables(n // GRID_W)
    row_table = pl.BlockSpec((rows_per_tile, HEAD_DIM), lambda bi, i: (i, 0))
    col_table = _resident((GRID_W, HEAD_DIM), lambda bi, i: (0, 0))
    return pl.pallas_call(
        _in_proj_kernel,
        grid=(b, n // tile),
        in_specs=[
            prev_spec, main_spec, next_spec,
            _mod_spec(d, 0, lambda bi, i: bi),
            _layer_spec(gains, 0),
            _layer_spec(w_in, 0),
            _layer_spec(q_gain, 0), _layer_spec(k_gain, 0),
            _layer_spec(conv_w, 0),
            row_table, row_table, col_table, col_table,
        ],
        out_specs=[pl.BlockSpec((1, ATTN_WIDTH, tile), lambda bi, i: (bi, 0, i)),
                   pl.BlockSpec((1, tile, KV_WIDTH), lambda bi, i: (bi, i, 0)),
                   pl.BlockSpec((1, KV_WIDTH, tile), lambda bi, i: (bi, 0, i)),
                   pl.BlockSpec((1, tile, CONV_WIDTH), lambda bi, i: (bi, i, 0))],
        out_shape=[jax.ShapeDtypeStruct((b, ATTN_WIDTH, n), _BF16), jax.ShapeDtypeStruct((b, n_keys, KV_WIDTH), _BF16),
                   jax.ShapeDtypeStruct((b, KV_WIDTH, n_keys), _BF16), jax.ShapeDtypeStruct((b, n, CONV_WIDTH), _BF16)],
        scratch_shapes=[
            pltpu.VMEM((tile + 2 * HALO, d), _BF16),
            pltpu.VMEM((tile, d), _BF16),
            pltpu.VMEM((tile, ATTN_WIDTH + 2 * KV_WIDTH), _F32),
            pltpu.VMEM((tile + 2 * HALO, 2 * CONV_WIDTH), _F32),
            pltpu.VMEM((tile, CONV_WIDTH), _F32),
        ],
        compiler_params=_compiler_params(2),
        name="in_proj",
    )(x, x, x, mods, gains, w_in, q_gain, k_gain, conv_w, rope[0], rope[1], rope[2], rope[3])


def _ctx_kv_kernel(ctx_ref, mod_ref, gain_ref, w_ref, kg_ref, k_in_ref, vt_in_ref, k_ref, vt_ref):
    del k_in_ref, vt_in_ref
    a = _norm_modulate(ctx_ref[0], _layer_row(gain_ref, EVEN_LAYER), mod_ref[0:1, :], mod_ref[1:2, :]).astype(_BF16)
    kv = _mm(a, w_ref[...])
    for h in range(N_KV_HEADS):
        sl = slice(h * HEAD_DIM, (h + 1) * HEAD_DIM)
        k_ref[0, :, sl] = _head_rms(kv[:, sl], kg_ref[...]).astype(_BF16)
    vt_ref[0] = kv[:, KV_WIDTH:].T.astype(_BF16)


def _ctx_kv(ctx, mods, gains, w_in, k_gain, k_all, vt_all, n_latent):
    b, lc, d = ctx.shape
    assert n_latent % lc == 0 and 2 * KV_WIDTH == ATTN_WIDTH
    ctx_block = n_latent // lc
    return pl.pallas_call(
        _ctx_kv_kernel,
        grid=(b,),
        in_specs=[
            pl.BlockSpec((1, lc, d), lambda i: (i, 0, 0)),
            _mod_spec(d, 0, lambda i: b),
            _layer_spec(gains, 0),
            _resident((None, d, 2 * KV_WIDTH), lambda i: (0, 0, 1)),
            _layer_spec(k_gain, 0),
            pl.BlockSpec(memory_space=pl.ANY), pl.BlockSpec(memory_space=pl.ANY),
        ],
        out_specs=[pl.BlockSpec((1, lc, KV_WIDTH), lambda i: (i, ctx_block, 0)),
                   pl.BlockSpec((1, KV_WIDTH, lc), lambda i: (i, 0, ctx_block))],
        out_shape=[jax.ShapeDtypeStruct(k_all.shape, _BF16), jax.ShapeDtypeStruct(vt_all.shape, _BF16)],
        input_output_aliases={5: 0, 6: 1},
        compiler_params=_compiler_params(1),
        name="ctx_kv",
    )(ctx, mods, gains, w_in, k_gain, k_all, vt_all)


def _attn_kernel(qt_ref, k_ref, vt_ref, conv_ref, x_ref, mod_ref, wo_ref, wup_f32_ref, wdown_f32_ref,
                 o_ref, wup_bf16_ref, wdown_bf16_ref, s_ref, cmax_ref, m_ref, acc_ref, attn_ref):
    n_chunks = k_ref.shape[1] // KV_TILE

    def keys_of(j):
        if isinstance(j, int):
            return slice(j * KV_TILE, (j + 1) * KV_TILE)
        return pl.ds(pl.multiple_of(j * KV_TILE, KV_TILE), KV_TILE)

    def score_stage(j, head):
        g = head // Q_PER_KV
        kch = k_ref[0, keys_of(j), g * HEAD_DIM:(g + 1) * HEAD_DIM]
        qt = qt_ref[0, head * HEAD_DIM:(head + 1) * HEAD_DIM, :]
        s = jnp.dot(kch, qt, preferred_element_type=_F32)
        s_ref[head % 2] = s
        cmax_ref[head % 2] = jnp.max(s, axis=0, keepdims=True)

    def value_stage(j, head):
        g = head // Q_PER_KV
        vtch = vt_ref[0, g * HEAD_DIM:(g + 1) * HEAD_DIM, keys_of(j)]
        v_aug = jnp.concatenate([vtch, jnp.ones((ONES_ROWS, KV_TILE), _BF16)], axis=0)
        m_prev = m_ref[head]
        m_new = jnp.maximum(m_prev, cmax_ref[head % 2])
        alpha = jnp.exp2(m_prev - m_new)
        p = jnp.exp2(s_ref[head % 2] - m_new).astype(_BF16)
        acc_ref[head] = alpha * acc_ref[head] + jnp.dot(v_aug, p, preferred_element_type=_F32)
        m_ref[head] = m_new

    def chunk(j, last):
        for head in range(N_Q_HEADS):
            if head + 1 < N_Q_HEADS:
                score_stage(j, head + 1)
            elif not last:
                score_stage(j + 1, 0)
            value_stage(j, head)

    m_ref[...] = jnp.full(m_ref.shape, -jnp.inf, _F32)
    acc_ref[...] = jnp.zeros(acc_ref.shape, _F32)
    score_stage(0, 0)

    def body(j, carry):
        chunk(j, last=False)
        return carry

    lax.fori_loop(0, n_chunks - 1, body, 0, unroll=5)
    chunk(n_chunks - 1, last=True)
    for head in range(N_Q_HEADS):
        acc = acc_ref[head]
        out_t = acc[:HEAD_DIM] / acc[HEAD_DIM:HEAD_DIM + 1]
        attn_ref[:, head * HEAD_DIM:(head + 1) * HEAD_DIM] = out_t.T.astype(_BF16)

    mixed = jnp.concatenate([attn_ref[...], conv_ref[0]], axis=-1)
    y = _mm(mixed, wo_ref[...])
    o_ref[0] = x_ref[0] + mod_ref[2:3, :] * y
    wup_bf16_ref[...] = wup_f32_ref[...].astype(_BF16)
    wdown_bf16_ref[...] = wdown_f32_ref[...].astype(_BF16)


def _attn_out(qt, k_all, vt_all, conv, x, mods, w_out, w_up, w_down, ffn_layer):
    b, n, d = x.shape
    n_keys = k_all.shape[1]
    assert n_keys % KV_TILE == 0
    tq = Q_TILE
    grid = (b, n // tq)
    per_batch = lambda rows, width: pl.BlockSpec((1, rows, width), lambda bi, i: (bi, 0, 0))
    tile = lambda width: pl.BlockSpec((1, tq, width), lambda bi, i: (bi, i, 0))
    up_in, up_out, up_shape = _cast_plan(w_up, ffn_layer, grid)
    down_in, down_out, down_shape = _cast_plan(w_down, ffn_layer, grid)
    return pl.pallas_call(
        _attn_kernel,
        grid=grid,
        in_specs=[
            pl.BlockSpec((1, ATTN_WIDTH, tq), lambda bi, i: (bi, 0, i)),
            per_batch(n_keys, KV_WIDTH), per_batch(KV_WIDTH, n_keys),
            tile(CONV_WIDTH),
            tile(d),
            _mod_spec(d, 0, lambda bi, i: bi),
            _layer_spec(w_out, 0),
            up_in, down_in,
        ],
        out_specs=[tile(d), up_out, down_out],
        out_shape=[jax.ShapeDtypeStruct((b, n, d), _F32), up_shape, down_shape],
        scratch_shapes=[
            pltpu.VMEM((2, KV_TILE, tq), _F32),
            pltpu.VMEM((2, 1, tq), _F32),
            pltpu.VMEM((N_Q_HEADS, 1, tq), _F32),
            pltpu.VMEM((N_Q_HEADS, HEAD_DIM + ONES_ROWS, tq), _F32),
            pltpu.VMEM((tq, ATTN_WIDTH), _BF16),
        ],
        compiler_params=_compiler_params(2),
        name="attn_out",
    )(qt, k_all, vt_all, conv, x, mods, w_out, w_up, w_down)


def _ffn_kernel(xp_ref, x_ref, xn_ref, mod_ref, gain_ref, wup_ref, cwb_ref, wd_ref, *rest, layer):
    if len(rest) == 2:
        o_ref, h_ref = rest
    else:
        wup_f32_ref, wdown_f32_ref, o_ref, wup_bf16_ref, wdown_bf16_ref, h_ref = rest
        wup_bf16_ref[...] = wup_f32_ref[...].astype(_BF16)
        wdown_bf16_ref[...] = wdown_f32_ref[...].astype(_BF16)
    tile = x_ref.shape[1]
    gain, shift, scale = _layer_row(gain_ref, layer), mod_ref[3:4, :], mod_ref[4:5, :]
    x = x_ref[0]
    prev_ok, next_ok = _edge_flags()
    f_main = _norm_modulate(x, gain, shift, scale)
    f_prev = _norm_modulate(xp_ref[0], gain, shift, scale) * prev_ok
    f_next = _norm_modulate(xn_ref[0], gain, shift, scale) * next_ok
    f_ext = jnp.concatenate([f_prev, f_main, f_next], axis=0).astype(_BF16)
    f_mb = f_main.astype(_BF16)
    n_chunks = D_FF // FF_CHUNK

    def up(c):
        cols = slice(c * FF_CHUNK, (c + 1) * FF_CHUNK)
        val_cols = slice(D_FF + c * FF_CHUNK, D_FF + (c + 1) * FF_CHUNK)
        return _mm(f_ext, wup_ref[:, cols]), _mm(f_mb, wup_ref[:, val_cols])

    def act(c, gate, val):
        cols = slice(c * FF_CHUNK, (c + 1) * FF_CHUNK)
        cwb = cwb_ref[:, cols]
        conv = cwb[0:1] * _shift_rows(gate, 1) + cwb[1:2] * gate + cwb[2:3] * _shift_rows(gate, -1)
        h_ref[:, cols] = (_silu(conv[HALO:HALO + tile] + cwb[3:4]) * val).astype(_BF16)

    nxt = up(0)
    for c in range(n_chunks):
        cur = nxt
        if c + 1 < n_chunks:
            nxt = up(c + 1)
        act(c, *cur)
    y = _mm(h_ref[...], wd_ref[...])
    o_ref[0] = x + mod_ref[5:6, :] * y


def _ffn(x, mods, gains, w_up_bf16, conv_wb, w_down_bf16, layer, next_weights=None):
    b, n, d = x.shape
    tile = TOKEN_TILE
    grid = (b, n // tile)
    prev_spec, main_spec, next_spec = _halo_specs(tile, HALO, n, d)
    whole = lambda a: _resident(a.shape, lambda bi, i: (0, 0))
    in_specs = [
        prev_spec, main_spec, next_spec,
        _mod_spec(d, layer, lambda bi, i: bi),
        _layer_spec(gains, layer),
        whole(w_up_bf16), _layer_spec(conv_wb, layer), whole(w_down_bf16),
    ]
    args = [x, x, x, mods, gains, w_up_bf16, conv_wb, w_down_bf16]
    out_specs, out_shape = [main_spec], [jax.ShapeDtypeStruct((b, n, d), _F32)]
    if next_weights is not None:
        for w in next_weights:
            cast_in, cast_out, cast_shape = _cast_plan(w, layer + 1, grid)
            in_specs.append(cast_in)
            out_specs.append(cast_out)
            out_shape.append(cast_shape)
        args += list(next_weights)
    return pl.pallas_call(
        functools.partial(_ffn_kernel, layer=layer),
        grid=grid,
        in_specs=in_specs,
        out_specs=out_specs,
        out_shape=out_shape,
        scratch_shapes=[pltpu.VMEM((tile, D_FF), _BF16)],
        compiler_params=_compiler_params(2),
        name="ffn",
    )(*args)


def _pool_kernel(xp_ref, x_ref, xn_ref, mod_ref, gain_ref, pw_ref, ps_ref, o_ref, *, layer):
    tile = x_ref.shape[1]
    halo = xp_ref.shape[1]
    n_total = tile * pl.num_programs(1)
    gain, shift, scale = _layer_row(gain_ref, layer), mod_ref[0:1, :], mod_ref[1:2, :]
    x = x_ref[0]
    prev_ok, next_ok = _edge_flags()
    a_prev = _norm_modulate(xp_ref[0], gain, shift, scale) * prev_ok
    a_main = _norm_modulate(x, gain, shift, scale)
    a_next = _norm_modulate(xn_ref[0], gain, shift, scale) * next_ok
    a_ext = jnp.concatenate([a_prev, a_main, a_next], axis=0)
    t = pl.program_id(1) * tile + lax.broadcasted_iota(jnp.int32, (tile, 1), 0)

    mixed = []
    for gi, w in enumerate(POOL_WINDOWS):
        a = a_ext[:, gi * POOL_GROUP:(gi + 1) * POOL_GROUP]
        s = a
        k = 1
        while 2 * k < w:
            s = s + _shift_rows(s, -k)
            k *= 2
        s = _shift_rows(s, k) + s
        count = jnp.minimum(t + w - w // 2, n_total) - jnp.maximum(t - w // 2, 0)
        pooled = s[halo:halo + tile] * (1.0 / count.astype(_F32)) - a[halo:halo + tile]
        mixed.append(_mm(pooled.astype(_BF16), pw_ref[gi]))
    o_ref[0] = x + jnp.concatenate(mixed, axis=-1) * (ps_ref[...] * mod_ref[2:3, :])


def _pool(x, mods, gains, pool_w, pool_scale, layer):
    b, n, d = x.shape
    tile = WIDE_TILE
    prev_spec, main_spec, next_spec = _halo_specs(tile, POOL_HALO, n, d)
    return pl.pallas_call(
        functools.partial(_pool_kernel, layer=layer),
        grid=(b, n // tile),
        in_specs=[
            prev_spec, main_spec, next_spec,
            _mod_spec(d, layer, lambda bi, i: bi),
            _layer_spec(gains, layer),
            _layer_spec(pool_w, layer // 2),
            _layer_spec(pool_scale, layer // 2),
        ],
        out_specs=main_spec,
        out_shape=jax.ShapeDtypeStruct((b, n, d), _F32),
        compiler_params=_compiler_params(2),
        name="pool",
    )(x, x, x, mods, gains, pool_w, pool_scale)


def kernel(x, c, ctx, c_ctx, ada_w, ada_b, mix_norm, ffn_norm, even_w_in, even_q_gain, even_k_gain, even_conv_w,
           even_w_out, odd_pool_w, odd_pool_scale, ffn_w_up, ffn_conv_w, ffn_conv_b, ffn_w_down):
    b, n, d = x.shape
    depth = ada_w.shape[0]
    lc = ctx.shape[1]
    assert depth == 2 and d == D_MODEL and b + 1 == N_COND
    assert n % TOKEN_TILE == 0 and n % Q_TILE == 0 and n % WIDE_TILE == 0
    assert even_q_gain.shape[0] == even_k_gain.shape[0] == odd_pool_scale.shape[0] == 1

    cond = jnp.concatenate([c, c_ctx[None, :], jnp.zeros((SUBLANES - b - 1, d), _F32)], axis=0)
    mods = _adaln(cond.T, ada_w, ada_b).reshape(depth, SUBLANES, N_MOD, d)

    w_in, w_out, pool_w, w_up, w_down = even_w_in, even_w_out, odd_pool_w, ffn_w_up, ffn_w_down
    conv_wb = jnp.concatenate([ffn_conv_w, ffn_conv_b[:, None, :],
                               jnp.zeros((depth, SUBLANES - ffn_conv_w.shape[1] - 1, D_FF), _F32)], axis=1)

    qt, k_all, vt_all, conv = _in_proj(x, mods, mix_norm, w_in, even_q_gain, even_k_gain, even_conv_w, n + lc)
    k_all, vt_all = _ctx_kv(ctx, mods, mix_norm, w_in, even_k_gain, k_all, vt_all, n)
    x, w_up0, w_down0 = _attn_out(qt, k_all, vt_all, conv, x, mods, w_out, w_up, w_down, EVEN_LAYER)
    x, w_up1, w_down1 = _ffn(x, mods, ffn_norm, w_up0, conv_wb, w_down0, EVEN_LAYER, next_weights=(w_up, w_down))

    x = _pool(x, mods, mix_norm, pool_w, odd_pool_scale, ODD_LAYER)
    (x,) = _ffn(x, mods, ffn_norm, w_up1, conv_wb, w_down1, ODD_LAYER)
    return x
```

```python
import functools

import jax
import jax.numpy as jnp
import numpy as np
from jax import lax
from jax.experimental import pallas as pl
from jax.experimental.pallas import tpu as pltpu

D_MODEL = 1024
GRID_W = 64
HEAD_DIM = 128
N_Q_HEADS = 4
N_KV_HEADS = 2
Q_PER_KV = N_Q_HEADS // N_KV_HEADS
ATTN_WIDTH = N_Q_HEADS * HEAD_DIM
KV_WIDTH = N_KV_HEADS * HEAD_DIM
CONV_WIDTH = D_MODEL - ATTN_WIDTH
IN_PROJ_WIDTH = ATTN_WIDTH + 2 * KV_WIDTH + 3 * CONV_WIDTH
ROPE_THETA = 10000.0
ROPE_FREQS = HEAD_DIM // 4
SCORE_SCALE_LOG2 = HEAD_DIM ** -0.5 * 1.4426950408889634
POOL_WINDOWS = (2, 4, 8, 16)
POOL_GROUP = D_MODEL // len(POOL_WINDOWS)
D_FF = 2816
N_MOD = 6
N_COND = 3
EVEN_LAYER, ODD_LAYER = 0, 1
EPS = 1e-6

SUBLANES = 8
VMEM_LIMIT_BYTES = 56 * 1024 * 1024

HALO = SUBLANES
POOL_HALO = 2 * SUBLANES
TOKEN_TILE = 1024
WIDE_TILE = 1024
Q_TILE = 512
KV_TILE = 768
ONES_ROWS = 16
FF_CHUNK = 256
ADALN_COLS = 1536

_BF16 = jnp.bfloat16
_F32 = jnp.float32


def _compiler_params(n_axes):
    return pltpu.CompilerParams(dimension_semantics=("arbitrary",) * n_axes,
                                vmem_limit_bytes=VMEM_LIMIT_BYTES)


def _resident(shape, index_map):
    return pl.BlockSpec(shape, index_map, pipeline_mode=pl.Buffered(1))


def _layer_spec(arr, layer):
    if arr.ndim == 2:
        return _resident(arr.shape, lambda *_: (0, 0))
    zeros = (0,) * (arr.ndim - 1)
    return _resident((None,) + arr.shape[1:], lambda *_: (layer,) + zeros)


def _layer_row(ref, layer):
    return ref[layer:layer + 1, :]


def _mod_spec(d, layer, row_of):
    return pl.BlockSpec((None, None, N_MOD, d), lambda *ids: (layer, row_of(*ids), 0, 0))


def _cast_plan(w, layer, grid):
    rows, cols = w.shape[1:]
    n_steps = grid[0] * grid[1]
    block, shared = rows // n_steps, 1
    while block % (2 * SUBLANES):
        block, shared = 2 * block, 2 * shared
    assert rows % block == 0 and n_steps * block == rows * shared
    block_of = lambda b, i: (b * grid[1] + i) // shared
    return (pl.BlockSpec((None, block, cols), lambda b, i: (layer, block_of(b, i), 0)),
            pl.BlockSpec((block, cols), lambda b, i: (block_of(b, i), 0)),
            jax.ShapeDtypeStruct((rows, cols), _BF16))


def _norm_modulate(x, gain, shift, scale):
    y = x * lax.rsqrt(jnp.mean(x * x, axis=-1, keepdims=True) + EPS)
    return y * (gain * (1.0 + scale)) + shift


def _mm(a, w):
    return lax.dot_general(a, w, (((1,), (0,)), ((), ())), preferred_element_type=_F32)


def _silu(x):
    return x * (1.0 / (1.0 + jnp.exp(-x)))


def _shift_rows(u, k):
    return pltpu.roll(u, k % u.shape[0], axis=0)


def _adaln_kernel(ct_ref, w_ref, b_ref, o_ref):
    s = _silu(ct_ref[...])
    w = w_ref[0]
    rows = [jnp.sum(s[:, r:r + 1] * w, axis=0, keepdims=True) + b_ref[0] for r in range(N_COND)]
    rows.append(jnp.zeros((SUBLANES - N_COND, w.shape[1]), _F32))
    o_ref[0] = jnp.concatenate(rows, axis=0)


def _adaln(cond_t, ada_w, ada_b):
    depth, d, width = ada_w.shape
    return pl.pallas_call(
        _adaln_kernel,
        grid=(depth, width // ADALN_COLS),
        in_specs=[
            pl.BlockSpec((d, SUBLANES), lambda l, j: (0, 0)),
            pl.BlockSpec((1, d, ADALN_COLS), lambda l, j: (l, 0, j)),
            pl.BlockSpec((1, 1, ADALN_COLS), lambda l, j: (l, 0, j)),
        ],
        out_specs=pl.BlockSpec((1, SUBLANES, ADALN_COLS), lambda l, j: (l, 0, j)),
        out_shape=jax.ShapeDtypeStruct((depth, SUBLANES, width), _F32),
        compiler_params=_compiler_params(2),
        name="adaln",
    )(cond_t, ada_w, ada_b.reshape(depth, 1, width))


def _head_rms(xh, gain):
    return xh * lax.rsqrt(jnp.mean(xh * xh, axis=-1, keepdims=True) + EPS) * gain


def _rope(xh, cos, sin_signed, first_half):
    partner = jnp.where(first_half, pltpu.roll(xh, HEAD_DIM - ROPE_FREQS, axis=1), pltpu.roll(xh, ROPE_FREQS, axis=1))
    return xh * cos + partner * sin_signed


def _rope_axis_tables(n_rows):
    axis_dim = HEAD_DIM // 2
    f32 = np.float32
    inv_freq = np.power(f32(ROPE_THETA), -np.arange(0, axis_dim, 2, dtype=f32) / f32(axis_dim)).astype(f32)
    ang_r = np.arange(n_rows, dtype=f32)[:, None] * inv_freq
    ang_c = np.arange(GRID_W, dtype=f32)[:, None] * inv_freq
    zr, zc = np.zeros((n_rows, axis_dim), f32), np.zeros((GRID_W, axis_dim), f32)
    return (np.concatenate([np.cos(ang_r), np.cos(ang_r), zr], axis=1),
            np.concatenate([-np.sin(ang_r), np.sin(ang_r), zr], axis=1),
            np.concatenate([zc, np.cos(ang_c), np.cos(ang_c)], axis=1),
            np.concatenate([zc, -np.sin(ang_c), np.sin(ang_c)], axis=1))


def _tile_table(row_ref, col_ref):
    col = col_ref[...]
    return jnp.concatenate([jnp.broadcast_to(row_ref[r:r + 1, :], col.shape) + col for r in range(row_ref.shape[0])],
                           axis=0)


def _halo_specs(tile, halo, n, d):
    per_tile = tile // halo
    n_halo_blocks = n // halo
    prev_spec = pl.BlockSpec((1, halo, d), lambda b, i: (b, jnp.maximum(i * per_tile - 1, 0), 0))
    main_spec = pl.BlockSpec((1, tile, d), lambda b, i: (b, i, 0))
    next_spec = pl.BlockSpec((1, halo, d), lambda b, i: (b, jnp.minimum((i + 1) * per_tile, n_halo_blocks - 1), 0))
    return prev_spec, main_spec, next_spec


def _edge_flags():
    i = pl.program_id(1)
    return (i > 0).astype(_F32), (i < pl.num_programs(1) - 1).astype(_F32)


def _in_proj_kernel(xp_ref, x_ref, xn_ref, mod_ref, gain_ref, w_ref, qg_ref, kg_ref, cw_ref,
                    rcos_ref, rsin_ref, ccos_ref, csin_ref, qt_ref, k_ref, vt_ref, conv_ref,
                    aext_ref, amain_ref, qkv_ref, cx_ref, gb_ref):
    tile = x_ref.shape[1]
    o_k = ATTN_WIDTH
    o_v = o_k + KV_WIDTH
    o_b = o_v + KV_WIDTH
    o_c = o_b + CONV_WIDTH
    every_step = pl.program_id(1) < pl.num_programs(1)

    @pl.when(every_step)
    def _():
        gain, shift, scale = _layer_row(gain_ref, EVEN_LAYER), mod_ref[0:1, :], mod_ref[1:2, :]
        prev_ok, next_ok = _edge_flags()
        a_main = _norm_modulate(x_ref[0], gain, shift, scale)
        a_prev = _norm_modulate(xp_ref[0], gain, shift, scale) * prev_ok
        a_next = _norm_modulate(xn_ref[0], gain, shift, scale) * next_ok
        aext_ref[...] = jnp.concatenate([a_prev, a_main, a_next], axis=0).astype(_BF16)
        a_mb = a_main.astype(_BF16)
        amain_ref[...] = a_mb
        qkv_ref[...] = _mm(a_mb, w_ref[:, 0:o_b])

    cx_ref[...] = _mm(aext_ref[...], w_ref[:, o_c:])
    gb_ref[...] = _mm(amain_ref[...], w_ref[:, o_b:o_c])
    cos, sin = _tile_table(rcos_ref, ccos_ref), _tile_table(rsin_ref, csin_ref)
    lane = lax.broadcasted_iota(jnp.int32, (1, HEAD_DIM), 1)
    first_half = (lane % (2 * ROPE_FREQS)) < ROPE_FREQS
    for h in range(N_Q_HEADS):
        sl = slice(h * HEAD_DIM, (h + 1) * HEAD_DIM)
        qh = _rope(_head_rms(qkv_ref[:, sl], qg_ref[...]), cos, sin, first_half)
        qt_ref[0, sl, :] = (qh * SCORE_SCALE_LOG2).T.astype(_BF16)
    for h in range(N_KV_HEADS):
        sl = slice(h * HEAD_DIM, (h + 1) * HEAD_DIM)
        kh = qkv_ref[:, o_k + h * HEAD_DIM:o_k + (h + 1) * HEAD_DIM]
        k_ref[0, :, sl] = _rope(_head_rms(kh, kg_ref[...]), cos, sin, first_half).astype(_BF16)
    vt_ref[0] = qkv_ref[:, o_v:o_b].T.astype(_BF16)

    @pl.when(every_step)
    def _():
        cx = cx_ref[...]
        u = cx[:, :CONV_WIDTH] * cx[:, CONV_WIDTH:]
        cw = cw_ref[...]
        conv = cw[0:1] * _shift_rows(u, 1) + cw[1:2] * u + cw[2:3] * _shift_rows(u, -1)
        conv_ref[0] = (gb_ref[...] * conv[HALO:HALO + tile]).astype(_BF16)


def _in_proj(x, mods, gains, w_in, q_gain, k_gain, conv_w, n_keys):
    b, n, d = x.shape
    tile = WIDE_TILE
    rows_per_tile = tile // GRID_W
    assert tile % GRID_W == 0
    prev_spec, main_spec, next_spec = _halo_specs(tile, HALO, n, d)
    rope = _rope_axis_tables(n // GRID_W)
    row_table = pl.BlockSpec((rows_per_tile, HEAD_DIM), lambda bi, i: (i, 0))
    col_table = _resident((GRID_W, HEAD_DIM), lambda bi, i: (0, 0))
    return pl.pallas_call(
        _in_proj_kernel,
        grid=(b, n // tile),
        in_specs=[
            prev_spec, main_spec, next_spec,
            _mod_spec(d, 0, lambda bi, i: bi),
            _layer_spec(gains, 0),
            _layer_spec(w_in, 0),
            _layer_spec(q_gain, 0), _layer_spec(k_gain, 0),
            _layer_spec(conv_w, 0),
            row_table, row_table, col_table, col_table,
        ],
        out_specs=[pl.BlockSpec((1, ATTN_WIDTH, tile), lambda bi, i: (bi, 0, i)),
                   pl.BlockSpec((1, tile, KV_WIDTH), lambda bi, i: (bi, i, 0)),
                   pl.BlockSpec((1, KV_WIDTH, tile), lambda bi, i: (bi, 0, i)),
                   pl.BlockSpec((1, tile, CONV_WIDTH), lambda bi, i: (bi, i, 0))],
        out_shape=[jax.ShapeDtypeStruct((b, ATTN_WIDTH, n), _BF16), jax.ShapeDtypeStruct((b, n_keys, KV_WIDTH), _BF16),
                   jax.ShapeDtypeStruct((b, KV_WIDTH, n_keys), _BF16), jax.ShapeDtypeStruct((b, n, CONV_WIDTH), _BF16)],
        scratch_shapes=[
            pltpu.VMEM((tile + 2 * HALO, d), _BF16),
            pltpu.VMEM((tile, d), _BF16),
            pltpu.VMEM((tile, ATTN_WIDTH + 2 * KV_WIDTH), _F32),
            pltpu.VMEM((tile + 2 * HALO, 2 * CONV_WIDTH), _F32),
            pltpu.VMEM((tile, CONV_WIDTH), _F32),
        ],
        compiler_params=_compiler_params(2),
        name="in_proj",
    )(x, x, x, mods, gains, w_in, q_gain, k_gain, conv_w, rope[0], rope[1], rope[2], rope[3])


def _ctx_kv_kernel(ctx_ref, mod_ref, gain_ref, w_ref, kg_ref, k_in_ref, vt_in_ref, k_ref, vt_ref):
    del k_in_ref, vt_in_ref
    a = _norm_modulate(ctx_ref[0], _layer_row(gain_ref, EVEN_LAYER), mod_ref[0:1, :], mod_ref[1:2, :]).astype(_BF16)
    kv = _mm(a, w_ref[...])
    for h in range(N_KV_HEADS):
        sl = slice(h * HEAD_DIM, (h + 1) * HEAD_DIM)
        k_ref[0, :, sl] = _head_rms(kv[:, sl], kg_ref[...]).astype(_BF16)
    vt_ref[0] = kv[:, KV_WIDTH:].T.astype(_BF16)


def _ctx_kv(ctx, mods, gains, w_in, k_gain, k_all, vt_all, n_latent):
    b, lc, d = ctx.shape
    assert n_latent % lc == 0 and 2 * KV_WIDTH == ATTN_WIDTH
    ctx_block = n_latent // lc
    return pl.pallas_call(
        _ctx_kv_kernel,
        grid=(b,),
        in_specs=[
            pl.BlockSpec((1, lc, d), lambda i: (i, 0, 0)),
            _mod_spec(d, 0, lambda i: b),
            _layer_spec(gains, 0),
            _resident((None, d, 2 * KV_WIDTH), lambda i: (0, 0, 1)),
            _layer_spec(k_gain, 0),
            pl.BlockSpec(memory_space=pl.ANY), pl.BlockSpec(memory_space=pl.ANY),
        ],
        out_specs=[pl.BlockSpec((1, lc, KV_WIDTH), lambda i: (i, ctx_block, 0)),
                   pl.BlockSpec((1, KV_WIDTH, lc), lambda i: (i, 0, ctx_block))],
        out_shape=[jax.ShapeDtypeStruct(k_all.shape, _BF16), jax.ShapeDtypeStruct(vt_all.shape, _BF16)],
        input_output_aliases={5: 0, 6: 1},
        compiler_params=_compiler_params(1),
        name="ctx_kv",
    )(ctx, mods, gains, w_in, k_gain, k_all, vt_all)


def _attn_kernel(qt_ref, k_ref, vt_ref, conv_ref, x_ref, mod_ref, wo_ref, wup_f32_ref, wdown_f32_ref,
                 o_ref, wup_bf16_ref, wdown_bf16_ref, s_ref, cmax_ref, m_ref, acc_ref, attn_ref):
    n_chunks = k_ref.shape[1] // KV_TILE

    def keys_of(j):
        if isinstance(j, int):
            return slice(j * KV_TILE, (j + 1) * KV_TILE)
        return pl.ds(pl.multiple_of(j * KV_TILE, KV_TILE), KV_TILE)

    def score_stage(j, head):
        g = head // Q_PER_KV
        kch = k_ref[0, keys_of(j), g * HEAD_DIM:(g + 1) * HEAD_DIM]
        qt = qt_ref[0, head * HEAD_DIM:(head + 1) * HEAD_DIM, :]
        s = jnp.dot(kch, qt, preferred_element_type=_F32)
        s_ref[head % 2] = s
        cmax_ref[head % 2] = jnp.max(s, axis=0, keepdims=True)

    def value_stage(j, head):
        g = head // Q_PER_KV
        vtch = vt_ref[0, g * HEAD_DIM:(g + 1) * HEAD_DIM, keys_of(j)]
        v_aug = jnp.concatenate([vtch, jnp.ones((ONES_ROWS, KV_TILE), _BF16)], axis=0)
        m_prev = m_ref[head]
        m_new = jnp.maximum(m_prev, cmax_ref[head % 2])
        alpha = jnp.exp2(m_prev - m_new)
        p = jnp.exp2(s_ref[head % 2] - m_new).astype(_BF16)
        acc_ref[head] = alpha * acc_ref[head] + jnp.dot(v_aug, p, preferred_element_type=_F32)
        m_ref[head] = m_new

    def chunk(j, last):
        for head in range(N_Q_HEADS):
            if head + 1 < N_Q_HEADS:
                score_stage(j, head + 1)
            elif not last:
                score_stage(j + 1, 0)
            value_stage(j, head)

    m_ref[...] = jnp.full(m_ref.shape, -jnp.inf, _F32)
    acc_ref[...] = jnp.zeros(acc_ref.shape, _F32)
    score_stage(0, 0)

    def body(j, carry):
        chunk(j, last=False)
        return carry

    lax.fori_loop(0, n_chunks - 1, body, 0, unroll=5)
    chunk(n_chunks - 1, last=True)
    for head in range(N_Q_HEADS):
        acc = acc_ref[head]
        out_t = acc[:HEAD_DIM] / acc[HEAD_DIM:HEAD_DIM + 1]
        attn_ref[:, head * HEAD_DIM:(head + 1) * HEAD_DIM] = out_t.T.astype(_BF16)

    mixed = jnp.concatenate([attn_ref[...], conv_ref[0]], axis=-1)
    y = _mm(mixed, wo_ref[...])
    o_ref[0] = x_ref[0] + mod_ref[2:3, :] * y
    wup_bf16_ref[...] = wup_f32_ref[...].astype(_BF16)
    wdown_bf16_ref[...] = wdown_f32_ref[...].astype(_BF16)


def _attn_out(qt, k_all, vt_all, conv, x, mods, w_out, w_up, w_down, ffn_layer):
    b, n, d = x.shape
    n_keys = k_all.shape[1]
    assert n_keys % KV_TILE == 0
    tq = Q_TILE
    grid = (b, n // tq)
    per_batch = lambda rows, width: pl.BlockSpec((1, rows, width), lambda bi, i: (bi, 0, 0))
    tile = lambda width: pl.BlockSpec((1, tq, width), lambda bi, i: (bi, i, 0))
    up_in, up_out, up_shape = _cast_plan(w_up, ffn_layer, grid)
    down_in, down_out, down_shape = _cast_plan(w_down, ffn_layer, grid)
    return pl.pallas_call(
        _attn_kernel,
        grid=grid,
        in_specs=[
            pl.BlockSpec((1, ATTN_WIDTH, tq), lambda bi, i: (bi, 0, i)),
            per_batch(n_keys, KV_WIDTH), per_batch(KV_WIDTH, n_keys),
            tile(CONV_WIDTH),
            tile(d),
            _mod_spec(d, 0, lambda bi, i: bi),
            _layer_spec(w_out, 0),
            up_in, down_in,
        ],
        out_specs=[tile(d), up_out, down_out],
        out_shape=[jax.ShapeDtypeStruct((b, n, d), _F32), up_shape, down_shape],
        scratch_shapes=[
            pltpu.VMEM((2, KV_TILE, tq), _F32),
            pltpu.VMEM((2, 1, tq), _F32),
            pltpu.VMEM((N_Q_HEADS, 1, tq), _F32),
            pltpu.VMEM((N_Q_HEADS, HEAD_DIM + ONES_ROWS, tq), _F32),
            pltpu.VMEM((tq, ATTN_WIDTH), _BF16),
        ],
        compiler_params=_compiler_params(2),
        name="attn_out",
    )(qt, k_all, vt_all, conv, x, mods, w_out, w_up, w_down)


def _ffn_kernel(xp_ref, x_ref, xn_ref, mod_ref, gain_ref, wup_ref, cwb_ref, wd_ref, *rest, layer):
    if len(rest) == 2:
        o_ref, h_ref = rest
    else:
        wup_f32_ref, wdown_f32_ref, o_ref, wup_bf16_ref, wdown_bf16_ref, h_ref = rest
        wup_bf16_ref[...] = wup_f32_ref[...].astype(_BF16)
        wdown_bf16_ref[...] = wdown_f32_ref[...].astype(_BF16)
    tile = x_ref.shape[1]
    gain, shift, scale = _layer_row(gain_ref, layer), mod_ref[3:4, :], mod_ref[4:5, :]
    x = x_ref[0]
    prev_ok, next_ok = _edge_flags()
    f_main = _norm_modulate(x, gain, shift, scale)
    f_prev = _norm_modulate(xp_ref[0], gain, shift, scale) * prev_ok
    f_next = _norm_modulate(xn_ref[0], gain, shift, scale) * next_ok
    f_ext = jnp.concatenate([f_prev, f_main, f_next], axis=0).astype(_BF16)
    f_mb = f_main.astype(_BF16)
    n_chunks = D_FF // FF_CHUNK

    def up(c):
        cols = slice(c * FF_CHUNK, (c + 1) * FF_CHUNK)
        val_cols = slice(D_FF + c * FF_CHUNK, D_FF + (c + 1) * FF_CHUNK)
        return _mm(f_ext, wup_ref[:, cols]), _mm(f_mb, wup_ref[:, val_cols])

    def act(c, gate, val):
        cols = slice(c * FF_CHUNK, (c + 1) * FF_CHUNK)
        cwb = cwb_ref[:, cols]
        conv = cwb[0:1] * _shift_rows(gate, 1) + cwb[1:2] * gate + cwb[2:3] * _shift_rows(gate, -1)
        h_ref[:, cols] = (_silu(conv[HALO:HALO + tile] + cwb[3:4]) * val).astype(_BF16)

    nxt = up(0)
    for c in range(n_chunks):
        cur = nxt
        if c + 1 < n_chunks:
            nxt = up(c + 1)
        act(c, *cur)
    y = _mm(h_ref[...], wd_ref[...])
    o_ref[0] = x + mod_ref[5:6, :] * y


def _ffn(x, mods, gains, w_up_bf16, conv_wb, w_down_bf16, layer, next_weights=None):
    b, n, d = x.shape
    tile = TOKEN_TILE
    grid = (b, n // tile)
    prev_spec, main_spec, next_spec = _halo_specs(tile, HALO, n, d)
    whole = lambda a: _resident(a.shape, lambda bi, i: (0, 0))
    in_specs = [
        prev_spec, main_spec, next_spec,
        _mod_spec(d, layer, lambda bi, i: bi),
        _layer_spec(gains, layer),
        whole(w_up_bf16), _layer_spec(conv_wb, layer), whole(w_down_bf16),
    ]
    args = [x, x, x, mods, gains, w_up_bf16, conv_wb, w_down_bf16]
    out_specs, out_shape = [main_spec], [jax.ShapeDtypeStruct((b, n, d), _F32)]
    if next_weights is not None:
        for w in next_weights:
            cast_in, cast_out, cast_shape = _cast_plan(w, layer + 1, grid)
            in_specs.append(cast_in)
            out_specs.append(cast_out)
            out_shape.append(cast_shape)
        args += list(next_weights)
    return pl.pallas_call(
        functools.partial(_ffn_kernel, layer=layer),
        grid=grid,
        in_specs=in_specs,
        out_specs=out_specs,
        out_shape=out_shape,
        scratch_shapes=[pltpu.VMEM((tile, D_FF), _BF16)],
        compiler_params=_compiler_params(2),
        name="ffn",
    )(*args)


def _pool_kernel(xp_ref, x_ref, xn_ref, mod_ref, gain_ref, pw_ref, ps_ref, o_ref, *, layer):
    tile = x_ref.shape[1]
    halo = xp_ref.shape[1]
    n_total = tile * pl.num_programs(1)
    gain, shift, scale = _layer_row(gain_ref, layer), mod_ref[0:1, :], mod_ref[1:2, :]
    x = x_ref[0]
    prev_ok, next_ok = _edge_flags()
    a_prev = _norm_modulate(xp_ref[0], gain, shift, scale) * prev_ok
    a_main = _norm_modulate(x, gain, shift, scale)
    a_next = _norm_modulate(xn_ref[0], gain, shift, scale) * next_ok
    a_ext = jnp.concatenate([a_prev, a_main, a_next], axis=0)
    edge = POOL_HALO // 2
    assert edge >= max(POOL_WINDOWS) // 2 and tile > 2 * edge
    rows = lax.broadcasted_iota(jnp.int32, (edge, 1), 0)
    t_top = pl.program_id(1) * tile + rows
    t_bot = t_top + (tile - edge)

    def inv_count(t, w):
        return 1.0 / (jnp.minimum(t + w - w // 2, n_total) - jnp.maximum(t - w // 2, 0)).astype(_F32)

    mixed = []
    for gi, w in enumerate(POOL_WINDOWS):
        a = a_ext[:, gi * POOL_GROUP:(gi + 1) * POOL_GROUP]
        s = a
        k = 1
        while 2 * k < w:
            s = s + _shift_rows(s, -k)
            k *= 2
        s = _shift_rows(s, k) + s
        s, a = s[halo:halo + tile], a[halo:halo + tile]
        pooled = jnp.concatenate([s[:edge] * inv_count(t_top, w) - a[:edge],
                                  s[edge:tile - edge] * (1.0 / w) - a[edge:tile - edge],
                                  s[tile - edge:] * inv_count(t_bot, w) - a[tile - edge:]], axis=0)
        mixed.append(_mm(pooled.astype(_BF16), pw_ref[gi]))
    o_ref[0] = x + jnp.concatenate(mixed, axis=-1) * (ps_ref[...] * mod_ref[2:3, :])


def _pool(x, mods, gains, pool_w, pool_scale, layer):
    b, n, d = x.shape
    tile = WIDE_TILE
    prev_spec, main_spec, next_spec = _halo_specs(tile, POOL_HALO, n, d)
    return pl.pallas_call(
        functools.partial(_pool_kernel, layer=layer),
        grid=(b, n // tile),
        in_specs=[
            prev_spec, main_spec, next_spec,
            _mod_spec(d, layer, lambda bi, i: bi),
            _layer_spec(gains, layer),
            _layer_spec(pool_w, layer // 2),
            _layer_spec(pool_scale, layer // 2),
        ],
        out_specs=main_spec,
        out_shape=jax.ShapeDtypeStruct((b, n, d), _F32),
        compiler_params=_compiler_params(2),
        name="pool",
    )(x, x, x, mods, gains, pool_w, pool_scale)


def kernel(x, c, ctx, c_ctx, ada_w, ada_b, mix_norm, ffn_norm, even_w_in, even_q_gain, even_k_gain, even_conv_w,
           even_w_out, odd_pool_w, odd_pool_scale, ffn_w_up, ffn_conv_w, ffn_conv_b, ffn_w_down):
    b, n, d = x.shape
    depth = ada_w.shape[0]
    lc = ctx.shape[1]
    assert depth == 2 and d == D_MODEL and b + 1 == N_COND
    assert n % TOKEN_TILE == 0 and n % Q_TILE == 0 and n % WIDE_TILE == 0
    assert even_q_gain.shape[0] == even_k_gain.shape[0] == odd_pool_scale.shape[0] == 1

    cond = jnp.concatenate([c, c_ctx[None, :], jnp.zeros((SUBLANES - b - 1, d), _F32)], axis=0)
    mods = _adaln(cond.T, ada_w, ada_b).reshape(depth, SUBLANES, N_MOD, d)

    w_in, w_out, pool_w, w_up, w_down = even_w_in, even_w_out, odd_pool_w, ffn_w_up, ffn_w_down
    conv_wb = jnp.concatenate([ffn_conv_w, ffn_conv_b[:, None, :],
                               jnp.zeros((depth, SUBLANES - ffn_conv_w.shape[1] - 1, D_FF), _F32)], axis=1)

    qt, k_all, vt_all, conv = _in_proj(x, mods, mix_norm, w_in, even_q_gain, even_k_gain, even_conv_w, n + lc)
    k_all, vt_all = _ctx_kv(ctx, mods, mix_norm, w_in, even_k_gain, k_all, vt_all, n)
    x, w_up0, w_down0 = _attn_out(qt, k_all, vt_all, conv, x, mods, w_out, w_up, w_down, EVEN_LAYER)
    x, w_up1, w_down1 = _ffn(x, mods, ffn_norm, w_up0, conv_wb, w_down0, EVEN_LAYER, next_weights=(w_up, w_down))

    x = _pool(x, mods, mix_norm, pool_w, odd_pool_scale, ODD_LAYER)
    (x,) = _ffn(x, mods, ffn_norm, w_up1, conv_wb, w_down1, ODD_LAYER)
    return x
```

```python
import functools

import jax
import jax.numpy as jnp
import numpy as np
from jax import lax
from jax.experimental import pallas as pl
from jax.experimental.pallas import tpu as pltpu

D_MODEL = 1024
GRID_W = 64
HEAD_DIM = 128
N_Q_HEADS = 4
N_KV_HEADS = 2
Q_PER_KV = N_Q_HEADS // N_KV_HEADS
ATTN_WIDTH = N_Q_HEADS * HEAD_DIM
KV_WIDTH = N_KV_HEADS * HEAD_DIM
CONV_WIDTH = D_MODEL - ATTN_WIDTH
IN_PROJ_WIDTH = ATTN_WIDTH + 2 * KV_WIDTH + 3 * CONV_WIDTH
ROPE_THETA = 10000.0
ROPE_FREQS = HEAD_DIM // 4
SCORE_SCALE_LOG2 = HEAD_DIM ** -0.5 * 1.4426950408889634
POOL_WINDOWS = (2, 4, 8, 16)
POOL_GROUP = D_MODEL // len(POOL_WINDOWS)
D_FF = 2816
N_MOD = 6
N_COND = 3
EVEN_LAYER, ODD_LAYER = 0, 1
EPS = 1e-6

SUBLANES = 8
VMEM_LIMIT_BYTES = 56 * 1024 * 1024

HALO = SUBLANES
POOL_HALO = 2 * SUBLANES
TOKEN_TILE = 1024
WIDE_TILE = 1024
Q_TILE = 512
KV_TILE = 768
ONES_ROWS = 16
FF_CHUNK = 256
ADALN_COLS = 1536

_BF16 = jnp.bfloat16
_F32 = jnp.float32


def _compiler_params(n_axes):
    return pltpu.CompilerParams(dimension_semantics=("arbitrary",) * n_axes,
                                vmem_limit_bytes=VMEM_LIMIT_BYTES)


def _resident(shape, index_map):
    return pl.BlockSpec(shape, index_map, pipeline_mode=pl.Buffered(1))


def _layer_spec(arr, layer):
    if arr.ndim == 2:
        return _resident(arr.shape, lambda *_: (0, 0))
    zeros = (0,) * (arr.ndim - 1)
    return _resident((None,) + arr.shape[1:], lambda *_: (layer,) + zeros)


def _layer_row(ref, layer):
    return ref[layer:layer + 1, :]


def _mod_spec(d, layer, row_of):
    return pl.BlockSpec((None, None, N_MOD, d), lambda *ids: (layer, row_of(*ids), 0, 0))


def _cast_plan(w, layer, grid):
    rows, cols = w.shape[1:]
    n_steps = grid[0] * grid[1]
    block, shared = rows // n_steps, 1
    while block % (2 * SUBLANES):
        block, shared = 2 * block, 2 * shared
    assert rows % block == 0 and n_steps * block == rows * shared
    block_of = lambda b, i: (b * grid[1] + i) // shared
    return (pl.BlockSpec((None, block, cols), lambda b, i: (layer, block_of(b, i), 0)),
            pl.BlockSpec((block, cols), lambda b, i: (block_of(b, i), 0)),
            jax.ShapeDtypeStruct((rows, cols), _BF16))


def _norm_modulate(x, gain, shift, scale):
    y = x * lax.rsqrt(jnp.mean(x * x, axis=-1, keepdims=True) + EPS)
    return y * (gain * (1.0 + scale)) + shift


def _mm(a, w):
    return lax.dot_general(a, w, (((1,), (0,)), ((), ())), preferred_element_type=_F32)


def _silu(x):
    h = 0.5 * x
    return h + h * jnp.tanh(h)


def _shift_rows(u, k):
    return pltpu.roll(u, k % u.shape[0], axis=0)


def _adaln_kernel(ct_ref, w_ref, b_ref, o_ref):
    s = _silu(ct_ref[...])
    w = w_ref[0]
    rows = [jnp.sum(s[:, r:r + 1] * w, axis=0, keepdims=True) + b_ref[0] for r in range(N_COND)]
    rows.append(jnp.zeros((SUBLANES - N_COND, w.shape[1]), _F32))
    o_ref[0] = jnp.concatenate(rows, axis=0)


def _adaln(cond_t, ada_w, ada_b):
    depth, d, width = ada_w.shape
    return pl.pallas_call(
        _adaln_kernel,
        grid=(depth, width // ADALN_COLS),
        in_specs=[
            pl.BlockSpec((d, SUBLANES), lambda l, j: (0, 0)),
            pl.BlockSpec((1, d, ADALN_COLS), lambda l, j: (l, 0, j)),
            pl.BlockSpec((1, 1, ADALN_COLS), lambda l, j: (l, 0, j)),
        ],
        out_specs=pl.BlockSpec((1, SUBLANES, ADALN_COLS), lambda l, j: (l, 0, j)),
        out_shape=jax.ShapeDtypeStruct((depth, SUBLANES, width), _F32),
        compiler_params=_compiler_params(2),
        name="adaln",
    )(cond_t, ada_w, ada_b.reshape(depth, 1, width))


def _head_rms(xh, gain):
    return xh * lax.rsqrt(jnp.mean(xh * xh, axis=-1, keepdims=True) + EPS) * gain


def _rope(xh, cos, sin_signed, first_half):
    partner = jnp.where(first_half, pltpu.roll(xh, HEAD_DIM - ROPE_FREQS, axis=1), pltpu.roll(xh, ROPE_FREQS, axis=1))
    return xh * cos + partner * sin_signed


def _rope_axis_tables(n_rows):
    axis_dim = HEAD_DIM // 2
    f32 = np.float32
    inv_freq = np.power(f32(ROPE_THETA), -np.arange(0, axis_dim, 2, dtype=f32) / f32(axis_dim)).astype(f32)
    ang_r = np.arange(n_rows, dtype=f32)[:, None] * inv_freq
    ang_c = np.arange(GRID_W, dtype=f32)[:, None] * inv_freq
    zr, zc = np.zeros((n_rows, axis_dim), f32), np.zeros((GRID_W, axis_dim), f32)
    return (np.concatenate([np.cos(ang_r), np.cos(ang_r), zr], axis=1),
            np.concatenate([-np.sin(ang_r), np.sin(ang_r), zr], axis=1),
            np.concatenate([zc, np.cos(ang_c), np.cos(ang_c)], axis=1),
            np.concatenate([zc, -np.sin(ang_c), np.sin(ang_c)], axis=1))


def _tile_table(row_ref, col_ref):
    col = col_ref[...]
    return jnp.concatenate([jnp.broadcast_to(row_ref[r:r + 1, :], col.shape) + col for r in range(row_ref.shape[0])],
                           axis=0)


def _halo_specs(tile, halo, n, d):
    per_tile = tile // halo
    n_halo_blocks = n // halo
    prev_spec = pl.BlockSpec((1, halo, d), lambda b, i: (b, jnp.maximum(i * per_tile - 1, 0), 0))
    main_spec = pl.BlockSpec((1, tile, d), lambda b, i: (b, i, 0))
    next_spec = pl.BlockSpec((1, halo, d), lambda b, i: (b, jnp.minimum((i + 1) * per_tile, n_halo_blocks - 1), 0))
    return prev_spec, main_spec, next_spec


def _edge_flags():
    i = pl.program_id(1)
    return (i > 0).astype(_F32), (i < pl.num_programs(1) - 1).astype(_F32)


def _in_proj_kernel(xp_ref, x_ref, xn_ref, mod_ref, gain_ref, w_ref, qg_ref, kg_ref, cw_ref,
                    rcos_ref, rsin_ref, ccos_ref, csin_ref, qt_ref, k_ref, vt_ref, conv_ref,
                    aext_ref, amain_ref, qkv_ref, cx_ref, gb_ref):
    tile = x_ref.shape[1]
    o_k = ATTN_WIDTH
    o_v = o_k + KV_WIDTH
    o_b = o_v + KV_WIDTH
    o_c = o_b + CONV_WIDTH
    every_step = pl.program_id(1) < pl.num_programs(1)

    @pl.when(every_step)
    def _():
        gain, shift, scale = _layer_row(gain_ref, EVEN_LAYER), mod_ref[0:1, :], mod_ref[1:2, :]
        prev_ok, next_ok = _edge_flags()
        a_main = _norm_modulate(x_ref[0], gain, shift, scale)
        a_prev = _norm_modulate(xp_ref[0], gain, shift, scale) * prev_ok
        a_next = _norm_modulate(xn_ref[0], gain, shift, scale) * next_ok
        aext_ref[...] = jnp.concatenate([a_prev, a_main, a_next], axis=0).astype(_BF16)
        a_mb = a_main.astype(_BF16)
        amain_ref[...] = a_mb
        qkv_ref[...] = _mm(a_mb, w_ref[:, 0:o_b])

    cx_ref[...] = _mm(aext_ref[...], w_ref[:, o_c:])
    gb_ref[...] = _mm(amain_ref[...], w_ref[:, o_b:o_c])
    cos, sin = _tile_table(rcos_ref, ccos_ref), _tile_table(rsin_ref, csin_ref)
    lane = lax.broadcasted_iota(jnp.int32, (1, HEAD_DIM), 1)
    first_half = (lane % (2 * ROPE_FREQS)) < ROPE_FREQS
    for h in range(N_Q_HEADS):
        sl = slice(h * HEAD_DIM, (h + 1) * HEAD_DIM)
        qh = _rope(_head_rms(qkv_ref[:, sl], qg_ref[...]), cos, sin, first_half)
        qt_ref[0, sl, :] = (qh * SCORE_SCALE_LOG2).T.astype(_BF16)
    for h in range(N_KV_HEADS):
        sl = slice(h * HEAD_DIM, (h + 1) * HEAD_DIM)
        kh = qkv_ref[:, o_k + h * HEAD_DIM:o_k + (h + 1) * HEAD_DIM]
        k_ref[0, :, sl] = _rope(_head_rms(kh, kg_ref[...]), cos, sin, first_half).astype(_BF16)
    vt_ref[0] = qkv_ref[:, o_v:o_b].T.astype(_BF16)

    @pl.when(every_step)
    def _():
        cx = cx_ref[...]
        u = cx[:, :CONV_WIDTH] * cx[:, CONV_WIDTH:]
        cw = cw_ref[...]
        conv = cw[0:1] * _shift_rows(u, 1) + cw[1:2] * u + cw[2:3] * _shift_rows(u, -1)
        conv_ref[0] = (gb_ref[...] * conv[HALO:HALO + tile]).astype(_BF16)


def _in_proj(x, mods, gains, w_in, q_gain, k_gain, conv_w, n_keys):
    b, n, d = x.shape
    tile = WIDE_TILE
    rows_per_tile = tile // GRID_W
    assert tile % GRID_W == 0
    prev_spec, main_spec, next_spec = _halo_specs(tile, HALO, n, d)
    rope = _rope_axis_tables(n // GRID_W)
    row_table = pl.BlockSpec((rows_per_tile, HEAD_DIM), lambda bi, i: (i, 0))
    col_table = _resident((GRID_W, HEAD_DIM), lambda bi, i: (0, 0))
    return pl.pallas_call(
        _in_proj_kernel,
        grid=(b, n // tile),
        in_specs=[
            prev_spec, main_spec, next_spec,
            _mod_spec(d, 0, lambda bi, i: bi),
            _layer_spec(gains, 0),
            _layer_spec(w_in, 0),
            _layer_spec(q_gain, 0), _layer_spec(k_gain, 0),
            _layer_spec(conv_w, 0),
            row_table, row_table, col_table, col_table,
        ],
        out_specs=[pl.BlockSpec((1, ATTN_WIDTH, tile), lambda bi, i: (bi, 0, i)),
                   pl.BlockSpec((1, tile, KV_WIDTH), lambda bi, i: (bi, i, 0)),
                   pl.BlockSpec((1, KV_WIDTH, tile), lambda bi, i: (bi, 0, i)),
                   pl.BlockSpec((1, tile, CONV_WIDTH), lambda bi, i: (bi, i, 0))],
        out_shape=[jax.ShapeDtypeStruct((b, ATTN_WIDTH, n), _BF16), jax.ShapeDtypeStruct((b, n_keys, KV_WIDTH), _BF16),
                   jax.ShapeDtypeStruct((b, KV_WIDTH, n_keys), _BF16), jax.ShapeDtypeStruct((b, n, CONV_WIDTH), _BF16)],
        scratch_shapes=[
            pltpu.VMEM((tile + 2 * HALO, d), _BF16),
            pltpu.VMEM((tile, d), _BF16),
            pltpu.VMEM((tile, ATTN_WIDTH + 2 * KV_WIDTH), _F32),
            pltpu.VMEM((tile + 2 * HALO, 2 * CONV_WIDTH), _F32),
            pltpu.VMEM((tile, CONV_WIDTH), _F32),
        ],
        compiler_params=_compiler_params(2),
        name="in_proj",
    )(x, x, x, mods, gains, w_in, q_gain, k_gain, conv_w, rope[0], rope[1], rope[2], rope[3])


def _ctx_kv_kernel(ctx_ref, mod_ref, gain_ref, w_ref, kg_ref, k_in_ref, vt_in_ref, k_ref, vt_ref):
    del k_in_ref, vt_in_ref
    a = _norm_modulate(ctx_ref[0], _layer_row(gain_ref, EVEN_LAYER), mod_ref[0:1, :], mod_ref[1:2, :]).astype(_BF16)
    kv = _mm(a, w_ref[...])
    for h in range(N_KV_HEADS):
        sl = slice(h * HEAD_DIM, (h + 1) * HEAD_DIM)
        k_ref[0, :, sl] = _head_rms(kv[:, sl], kg_ref[...]).astype(_BF16)
    vt_ref[0] = kv[:, KV_WIDTH:].T.astype(_BF16)


def _ctx_kv(ctx, mods, gains, w_in, k_gain, k_all, vt_all, n_latent):
    b, lc, d = ctx.shape
    assert n_latent % lc == 0 and 2 * KV_WIDTH == ATTN_WIDTH
    ctx_block = n_latent // lc
    return pl.pallas_call(
        _ctx_kv_kernel,
        grid=(b,),
        in_specs=[
            pl.BlockSpec((1, lc, d), lambda i: (i, 0, 0)),
            _mod_spec(d, 0, lambda i: b),
            _layer_spec(gains, 0),
            _resident((None, d, 2 * KV_WIDTH), lambda i: (0, 0, 1)),
            _layer_spec(k_gain, 0),
            pl.BlockSpec(memory_space=pl.ANY), pl.BlockSpec(memory_space=pl.ANY),
        ],
        out_specs=[pl.BlockSpec((1, lc, KV_WIDTH), lambda i: (i, ctx_block, 0)),
                   pl.BlockSpec((1, KV_WIDTH, lc), lambda i: (i, 0, ctx_block))],
        out_shape=[jax.ShapeDtypeStruct(k_all.shape, _BF16), jax.ShapeDtypeStruct(vt_all.shape, _BF16)],
        input_output_aliases={5: 0, 6: 1},
        compiler_params=_compiler_params(1),
        name="ctx_kv",
    )(ctx, mods, gains, w_in, k_gain, k_all, vt_all)


def _attn_kernel(qt_ref, k_ref, vt_ref, conv_ref, x_ref, mod_ref, wo_ref, wup_f32_ref, wdown_f32_ref,
                 o_ref, wup_bf16_ref, wdown_bf16_ref, s_ref, cmax_ref, m_ref, acc_ref, attn_ref):
    n_chunks = k_ref.shape[1] // KV_TILE

    def keys_of(j):
        if isinstance(j, int):
            return slice(j * KV_TILE, (j + 1) * KV_TILE)
        return pl.ds(pl.multiple_of(j * KV_TILE, KV_TILE), KV_TILE)

    def score_stage(j, head):
        g = head // Q_PER_KV
        kch = k_ref[0, keys_of(j), g * HEAD_DIM:(g + 1) * HEAD_DIM]
        qt = qt_ref[0, head * HEAD_DIM:(head + 1) * HEAD_DIM, :]
        s = jnp.dot(kch, qt, preferred_element_type=_F32)
        s_ref[head % 2] = s
        cmax_ref[head % 2] = jnp.max(s, axis=0, keepdims=True)

    def value_stage(j, head):
        g = head // Q_PER_KV
        vtch = vt_ref[0, g * HEAD_DIM:(g + 1) * HEAD_DIM, keys_of(j)]
        v_aug = jnp.concatenate([vtch, jnp.ones((ONES_ROWS, KV_TILE), _BF16)], axis=0)
        m_prev = m_ref[head]
        m_new = jnp.maximum(m_prev, cmax_ref[head % 2])
        alpha = jnp.exp2(m_prev - m_new)
        p = jnp.exp2(s_ref[head % 2] - m_new).astype(_BF16)
        acc_ref[head] = alpha * acc_ref[head] + jnp.dot(v_aug, p, preferred_element_type=_F32)
        m_ref[head] = m_new

    def chunk(j, last):
        for head in range(N_Q_HEADS):
            if head + 1 < N_Q_HEADS:
                score_stage(j, head + 1)
            elif not last:
                score_stage(j + 1, 0)
            value_stage(j, head)

    m_ref[...] = jnp.full(m_ref.shape, -jnp.inf, _F32)
    acc_ref[...] = jnp.zeros(acc_ref.shape, _F32)
    score_stage(0, 0)

    def body(j, carry):
        chunk(j, last=False)
        return carry

    lax.fori_loop(0, n_chunks - 1, body, 0, unroll=5)
    chunk(n_chunks - 1, last=True)
    for head in range(N_Q_HEADS):
        acc = acc_ref[head]
        out_t = acc[:HEAD_DIM] / acc[HEAD_DIM:HEAD_DIM + 1]
        attn_ref[:, head * HEAD_DIM:(head + 1) * HEAD_DIM] = out_t.T.astype(_BF16)

    mixed = jnp.concatenate([attn_ref[...], conv_ref[0]], axis=-1)
    y = _mm(mixed, wo_ref[...])
    o_ref[0] = x_ref[0] + mod_ref[2:3, :] * y
    wup_bf16_ref[...] = wup_f32_ref[...].astype(_BF16)
    wdown_bf16_ref[...] = wdown_f32_ref[...].astype(_BF16)


def _attn_out(qt, k_all, vt_all, conv, x, mods, w_out, w_up, w_down, ffn_layer):
    b, n, d = x.shape
    n_keys = k_all.shape[1]
    assert n_keys % KV_TILE == 0
    tq = Q_TILE
    grid = (b, n // tq)
    per_batch = lambda rows, width: pl.BlockSpec((1, rows, width), lambda bi, i: (bi, 0, 0))
    tile = lambda width: pl.BlockSpec((1, tq, width), lambda bi, i: (bi, i, 0))
    up_in, up_out, up_shape = _cast_plan(w_up, ffn_layer, grid)
    down_in, down_out, down_shape = _cast_plan(w_down, ffn_layer, grid)
    return pl.pallas_call(
        _attn_kernel,
        grid=grid,
        in_specs=[
            pl.BlockSpec((1, ATTN_WIDTH, tq), lambda bi, i: (bi, 0, i)),
            per_batch(n_keys, KV_WIDTH), per_batch(KV_WIDTH, n_keys),
            tile(CONV_WIDTH),
            tile(d),
            _mod_spec(d, 0, lambda bi, i: bi),
            _layer_spec(w_out, 0),
            up_in, down_in,
        ],
        out_specs=[tile(d), up_out, down_out],
        out_shape=[jax.ShapeDtypeStruct((b, n, d), _F32), up_shape, down_shape],
        scratch_shapes=[
            pltpu.VMEM((2, KV_TILE, tq), _F32),
            pltpu.VMEM((2, 1, tq), _F32),
            pltpu.VMEM((N_Q_HEADS, 1, tq), _F32),
            pltpu.VMEM((N_Q_HEADS, HEAD_DIM + ONES_ROWS, tq), _F32),
            pltpu.VMEM((tq, ATTN_WIDTH), _BF16),
        ],
        compiler_params=_compiler_params(2),
        name="attn_out",
    )(qt, k_all, vt_all, conv, x, mods, w_out, w_up, w_down)


def _ffn_kernel(xp_ref, x_ref, xn_ref, mod_ref, gain_ref, wup_ref, cwb_ref, wd_ref, *rest, layer):
    if len(rest) == 2:
        o_ref, h_ref = rest
    else:
        wup_f32_ref, wdown_f32_ref, o_ref, wup_bf16_ref, wdown_bf16_ref, h_ref = rest
        wup_bf16_ref[...] = wup_f32_ref[...].astype(_BF16)
        wdown_bf16_ref[...] = wdown_f32_ref[...].astype(_BF16)
    tile = x_ref.shape[1]
    gain, shift, scale = _layer_row(gain_ref, layer), mod_ref[3:4, :], mod_ref[4:5, :]
    x = x_ref[0]
    prev_ok, next_ok = _edge_flags()
    f_main = _norm_modulate(x, gain, shift, scale)
    f_prev = _norm_modulate(xp_ref[0], gain, shift, scale) * prev_ok
    f_next = _norm_modulate(xn_ref[0], gain, shift, scale) * next_ok
    f_ext = jnp.concatenate([f_prev, f_main, f_next], axis=0).astype(_BF16)
    f_mb = f_main.astype(_BF16)
    n_chunks = D_FF // FF_CHUNK

    def up(c):
        cols = slice(c * FF_CHUNK, (c + 1) * FF_CHUNK)
        val_cols = slice(D_FF + c * FF_CHUNK, D_FF + (c + 1) * FF_CHUNK)
        return _mm(f_ext, wup_ref[:, cols]), _mm(f_mb, wup_ref[:, val_cols])

    def act(c, gate, val):
        cols = slice(c * FF_CHUNK, (c + 1) * FF_CHUNK)
        cwb = cwb_ref[:, cols]
        conv = cwb[0:1] * _shift_rows(gate, 1) + cwb[1:2] * gate + cwb[2:3] * _shift_rows(gate, -1)
        h_ref[:, cols] = (_silu(conv[HALO:HALO + tile] + cwb[3:4]) * val).astype(_BF16)

    nxt = up(0)
    for c in range(n_chunks):
        cur = nxt
        if c + 1 < n_chunks:
            nxt = up(c + 1)
        act(c, *cur)
    y = _mm(h_ref[...], wd_ref[...])
    o_ref[0] = x + mod_ref[5:6, :] * y


def _ffn(x, mods, gains, w_up_bf16, conv_wb, w_down_bf16, layer, next_weights=None):
    b, n, d = x.shape
    tile = TOKEN_TILE
    grid = (b, n // tile)
    prev_spec, main_spec, next_spec = _halo_specs(tile, HALO, n, d)
    whole = lambda a: _resident(a.shape, lambda bi, i: (0, 0))
    in_specs = [
        prev_spec, main_spec, next_spec,
        _mod_spec(d, layer, lambda bi, i: bi),
        _layer_spec(gains, layer),
        whole(w_up_bf16), _layer_spec(conv_wb, layer), whole(w_down_bf16),
    ]
    args = [x, x, x, mods, gains, w_up_bf16, conv_wb, w_down_bf16]
    out_specs, out_shape = [main_spec], [jax.ShapeDtypeStruct((b, n, d), _F32)]
    if next_weights is not None:
        for w in next_weights:
            cast_in, cast_out, cast_shape = _cast_plan(w, layer + 1, grid)
            in_specs.append(cast_in)
            out_specs.append(cast_out)
            out_shape.append(cast_shape)
        args += list(next_weights)
    return pl.pallas_call(
        functools.partial(_ffn_kernel, layer=layer),
        grid=grid,
        in_specs=in_specs,
        out_specs=out_specs,
        out_shape=out_shape,
        scratch_shapes=[pltpu.VMEM((tile, D_FF), _BF16)],
        compiler_params=_compiler_params(2),
        name="ffn",
    )(*args)


def _pool_kernel(xp_ref, x_ref, xn_ref, mod_ref, gain_ref, pw_ref, ps_ref, o_ref, *, layer):
    tile = x_ref.shape[1]
    halo = xp_ref.shape[1]
    n_total = tile * pl.num_programs(1)
    gain, shift, scale = _layer_row(gain_ref, layer), mod_ref[0:1, :], mod_ref[1:2, :]
    x = x_ref[0]
    prev_ok, next_ok = _edge_flags()
    a_prev = _norm_modulate(xp_ref[0], gain, shift, scale) * prev_ok
    a_main = _norm_modulate(x, gain, shift, scale)
    a_next = _norm_modulate(xn_ref[0], gain, shift, scale) * next_ok
    a_ext = jnp.concatenate([a_prev, a_main, a_next], axis=0)
    edge = POOL_HALO // 2
    assert edge >= max(POOL_WINDOWS) // 2 and tile > 2 * edge
    rows = lax.broadcasted_iota(jnp.int32, (edge, 1), 0)
    t_top = pl.program_id(1) * tile + rows
    t_bot = t_top + (tile - edge)

    def inv_count(t, w):
        return 1.0 / (jnp.minimum(t + w - w // 2, n_total) - jnp.maximum(t - w // 2, 0)).astype(_F32)

    mixed = []
    for gi, w in enumerate(POOL_WINDOWS):
        a = a_ext[:, gi * POOL_GROUP:(gi + 1) * POOL_GROUP]
        s = a
        k = 1
        while 2 * k < w:
            s = s + _shift_rows(s, -k)
            k *= 2
        s = _shift_rows(s, k) + s
        s, a = s[halo:halo + tile], a[halo:halo + tile]
        pooled = jnp.concatenate([s[:edge] * inv_count(t_top, w) - a[:edge],
                                  s[edge:tile - edge] * (1.0 / w) - a[edge:tile - edge],
                                  s[tile - edge:] * inv_count(t_bot, w) - a[tile - edge:]], axis=0)
        mixed.append(_mm(pooled.astype(_BF16), pw_ref[gi]))
    o_ref[0] = x + jnp.concatenate(mixed, axis=-1) * (ps_ref[...] * mod_ref[2:3, :])


def _pool(x, mods, gains, pool_w, pool_scale, layer):
    b, n, d = x.shape
    tile = WIDE_TILE
    prev_spec, main_spec, next_spec = _halo_specs(tile, POOL_HALO, n, d)
    return pl.pallas_call(
        functools.partial(_pool_kernel, layer=layer),
        grid=(b, n // tile),
        in_specs=[
            prev_spec, main_spec, next_spec,
            _mod_spec(d, layer, lambda bi, i: bi),
            _layer_spec(gains, layer),
            _layer_spec(pool_w, layer // 2),
            _layer_spec(pool_scale, layer // 2),
        ],
        out_specs=main_spec,
        out_shape=jax.ShapeDtypeStruct((b, n, d), _F32),
        compiler_params=_compiler_params(2),
        name="pool",
    )(x, x, x, mods, gains, pool_w, pool_scale)


def kernel(x, c, ctx, c_ctx, ada_w, ada_b, mix_norm, ffn_norm, even_w_in, even_q_gain, even_k_gain, even_conv_w,
           even_w_out, odd_pool_w, odd_pool_scale, ffn_w_up, ffn_conv_w, ffn_conv_b, ffn_w_down):
    b, n, d = x.shape
    depth = ada_w.shape[0]
    lc = ctx.shape[1]
    assert depth == 2 and d == D_MODEL and b + 1 == N_COND
    assert n % TOKEN_TILE == 0 and n % Q_TILE == 0 and n % WIDE_TILE == 0
    assert even_q_gain.shape[0] == even_k_gain.shape[0] == odd_pool_scale.shape[0] == 1

    cond = jnp.concatenate([c, c_ctx[None, :], jnp.zeros((SUBLANES - b - 1, d), _F32)], axis=0)
    mods = _adaln(cond.T, ada_w, ada_b).reshape(depth, SUBLANES, N_MOD, d)

    w_in, w_out, pool_w, w_up, w_down = even_w_in, even_w_out, odd_pool_w, ffn_w_up, ffn_w_down
    conv_wb = jnp.concatenate([ffn_conv_w, ffn_conv_b[:, None, :],
                               jnp.zeros((depth, SUBLANES - ffn_conv_w.shape[1] - 1, D_FF), _F32)], axis=1)

    qt, k_all, vt_all, conv = _in_proj(x, mods, mix_norm, w_in, even_q_gain, even_k_gain, even_conv_w, n + lc)
    k_all, vt_all = _ctx_kv(ctx, mods, mix_norm, w_in, even_k_gain, k_all, vt_all, n)
    x, w_up0, w_down0 = _attn_out(qt, k_all, vt_all, conv, x, mods, w_out, w_up, w_down, EVEN_LAYER)
    x, w_up1, w_down1 = _ffn(x, mods, ffn_norm, w_up0, conv_wb, w_down0, EVEN_LAYER, next_weights=(w_up, w_down))

    x = _pool(x, mods, mix_norm, pool_w, odd_pool_scale, ODD_LAYER)
    (x,) = _ffn(x, mods, ffn_norm, w_up1, conv_wb, w_down1, ODD_LAYER)
    return x
```

```python
import functools

import jax
import jax.numpy as jnp
import numpy as np
from jax import lax
from jax.experimental import pallas as pl
from jax.experimental.pallas import tpu as pltpu

D_MODEL = 1024
GRID_W = 64
HEAD_DIM = 128
N_Q_HEADS = 4
N_KV_HEADS = 2
Q_PER_KV = N_Q_HEADS // N_KV_HEADS
ATTN_WIDTH = N_Q_HEADS * HEAD_DIM
KV_WIDTH = N_KV_HEADS * HEAD_DIM
CONV_WIDTH = D_MODEL - ATTN_WIDTH
IN_PROJ_WIDTH = ATTN_WIDTH + 2 * KV_WIDTH + 3 * CONV_WIDTH
ROPE_THETA = 10000.0
ROPE_FREQS = HEAD_DIM // 4
SCORE_SCALE_LOG2 = HEAD_DIM ** -0.5 * 1.4426950408889634
POOL_WINDOWS = (2, 4, 8, 16)
POOL_GROUP = D_MODEL // len(POOL_WINDOWS)
D_FF = 2816
N_MOD = 6
N_COND = 3
EVEN_LAYER, ODD_LAYER = 0, 1
EPS = 1e-6

SUBLANES = 8
VMEM_LIMIT_BYTES = 56 * 1024 * 1024

HALO = SUBLANES
POOL_HALO = 2 * SUBLANES
TOKEN_TILE = 1024
WIDE_TILE = 1024
Q_TILE = 512
Q_TILES_PER_STEP = 2
KV_TILE = 768
ONES_ROWS = 16
FF_CHUNK = 256
ADALN_COLS = 1536

_BF16 = jnp.bfloat16
_F32 = jnp.float32


def _compiler_params(n_axes):
    return pltpu.CompilerParams(dimension_semantics=("arbitrary",) * n_axes,
                                vmem_limit_bytes=VMEM_LIMIT_BYTES)


def _resident(shape, index_map):
    return pl.BlockSpec(shape, index_map, pipeline_mode=pl.Buffered(1))


def _layer_spec(arr, layer):
    if arr.ndim == 2:
        return _resident(arr.shape, lambda *_: (0, 0))
    zeros = (0,) * (arr.ndim - 1)
    return _resident((None,) + arr.shape[1:], lambda *_: (layer,) + zeros)


def _layer_row(ref, layer):
    return ref[layer:layer + 1, :]


def _mod_spec(d, layer, row_of):
    return pl.BlockSpec((None, None, N_MOD, d), lambda *ids: (layer, row_of(*ids), 0, 0))


def _cast_plan(w, layer, grid):
    rows, cols = w.shape[1:]
    n_steps = grid[0] * grid[1]
    block, shared = rows // n_steps, 1
    while block % (2 * SUBLANES):
        block, shared = 2 * block, 2 * shared
    assert rows % block == 0 and n_steps * block == rows * shared
    block_of = lambda b, i: (b * grid[1] + i) // shared
    return (pl.BlockSpec((None, block, cols), lambda b, i: (layer, block_of(b, i), 0)),
            pl.BlockSpec((block, cols), lambda b, i: (block_of(b, i), 0)),
            jax.ShapeDtypeStruct((rows, cols), _BF16))


def _norm_modulate(x, gain, shift, scale):
    y = x * lax.rsqrt(jnp.mean(x * x, axis=-1, keepdims=True) + EPS)
    return y * (gain * (1.0 + scale)) + shift


def _mm(a, w):
    return lax.dot_general(a, w, (((1,), (0,)), ((), ())), preferred_element_type=_F32)


def _silu(x):
    h = 0.5 * x
    return h + h * jnp.tanh(h)


def _shift_rows(u, k):
    return pltpu.roll(u, k % u.shape[0], axis=0)


def _adaln_kernel(ct_ref, w_ref, b_ref, o_ref):
    s = _silu(ct_ref[...])
    w = w_ref[0]
    rows = [jnp.sum(s[:, r:r + 1] * w, axis=0, keepdims=True) + b_ref[0] for r in range(N_COND)]
    rows.append(jnp.zeros((SUBLANES - N_COND, w.shape[1]), _F32))
    o_ref[0] = jnp.concatenate(rows, axis=0)


def _adaln(cond_t, ada_w, ada_b):
    depth, d, width = ada_w.shape
    return pl.pallas_call(
        _adaln_kernel,
        grid=(depth, width // ADALN_COLS),
        in_specs=[
            pl.BlockSpec((d, SUBLANES), lambda l, j: (0, 0)),
            pl.BlockSpec((1, d, ADALN_COLS), lambda l, j: (l, 0, j)),
            pl.BlockSpec((1, 1, ADALN_COLS), lambda l, j: (l, 0, j)),
        ],
        out_specs=pl.BlockSpec((1, SUBLANES, ADALN_COLS), lambda l, j: (l, 0, j)),
        out_shape=jax.ShapeDtypeStruct((depth, SUBLANES, width), _F32),
        compiler_params=_compiler_params(2),
        name="adaln",
    )(cond_t, ada_w, ada_b.reshape(depth, 1, width))


def _head_rms(xh, gain):
    return xh * lax.rsqrt(jnp.mean(xh * xh, axis=-1, keepdims=True) + EPS) * gain


def _rope(xh, cos, sin_signed, first_half):
    partner = jnp.where(first_half, pltpu.roll(xh, HEAD_DIM - ROPE_FREQS, axis=1), pltpu.roll(xh, ROPE_FREQS, axis=1))
    return xh * cos + partner * sin_signed


def _rope_axis_tables(n_rows):
    axis_dim = HEAD_DIM // 2
    f32 = np.float32
    inv_freq = np.power(f32(ROPE_THETA), -np.arange(0, axis_dim, 2, dtype=f32) / f32(axis_dim)).astype(f32)
    ang_r = np.arange(n_rows, dtype=f32)[:, None] * inv_freq
    ang_c = np.arange(GRID_W, dtype=f32)[:, None] * inv_freq
    zr, zc = np.zeros((n_rows, axis_dim), f32), np.zeros((GRID_W, axis_dim), f32)
    return (np.concatenate([np.cos(ang_r), np.cos(ang_r), zr], axis=1),
            np.concatenate([-np.sin(ang_r), np.sin(ang_r), zr], axis=1),
            np.concatenate([zc, np.cos(ang_c), np.cos(ang_c)], axis=1),
            np.concatenate([zc, -np.sin(ang_c), np.sin(ang_c)], axis=1))


def _tile_table(row_ref, col_ref):
    col = col_ref[...]
    return jnp.concatenate([jnp.broadcast_to(row_ref[r:r + 1, :], col.shape) + col for r in range(row_ref.shape[0])],
                           axis=0)


def _halo_specs(tile, halo, n, d):
    per_tile = tile // halo
    n_halo_blocks = n // halo
    prev_spec = pl.BlockSpec((1, halo, d), lambda b, i: (b, jnp.maximum(i * per_tile - 1, 0), 0))
    main_spec = pl.BlockSpec((1, tile, d), lambda b, i: (b, i, 0))
    next_spec = pl.BlockSpec((1, halo, d), lambda b, i: (b, jnp.minimum((i + 1) * per_tile, n_halo_blocks - 1), 0))
    return prev_spec, main_spec, next_spec


def _edge_flags():
    i = pl.program_id(1)
    return (i > 0).astype(_F32), (i < pl.num_programs(1) - 1).astype(_F32)


def _in_proj_kernel(xp_ref, x_ref, xn_ref, mod_ref, gain_ref, w_ref, qg_ref, kg_ref, cw_ref,
                    rcos_ref, rsin_ref, ccos_ref, csin_ref, qt_ref, k_ref, vt_ref, conv_ref,
                    aext_ref, amain_ref, qkv_ref, cx_ref, gb_ref):
    tile = x_ref.shape[1]
    o_k = ATTN_WIDTH
    o_v = o_k + KV_WIDTH
    o_b = o_v + KV_WIDTH
    o_c = o_b + CONV_WIDTH
    every_step = pl.program_id(1) < pl.num_programs(1)

    @pl.when(every_step)
    def _():
        gain, shift, scale = _layer_row(gain_ref, EVEN_LAYER), mod_ref[0:1, :], mod_ref[1:2, :]
        prev_ok, next_ok = _edge_flags()
        a_main = _norm_modulate(x_ref[0], gain, shift, scale)
        a_prev = _norm_modulate(xp_ref[0], gain, shift, scale) * prev_ok
        a_next = _norm_modulate(xn_ref[0], gain, shift, scale) * next_ok
        aext_ref[...] = jnp.concatenate([a_prev, a_main, a_next], axis=0).astype(_BF16)
        a_mb = a_main.astype(_BF16)
        amain_ref[...] = a_mb
        qkv_ref[...] = _mm(a_mb, w_ref[:, 0:o_b])

    cx_ref[...] = _mm(aext_ref[...], w_ref[:, o_c:])
    gb_ref[...] = _mm(amain_ref[...], w_ref[:, o_b:o_c])
    cos, sin = _tile_table(rcos_ref, ccos_ref), _tile_table(rsin_ref, csin_ref)
    lane = lax.broadcasted_iota(jnp.int32, (1, HEAD_DIM), 1)
    first_half = (lane % (2 * ROPE_FREQS)) < ROPE_FREQS
    for h in range(N_Q_HEADS):
        sl = slice(h * HEAD_DIM, (h + 1) * HEAD_DIM)
        qh = _rope(_head_rms(qkv_ref[:, sl], qg_ref[...]), cos, sin, first_half)
        qt_ref[0, sl, :] = (qh * SCORE_SCALE_LOG2).T.astype(_BF16)
    for h in range(N_KV_HEADS):
        sl = slice(h * HEAD_DIM, (h + 1) * HEAD_DIM)
        kh = qkv_ref[:, o_k + h * HEAD_DIM:o_k + (h + 1) * HEAD_DIM]
        k_ref[0, :, sl] = _rope(_head_rms(kh, kg_ref[...]), cos, sin, first_half).astype(_BF16)
    vt_ref[0] = qkv_ref[:, o_v:o_b].T.astype(_BF16)

    @pl.when(every_step)
    def _():
        cx = cx_ref[...]
        u = cx[:, :CONV_WIDTH] * cx[:, CONV_WIDTH:]
        cw = cw_ref[...]
        conv = cw[0:1] * _shift_rows(u, 1) + cw[1:2] * u + cw[2:3] * _shift_rows(u, -1)
        conv_ref[0] = (gb_ref[...] * conv[HALO:HALO + tile]).astype(_BF16)


def _in_proj(x, mods, gains, w_in, q_gain, k_gain, conv_w, n_keys):
    b, n, d = x.shape
    tile = WIDE_TILE
    rows_per_tile = tile // GRID_W
    assert tile % GRID_W == 0
    prev_spec, main_spec, next_spec = _halo_specs(tile, HALO, n, d)
    rope = _rope_axis_tables(n // GRID_W)
    row_table = pl.BlockSpec((rows_per_tile, HEAD_DIM), lambda bi, i: (i, 0))
    col_table = _resident((GRID_W, HEAD_DIM), lambda bi, i: (0, 0))
    return pl.pallas_call(
        _in_proj_kernel,
        grid=(b, n // tile),
        in_specs=[
            prev_spec, main_spec, next_spec,
            _mod_spec(d, 0, lambda bi, i: bi),
            _layer_spec(gains, 0),
            _layer_spec(w_in, 0),
            _layer_spec(q_gain, 0), _layer_spec(k_gain, 0),
            _layer_spec(conv_w, 0),
            row_table, row_table, col_table, col_table,
        ],
        out_specs=[pl.BlockSpec((1, ATTN_WIDTH, tile), lambda bi, i: (bi, 0, i)),
                   pl.BlockSpec((1, tile, KV_WIDTH), lambda bi, i: (bi, i, 0)),
                   pl.BlockSpec((1, KV_WIDTH, tile), lambda bi, i: (bi, 0, i)),
                   pl.BlockSpec((1, tile, CONV_WIDTH), lambda bi, i: (bi, i, 0))],
        out_shape=[jax.ShapeDtypeStruct((b, ATTN_WIDTH, n), _BF16), jax.ShapeDtypeStruct((b, n_keys, KV_WIDTH), _BF16),
                   jax.ShapeDtypeStruct((b, KV_WIDTH, n_keys), _BF16), jax.ShapeDtypeStruct((b, n, CONV_WIDTH), _BF16)],
        scratch_shapes=[
            pltpu.VMEM((tile + 2 * HALO, d), _BF16),
            pltpu.VMEM((tile, d), _BF16),
            pltpu.VMEM((tile, ATTN_WIDTH + 2 * KV_WIDTH), _F32),
            pltpu.VMEM((tile + 2 * HALO, 2 * CONV_WIDTH), _F32),
            pltpu.VMEM((tile, CONV_WIDTH), _F32),
        ],
        compiler_params=_compiler_params(2),
        name="in_proj",
    )(x, x, x, mods, gains, w_in, q_gain, k_gain, conv_w, rope[0], rope[1], rope[2], rope[3])


def _ctx_kv_kernel(ctx_ref, mod_ref, gain_ref, w_ref, kg_ref, k_in_ref, vt_in_ref, k_ref, vt_ref):
    del k_in_ref, vt_in_ref
    a = _norm_modulate(ctx_ref[0], _layer_row(gain_ref, EVEN_LAYER), mod_ref[0:1, :], mod_ref[1:2, :]).astype(_BF16)
    kv = _mm(a, w_ref[...])
    for h in range(N_KV_HEADS):
        sl = slice(h * HEAD_DIM, (h + 1) * HEAD_DIM)
        k_ref[0, :, sl] = _head_rms(kv[:, sl], kg_ref[...]).astype(_BF16)
    vt_ref[0] = kv[:, KV_WIDTH:].T.astype(_BF16)


def _ctx_kv(ctx, mods, gains, w_in, k_gain, k_all, vt_all, n_latent):
    b, lc, d = ctx.shape
    assert n_latent % lc == 0 and 2 * KV_WIDTH == ATTN_WIDTH
    ctx_block = n_latent // lc
    return pl.pallas_call(
        _ctx_kv_kernel,
        grid=(b,),
        in_specs=[
            pl.BlockSpec((1, lc, d), lambda i: (i, 0, 0)),
            _mod_spec(d, 0, lambda i: b),
            _layer_spec(gains, 0),
            _resident((None, d, 2 * KV_WIDTH), lambda i: (0, 0, 1)),
            _layer_spec(k_gain, 0),
            pl.BlockSpec(memory_space=pl.ANY), pl.BlockSpec(memory_space=pl.ANY),
        ],
        out_specs=[pl.BlockSpec((1, lc, KV_WIDTH), lambda i: (i, ctx_block, 0)),
                   pl.BlockSpec((1, KV_WIDTH, lc), lambda i: (i, 0, ctx_block))],
        out_shape=[jax.ShapeDtypeStruct(k_all.shape, _BF16), jax.ShapeDtypeStruct(vt_all.shape, _BF16)],
        input_output_aliases={5: 0, 6: 1},
        compiler_params=_compiler_params(1),
        name="ctx_kv",
    )(ctx, mods, gains, w_in, k_gain, k_all, vt_all)


def _attn_kernel(qt_ref, k_ref, vt_ref, conv_ref, x_ref, mod_ref, wo_ref, wup_f32_ref, wdown_f32_ref,
                 o_ref, wup_bf16_ref, wdown_bf16_ref, s_ref, cmax_ref, m_ref, acc_ref, attn_ref):
    n_chunks = k_ref.shape[1] // KV_TILE
    tq = Q_TILE
    n_units = m_ref.shape[0]

    def keys_of(j):
        if isinstance(j, int):
            return slice(j * KV_TILE, (j + 1) * KV_TILE)
        return pl.ds(pl.multiple_of(j * KV_TILE, KV_TILE), KV_TILE)

    def score_stage(j, u):
        t, head = divmod(u, N_Q_HEADS)
        g = head // Q_PER_KV
        kch = k_ref[0, keys_of(j), g * HEAD_DIM:(g + 1) * HEAD_DIM]
        qt = qt_ref[0, head * HEAD_DIM:(head + 1) * HEAD_DIM, t * tq:(t + 1) * tq]
        s = jnp.dot(kch, qt, preferred_element_type=_F32)
        s_ref[u % 2] = s
        cmax_ref[u % 2] = jnp.max(s, axis=0, keepdims=True)

    def value_stage(j, u):
        g = (u % N_Q_HEADS) // Q_PER_KV
        vtch = vt_ref[0, g * HEAD_DIM:(g + 1) * HEAD_DIM, keys_of(j)]
        v_aug = jnp.concatenate([vtch, jnp.ones((ONES_ROWS, KV_TILE), _BF16)], axis=0)
        m_prev = m_ref[u]
        m_new = jnp.maximum(m_prev, cmax_ref[u % 2])
        alpha = jnp.exp2(m_prev - m_new)
        p = jnp.exp2(s_ref[u % 2] - m_new).astype(_BF16)
        acc_ref[u] = alpha * acc_ref[u] + jnp.dot(v_aug, p, preferred_element_type=_F32)
        m_ref[u] = m_new

    def chunk(j, last):
        for u in range(n_units):
            if u + 1 < n_units:
                score_stage(j, u + 1)
            elif not last:
                score_stage(j + 1, 0)
            value_stage(j, u)

    m_ref[...] = jnp.full(m_ref.shape, -jnp.inf, _F32)
    acc_ref[...] = jnp.zeros(acc_ref.shape, _F32)
    score_stage(0, 0)

    def body(j, carry):
        chunk(j, last=False)
        return carry

    lax.fori_loop(0, n_chunks - 1, body, 0, unroll=2)
    chunk(n_chunks - 1, last=True)
    for u in range(n_units):
        t, head = divmod(u, N_Q_HEADS)
        acc = acc_ref[u]
        out_t = acc[:HEAD_DIM] / acc[HEAD_DIM:HEAD_DIM + 1]
        attn_ref[t * tq:(t + 1) * tq, head * HEAD_DIM:(head + 1) * HEAD_DIM] = out_t.T.astype(_BF16)

    mixed = jnp.concatenate([attn_ref[...], conv_ref[0]], axis=-1)
    y = _mm(mixed, wo_ref[...])
    o_ref[0] = x_ref[0] + mod_ref[2:3, :] * y
    wup_bf16_ref[...] = wup_f32_ref[...].astype(_BF16)
    wdown_bf16_ref[...] = wdown_f32_ref[...].astype(_BF16)


def _attn_out(qt, k_all, vt_all, conv, x, mods, w_out, w_up, w_down, ffn_layer):
    b, n, d = x.shape
    n_keys = k_all.shape[1]
    assert n_keys % KV_TILE == 0
    tq = Q_TILE
    rows = Q_TILES_PER_STEP * tq
    assert n % rows == 0
    n_units = Q_TILES_PER_STEP * N_Q_HEADS
    grid = (b, n // rows)
    per_batch = lambda rows_, width: pl.BlockSpec((1, rows_, width), lambda bi, i: (bi, 0, 0))
    tile = lambda width: pl.BlockSpec((1, rows, width), lambda bi, i: (bi, i, 0))
    up_in, up_out, up_shape = _cast_plan(w_up, ffn_layer, grid)
    down_in, down_out, down_shape = _cast_plan(w_down, ffn_layer, grid)
    return pl.pallas_call(
        _attn_kernel,
        grid=grid,
        in_specs=[
            pl.BlockSpec((1, ATTN_WIDTH, rows), lambda bi, i: (bi, 0, i)),
            per_batch(n_keys, KV_WIDTH), per_batch(KV_WIDTH, n_keys),
            tile(CONV_WIDTH),
            tile(d),
            _mod_spec(d, 0, lambda bi, i: bi),
            _layer_spec(w_out, 0),
            up_in, down_in,
        ],
        out_specs=[tile(d), up_out, down_out],
        out_shape=[jax.ShapeDtypeStruct((b, n, d), _F32), up_shape, down_shape],
        scratch_shapes=[
            pltpu.VMEM((2, KV_TILE, tq), _F32),
            pltpu.VMEM((2, 1, tq), _F32),
            pltpu.VMEM((n_units, 1, tq), _F32),
            pltpu.VMEM((n_units, HEAD_DIM + ONES_ROWS, tq), _F32),
            pltpu.VMEM((rows, ATTN_WIDTH), _BF16),
        ],
        compiler_params=_compiler_params(2),
        name="attn_out",
    )(qt, k_all, vt_all, conv, x, mods, w_out, w_up, w_down)


def _ffn_kernel(xp_ref, x_ref, xn_ref, mod_ref, gain_ref, wup_ref, cwb_ref, wd_ref, *rest, layer):
    if len(rest) == 2:
        o_ref, h_ref = rest
    else:
        wup_f32_ref, wdown_f32_ref, o_ref, wup_bf16_ref, wdown_bf16_ref, h_ref = rest
        wup_bf16_ref[...] = wup_f32_ref[...].astype(_BF16)
        wdown_bf16_ref[...] = wdown_f32_ref[...].astype(_BF16)
    tile = x_ref.shape[1]
    gain, shift, scale = _layer_row(gain_ref, layer), mod_ref[3:4, :], mod_ref[4:5, :]
    x = x_ref[0]
    prev_ok, next_ok = _edge_flags()
    f_main = _norm_modulate(x, gain, shift, scale)
    f_prev = _norm_modulate(xp_ref[0], gain, shift, scale) * prev_ok
    f_next = _norm_modulate(xn_ref[0], gain, shift, scale) * next_ok
    f_ext = jnp.concatenate([f_prev, f_main, f_next], axis=0).astype(_BF16)
    f_mb = f_main.astype(_BF16)
    n_chunks = D_FF // FF_CHUNK

    def up(c):
        cols = slice(c * FF_CHUNK, (c + 1) * FF_CHUNK)
        val_cols = slice(D_FF + c * FF_CHUNK, D_FF + (c + 1) * FF_CHUNK)
        return _mm(f_ext, wup_ref[:, cols]), _mm(f_mb, wup_ref[:, val_cols])

    def act(c, gate, val):
        cols = slice(c * FF_CHUNK, (c + 1) * FF_CHUNK)
        cwb = cwb_ref[:, cols]
        conv = cwb[0:1] * _shift_rows(gate, 1) + cwb[1:2] * gate + cwb[2:3] * _shift_rows(gate, -1)
        h_ref[:, cols] = (_silu(conv[HALO:HALO + tile] + cwb[3:4]) * val).astype(_BF16)

    nxt = up(0)
    for c in range(n_chunks):
        cur = nxt
        if c + 1 < n_chunks:
            nxt = up(c + 1)
        act(c, *cur)
    y = _mm(h_ref[...], wd_ref[...])
    o_ref[0] = x + mod_ref[5:6, :] * y


def _ffn(x, mods, gains, w_up_bf16, conv_wb, w_down_bf16, layer, next_weights=None):
    b, n, d = x.shape
    tile = TOKEN_TILE
    grid = (b, n // tile)
    prev_spec, main_spec, next_spec = _halo_specs(tile, HALO, n, d)
    whole = lambda a: _resident(a.shape, lambda bi, i: (0, 0))
    in_specs = [
        prev_spec, main_spec, next_spec,
        _mod_spec(d, layer, lambda bi, i: bi),
        _layer_spec(gains, layer),
        whole(w_up_bf16), _layer_spec(conv_wb, layer), whole(w_down_bf16),
    ]
    args = [x, x, x, mods, gains, w_up_bf16, conv_wb, w_down_bf16]
    out_specs, out_shape = [main_spec], [jax.ShapeDtypeStruct((b, n, d), _F32)]
    if next_weights is not None:
        for w in next_weights:
            cast_in, cast_out, cast_shape = _cast_plan(w, layer + 1, grid)
            in_specs.append(cast_in)
            out_specs.append(cast_out)
            out_shape.append(cast_shape)
        args += list(next_weights)
    return pl.pallas_call(
        functools.partial(_ffn_kernel, layer=layer),
        grid=grid,
        in_specs=in_specs,
        out_specs=out_specs,
        out_shape=out_shape,
        scratch_shapes=[pltpu.VMEM((tile, D_FF), _BF16)],
        compiler_params=_compiler_params(2),
        name="ffn",
    )(*args)


def _pool_kernel(xp_ref, x_ref, xn_ref, mod_ref, gain_ref, pw_ref, ps_ref, o_ref, *, layer):
    tile = x_ref.shape[1]
    halo = xp_ref.shape[1]
    n_total = tile * pl.num_programs(1)
    gain, shift, scale = _layer_row(gain_ref, layer), mod_ref[0:1, :], mod_ref[1:2, :]
    x = x_ref[0]
    prev_ok, next_ok = _edge_flags()
    a_prev = _norm_modulate(xp_ref[0], gain, shift, scale) * prev_ok
    a_main = _norm_modulate(x, gain, shift, scale)
    a_next = _norm_modulate(xn_ref[0], gain, shift, scale) * next_ok
    a_ext = jnp.concatenate([a_prev, a_main, a_next], axis=0)
    edge = POOL_HALO // 2
    assert edge >= max(POOL_WINDOWS) // 2 and tile > 2 * edge
    rows = lax.broadcasted_iota(jnp.int32, (edge, 1), 0)
    t_top = pl.program_id(1) * tile + rows
    t_bot = t_top + (tile - edge)

    def inv_count(t, w):
        return 1.0 / (jnp.minimum(t + w - w // 2, n_total) - jnp.maximum(t - w // 2, 0)).astype(_F32)

    mixed = []
    for gi, w in enumerate(POOL_WINDOWS):
        a = a_ext[:, gi * POOL_GROUP:(gi + 1) * POOL_GROUP]
        s = a
        k = 1
        while 2 * k < w:
            s = s + _shift_rows(s, -k)
            k *= 2
        s = _shift_rows(s, k) + s
        s, a = s[halo:halo + tile], a[halo:halo + tile]
        pooled = jnp.concatenate([s[:edge] * inv_count(t_top, w) - a[:edge],
                                  s[edge:tile - edge] * (1.0 / w) - a[edge:tile - edge],
                                  s[tile - edge:] * inv_count(t_bot, w) - a[tile - edge:]], axis=0)
        mixed.append(_mm(pooled.astype(_BF16), pw_ref[gi]))
    o_ref[0] = x + jnp.concatenate(mixed, axis=-1) * (ps_ref[...] * mod_ref[2:3, :])


def _pool(x, mods, gains, pool_w, pool_scale, layer):
    b, n, d = x.shape
    tile = WIDE_TILE
    prev_spec, main_spec, next_spec = _halo_specs(tile, POOL_HALO, n, d)
    return pl.pallas_call(
        functools.partial(_pool_kernel, layer=layer),
        grid=(b, n // tile),
        in_specs=[
            prev_spec, main_spec, next_spec,
            _mod_spec(d, layer, lambda bi, i: bi),
            _layer_spec(gains, layer),
            _layer_spec(pool_w, layer // 2),
            _layer_spec(pool_scale, layer // 2),
        ],
        out_specs=main_spec,
        out_shape=jax.ShapeDtypeStruct((b, n, d), _F32),
        compiler_params=_compiler_params(2),
        name="pool",
    )(x, x, x, mods, gains, pool_w, pool_scale)


def kernel(x, c, ctx, c_ctx, ada_w, ada_b, mix_norm, ffn_norm, even_w_in, even_q_gain, even_k_gain, even_conv_w,
           even_w_out, odd_pool_w, odd_pool_scale, ffn_w_up, ffn_conv_w, ffn_conv_b, ffn_w_down):
    b, n, d = x.shape
    depth = ada_w.shape[0]
    lc = ctx.shape[1]
    assert depth == 2 and d == D_MODEL and b + 1 == N_COND
    assert n % TOKEN_TILE == 0 and n % Q_TILE == 0 and n % WIDE_TILE == 0
    assert even_q_gain.shape[0] == even_k_gain.shape[0] == odd_pool_scale.shape[0] == 1

    cond = jnp.concatenate([c, c_ctx[None, :], jnp.zeros((SUBLANES - b - 1, d), _F32)], axis=0)
    mods = _adaln(cond.T, ada_w, ada_b).reshape(depth, SUBLANES, N_MOD, d)

    w_in, w_out, pool_w, w_up, w_down = even_w_in, even_w_out, odd_pool_w, ffn_w_up, ffn_w_down
    conv_wb = jnp.concatenate([ffn_conv_w, ffn_conv_b[:, None, :],
                               jnp.zeros((depth, SUBLANES - ffn_conv_w.shape[1] - 1, D_FF), _F32)], axis=1)

    qt, k_all, vt_all, conv = _in_proj(x, mods, mix_norm, w_in, even_q_gain, even_k_gain, even_conv_w, n + lc)
    k_all, vt_all = _ctx_kv(ctx, mods, mix_norm, w_in, even_k_gain, k_all, vt_all, n)
    x, w_up0, w_down0 = _attn_out(qt, k_all, vt_all, conv, x, mods, w_out, w_up, w_down, EVEN_LAYER)
    x, w_up1, w_down1 = _ffn(x, mods, ffn_norm, w_up0, conv_wb, w_down0, EVEN_LAYER, next_weights=(w_up, w_down))

    x = _pool(x, mods, mix_norm, pool_w, odd_pool_scale, ODD_LAYER)
    (x,) = _ffn(x, mods, ffn_norm, w_up1, conv_wb, w_down1, ODD_LAYER)
    return x
```

```python
import functools

import jax
import jax.numpy as jnp
import numpy as np
from jax import lax
from jax.experimental import pallas as pl
from jax.experimental.pallas import tpu as pltpu

D_MODEL = 1024
GRID_W = 64
HEAD_DIM = 128
N_Q_HEADS = 4
N_KV_HEADS = 2
Q_PER_KV = N_Q_HEADS // N_KV_HEADS
ATTN_WIDTH = N_Q_HEADS * HEAD_DIM
KV_WIDTH = N_KV_HEADS * HEAD_DIM
CONV_WIDTH = D_MODEL - ATTN_WIDTH
IN_PROJ_WIDTH = ATTN_WIDTH + 2 * KV_WIDTH + 3 * CONV_WIDTH
ROPE_THETA = 10000.0
ROPE_FREQS = HEAD_DIM // 4
SCORE_SCALE_LOG2 = HEAD_DIM ** -0.5 * 1.4426950408889634
POOL_WINDOWS = (2, 4, 8, 16)
POOL_GROUP = D_MODEL // len(POOL_WINDOWS)
D_FF = 2816
N_MOD = 6
N_COND = 3
EVEN_LAYER, ODD_LAYER = 0, 1
EPS = 1e-6

SUBLANES = 8
VMEM_LIMIT_BYTES = 56 * 1024 * 1024

HALO = SUBLANES
POOL_HALO = 2 * SUBLANES
TOKEN_TILE = 1024
WIDE_TILE = 1024
Q_TILE = 512
Q_TILES_PER_STEP = 2
KV_TILE = 768
ONES_ROWS = 16
FF_CHUNK = 256
ADALN_COLS = 1536

_BF16 = jnp.bfloat16
_F32 = jnp.float32


def _compiler_params(n_axes):
    return pltpu.CompilerParams(dimension_semantics=("arbitrary",) * n_axes,
                                vmem_limit_bytes=VMEM_LIMIT_BYTES)


def _resident(shape, index_map):
    return pl.BlockSpec(shape, index_map, pipeline_mode=pl.Buffered(1))


def _layer_spec(arr, layer):
    if arr.ndim == 2:
        return _resident(arr.shape, lambda *_: (0, 0))
    zeros = (0,) * (arr.ndim - 1)
    return _resident((None,) + arr.shape[1:], lambda *_: (layer,) + zeros)


def _layer_row(ref, layer):
    return ref[layer:layer + 1, :]


def _mod_spec(d, layer, row_of):
    return pl.BlockSpec((None, None, N_MOD, d), lambda *ids: (layer, row_of(*ids), 0, 0))


def _cast_plan(w, layer, grid):
    rows, cols = w.shape[1:]
    n_steps = grid[0] * grid[1]
    block, shared = rows // n_steps, 1
    while block % (2 * SUBLANES):
        block, shared = 2 * block, 2 * shared
    assert rows % block == 0 and n_steps * block == rows * shared
    block_of = lambda b, i: (b * grid[1] + i) // shared
    return (pl.BlockSpec((None, block, cols), lambda b, i: (layer, block_of(b, i), 0)),
            pl.BlockSpec((block, cols), lambda b, i: (block_of(b, i), 0)),
            jax.ShapeDtypeStruct((rows, cols), _BF16))


def _norm_modulate(x, gain, shift, scale):
    y = x * lax.rsqrt(jnp.mean(x * x, axis=-1, keepdims=True) + EPS)
    return y * (gain * (1.0 + scale)) + shift


def _mm(a, w):
    return lax.dot_general(a, w, (((1,), (0,)), ((), ())), preferred_element_type=_F32)


def _silu(x):
    h = 0.5 * x
    return h + h * jnp.tanh(h)


def _shift_rows(u, k):
    return pltpu.roll(u, k % u.shape[0], axis=0)


def _adaln_kernel(ct_ref, w_ref, b_ref, o_ref):
    s = _silu(ct_ref[...])
    w = w_ref[0]
    rows = [jnp.sum(s[:, r:r + 1] * w, axis=0, keepdims=True) + b_ref[0] for r in range(N_COND)]
    rows.append(jnp.zeros((SUBLANES - N_COND, w.shape[1]), _F32))
    o_ref[0] = jnp.concatenate(rows, axis=0)


def _adaln(cond_t, ada_w, ada_b):
    depth, d, width = ada_w.shape
    return pl.pallas_call(
        _adaln_kernel,
        grid=(depth, width // ADALN_COLS),
        in_specs=[
            pl.BlockSpec((d, SUBLANES), lambda l, j: (0, 0)),
            pl.BlockSpec((1, d, ADALN_COLS), lambda l, j: (l, 0, j)),
            pl.BlockSpec((1, 1, ADALN_COLS), lambda l, j: (l, 0, j)),
        ],
        out_specs=pl.BlockSpec((1, SUBLANES, ADALN_COLS), lambda l, j: (l, 0, j)),
        out_shape=jax.ShapeDtypeStruct((depth, SUBLANES, width), _F32),
        compiler_params=_compiler_params(2),
        name="adaln",
    )(cond_t, ada_w, ada_b.reshape(depth, 1, width))


def _head_rms(xh, gain):
    return xh * lax.rsqrt(jnp.mean(xh * xh, axis=-1, keepdims=True) + EPS) * gain


def _rope(xh, cos, sin_signed, first_half):
    partner = jnp.where(first_half, pltpu.roll(xh, HEAD_DIM - ROPE_FREQS, axis=1), pltpu.roll(xh, ROPE_FREQS, axis=1))
    return xh * cos + partner * sin_signed


def _rope_axis_tables(n_rows):
    axis_dim = HEAD_DIM // 2
    f32 = np.float32
    inv_freq = np.power(f32(ROPE_THETA), -np.arange(0, axis_dim, 2, dtype=f32) / f32(axis_dim)).astype(f32)
    ang_r = np.arange(n_rows, dtype=f32)[:, None] * inv_freq
    ang_c = np.arange(GRID_W, dtype=f32)[:, None] * inv_freq
    zr, zc = np.zeros((n_rows, axis_dim), f32), np.zeros((GRID_W, axis_dim), f32)
    return (np.concatenate([np.cos(ang_r), np.cos(ang_r), zr], axis=1),
            np.concatenate([-np.sin(ang_r), np.sin(ang_r), zr], axis=1),
            np.concatenate([zc, np.cos(ang_c), np.cos(ang_c)], axis=1),
            np.concatenate([zc, -np.sin(ang_c), np.sin(ang_c)], axis=1))


def _tile_table(row_ref, col_ref):
    col = col_ref[...]
    return jnp.concatenate([jnp.broadcast_to(row_ref[r:r + 1, :], col.shape) + col for r in range(row_ref.shape[0])],
                           axis=0)


def _halo_specs(tile, halo, n, d):
    per_tile = tile // halo
    n_halo_blocks = n // halo
    prev_spec = pl.BlockSpec((1, halo, d), lambda b, i: (b, jnp.maximum(i * per_tile - 1, 0), 0))
    main_spec = pl.BlockSpec((1, tile, d), lambda b, i: (b, i, 0))
    next_spec = pl.BlockSpec((1, halo, d), lambda b, i: (b, jnp.minimum((i + 1) * per_tile, n_halo_blocks - 1), 0))
    return prev_spec, main_spec, next_spec


def _edge_flags():
    i = pl.program_id(1)
    return (i > 0).astype(_F32), (i < pl.num_programs(1) - 1).astype(_F32)


def _in_proj_kernel(xp_ref, x_ref, xn_ref, mod_ref, gain_ref, w_ref, qg_ref, kg_ref, cw_ref,
                    rcos_ref, rsin_ref, ccos_ref, csin_ref, qt_ref, k_ref, vt_ref, conv_ref,
                    aext_ref, amain_ref, qkv_ref, cx_ref, gb_ref):
    tile = x_ref.shape[1]
    o_k = ATTN_WIDTH
    o_v = o_k + KV_WIDTH
    o_b = o_v + KV_WIDTH
    o_c = o_b + CONV_WIDTH
    every_step = pl.program_id(1) < pl.num_programs(1)

    @pl.when(every_step)
    def _():
        gain, shift, scale = _layer_row(gain_ref, EVEN_LAYER), mod_ref[0:1, :], mod_ref[1:2, :]
        prev_ok, next_ok = _edge_flags()
        a_main = _norm_modulate(x_ref[0], gain, shift, scale)
        a_prev = _norm_modulate(xp_ref[0], gain, shift, scale) * prev_ok
        a_next = _norm_modulate(xn_ref[0], gain, shift, scale) * next_ok
        aext_ref[...] = jnp.concatenate([a_prev, a_main, a_next], axis=0).astype(_BF16)
        a_mb = a_main.astype(_BF16)
        amain_ref[...] = a_mb
        qkv_ref[...] = _mm(a_mb, w_ref[:, 0:o_b])

    cx_ref[...] = _mm(aext_ref[...], w_ref[:, o_c:])
    gb_ref[...] = _mm(amain_ref[...], w_ref[:, o_b:o_c])
    cos, sin = _tile_table(rcos_ref, ccos_ref), _tile_table(rsin_ref, csin_ref)
    lane = lax.broadcasted_iota(jnp.int32, (1, HEAD_DIM), 1)
    first_half = (lane % (2 * ROPE_FREQS)) < ROPE_FREQS
    for h in range(N_Q_HEADS):
        sl = slice(h * HEAD_DIM, (h + 1) * HEAD_DIM)
        qh = _rope(_head_rms(qkv_ref[:, sl], qg_ref[...]), cos, sin, first_half)
        qt_ref[0, sl, :] = (qh * SCORE_SCALE_LOG2).T.astype(_BF16)
    for h in range(N_KV_HEADS):
        sl = slice(h * HEAD_DIM, (h + 1) * HEAD_DIM)
        kh = qkv_ref[:, o_k + h * HEAD_DIM:o_k + (h + 1) * HEAD_DIM]
        k_ref[0, :, sl] = _rope(_head_rms(kh, kg_ref[...]), cos, sin, first_half).astype(_BF16)
    vt_ref[0] = qkv_ref[:, o_v:o_b].T.astype(_BF16)

    @pl.when(every_step)
    def _():
        cx = cx_ref[...]
        u = cx[:, :CONV_WIDTH] * cx[:, CONV_WIDTH:]
        cw = cw_ref[...]
        conv = cw[0:1] * _shift_rows(u, 1) + cw[1:2] * u + cw[2:3] * _shift_rows(u, -1)
        conv_ref[0] = (gb_ref[...] * conv[HALO:HALO + tile]).astype(_BF16)


def _in_proj(x, mods, gains, w_in, q_gain, k_gain, conv_w, n_keys):
    b, n, d = x.shape
    tile = WIDE_TILE
    rows_per_tile = tile // GRID_W
    assert tile % GRID_W == 0
    prev_spec, main_spec, next_spec = _halo_specs(tile, HALO, n, d)
    rope = _rope_axis_tables(n // GRID_W)
    row_table = pl.BlockSpec((rows_per_tile, HEAD_DIM), lambda bi, i: (i, 0))
    col_table = _resident((GRID_W, HEAD_DIM), lambda bi, i: (0, 0))
    return pl.pallas_call(
        _in_proj_kernel,
        grid=(b, n // tile),
        in_specs=[
            prev_spec, main_spec, next_spec,
            _mod_spec(d, 0, lambda bi, i: bi),
            _layer_spec(gains, 0),
            _layer_spec(w_in, 0),
            _layer_spec(q_gain, 0), _layer_spec(k_gain, 0),
            _layer_spec(conv_w, 0),
            row_table, row_table, col_table, col_table,
        ],
        out_specs=[pl.BlockSpec((1, ATTN_WIDTH, tile), lambda bi, i: (bi, 0, i)),
                   pl.BlockSpec((1, tile, KV_WIDTH), lambda bi, i: (bi, i, 0)),
                   pl.BlockSpec((1, KV_WIDTH, tile), lambda bi, i: (bi, 0, i)),
                   pl.BlockSpec((1, tile, CONV_WIDTH), lambda bi, i: (bi, i, 0))],
        out_shape=[jax.ShapeDtypeStruct((b, ATTN_WIDTH, n), _BF16), jax.ShapeDtypeStruct((b, n_keys, KV_WIDTH), _BF16),
                   jax.ShapeDtypeStruct((b, KV_WIDTH, n_keys), _BF16), jax.ShapeDtypeStruct((b, n, CONV_WIDTH), _BF16)],
        scratch_shapes=[
            pltpu.VMEM((tile + 2 * HALO, d), _BF16),
            pltpu.VMEM((tile, d), _BF16),
            pltpu.VMEM((tile, ATTN_WIDTH + 2 * KV_WIDTH), _F32),
            pltpu.VMEM((tile + 2 * HALO, 2 * CONV_WIDTH), _F32),
            pltpu.VMEM((tile, CONV_WIDTH), _F32),
        ],
        compiler_params=_compiler_params(2),
        name="in_proj",
    )(x, x, x, mods, gains, w_in, q_gain, k_gain, conv_w, rope[0], rope[1], rope[2], rope[3])


def _ctx_kv_kernel(ctx_ref, mod_ref, gain_ref, w_ref, kg_ref, k_in_ref, vt_in_ref, k_ref, vt_ref):
    del k_in_ref, vt_in_ref
    a = _norm_modulate(ctx_ref[0], _layer_row(gain_ref, EVEN_LAYER), mod_ref[0:1, :], mod_ref[1:2, :]).astype(_BF16)
    kv = _mm(a, w_ref[...])
    for h in range(N_KV_HEADS):
        sl = slice(h * HEAD_DIM, (h + 1) * HEAD_DIM)
        k_ref[0, :, sl] = _head_rms(kv[:, sl], kg_ref[...]).astype(_BF16)
    vt_ref[0] = kv[:, KV_WIDTH:].T.astype(_BF16)


def _ctx_kv(ctx, mods, gains, w_in, k_gain, k_all, vt_all, n_latent):
    b, lc, d = ctx.shape
    assert n_latent % lc == 0 and 2 * KV_WIDTH == ATTN_WIDTH
    ctx_block = n_latent // lc
    return pl.pallas_call(
        _ctx_kv_kernel,
        grid=(b,),
        in_specs=[
            pl.BlockSpec((1, lc, d), lambda i: (i, 0, 0)),
            _mod_spec(d, 0, lambda i: b),
            _layer_spec(gains, 0),
            _resident((None, d, 2 * KV_WIDTH), lambda i: (0, 0, 1)),
            _layer_spec(k_gain, 0),
            pl.BlockSpec(memory_space=pl.ANY), pl.BlockSpec(memory_space=pl.ANY),
        ],
        out_specs=[pl.BlockSpec((1, lc, KV_WIDTH), lambda i: (i, ctx_block, 0)),
                   pl.BlockSpec((1, KV_WIDTH, lc), lambda i: (i, 0, ctx_block))],
        out_shape=[jax.ShapeDtypeStruct(k_all.shape, _BF16), jax.ShapeDtypeStruct(vt_all.shape, _BF16)],
        input_output_aliases={5: 0, 6: 1},
        compiler_params=_compiler_params(1),
        name="ctx_kv",
    )(ctx, mods, gains, w_in, k_gain, k_all, vt_all)


def _attn_kernel(qt_ref, k_ref, vt_ref, conv_ref, x_ref, mod_ref, wo_ref, wup_f32_ref, wdown_f32_ref,
                 o_ref, wup_bf16_ref, wdown_bf16_ref, s_ref, cmax_ref, m_ref, acc_ref, attn_ref):
    n_chunks = k_ref.shape[1] // KV_TILE
    tq = Q_TILE
    n_units = m_ref.shape[0]

    def keys_of(j):
        if isinstance(j, int):
            return slice(j * KV_TILE, (j + 1) * KV_TILE)
        return pl.ds(pl.multiple_of(j * KV_TILE, KV_TILE), KV_TILE)

    def score_stage(j, u):
        t, head = divmod(u, N_Q_HEADS)
        g = head // Q_PER_KV
        kch = k_ref[0, keys_of(j), g * HEAD_DIM:(g + 1) * HEAD_DIM]
        qt = qt_ref[0, head * HEAD_DIM:(head + 1) * HEAD_DIM, t * tq:(t + 1) * tq]
        s = jnp.dot(kch, qt, preferred_element_type=_F32)
        s_ref[u % 2] = s
        cmax_ref[u % 2] = jnp.max(s, axis=0, keepdims=True)

    def value_stage(j, u):
        g = (u % N_Q_HEADS) // Q_PER_KV
        vtch = vt_ref[0, g * HEAD_DIM:(g + 1) * HEAD_DIM, keys_of(j)]
        v_aug = jnp.concatenate([vtch, jnp.ones((ONES_ROWS, KV_TILE), _BF16)], axis=0)
        m_prev = m_ref[u]
        m_new = jnp.maximum(m_prev, cmax_ref[u % 2])
        alpha = jnp.exp2(m_prev - m_new)
        p = jnp.exp2(s_ref[u % 2] - m_new).astype(_BF16)
        acc_ref[u] = alpha * acc_ref[u] + jnp.dot(v_aug, p, preferred_element_type=_F32)
        m_ref[u] = m_new

    def chunk(j, last):
        for u in range(n_units):
            if u + 1 < n_units:
                score_stage(j, u + 1)
            elif not last:
                score_stage(j + 1, 0)
            value_stage(j, u)

    m_ref[...] = jnp.full(m_ref.shape, -jnp.inf, _F32)
    acc_ref[...] = jnp.zeros(acc_ref.shape, _F32)
    score_stage(0, 0)

    def body(j, carry):
        chunk(j, last=False)
        return carry

    lax.fori_loop(0, n_chunks - 1, body, 0, unroll=5)
    chunk(n_chunks - 1, last=True)
    for u in range(n_units):
        t, head = divmod(u, N_Q_HEADS)
        acc = acc_ref[u]
        out_t = acc[:HEAD_DIM] / acc[HEAD_DIM:HEAD_DIM + 1]
        attn_ref[t * tq:(t + 1) * tq, head * HEAD_DIM:(head + 1) * HEAD_DIM] = out_t.T.astype(_BF16)

    mixed = jnp.concatenate([attn_ref[...], conv_ref[0]], axis=-1)
    y = _mm(mixed, wo_ref[...])
    o_ref[0] = x_ref[0] + mod_ref[2:3, :] * y
    wup_bf16_ref[...] = wup_f32_ref[...].astype(_BF16)
    wdown_bf16_ref[...] = wdown_f32_ref[...].astype(_BF16)


def _attn_out(qt, k_all, vt_all, conv, x, mods, w_out, w_up, w_down, ffn_layer):
    b, n, d = x.shape
    n_keys = k_all.shape[1]
    assert n_keys % KV_TILE == 0
    tq = Q_TILE
    rows = Q_TILES_PER_STEP * tq
    assert n % rows == 0
    n_units = Q_TILES_PER_STEP * N_Q_HEADS
    grid = (b, n // rows)
    per_batch = lambda rows_, width: pl.BlockSpec((1, rows_, width), lambda bi, i: (bi, 0, 0))
    tile = lambda width: pl.BlockSpec((1, rows, width), lambda bi, i: (bi, i, 0))
    up_in, up_out, up_shape = _cast_plan(w_up, ffn_layer, grid)
    down_in, down_out, down_shape = _cast_plan(w_down, ffn_layer, grid)
    return pl.pallas_call(
        _attn_kernel,
        grid=grid,
        in_specs=[
            pl.BlockSpec((1, ATTN_WIDTH, rows), lambda bi, i: (bi, 0, i)),
            per_batch(n_keys, KV_WIDTH), per_batch(KV_WIDTH, n_keys),
            tile(CONV_WIDTH),
            tile(d),
            _mod_spec(d, 0, lambda bi, i: bi),
            _layer_spec(w_out, 0),
            up_in, down_in,
        ],
        out_specs=[tile(d), up_out, down_out],
        out_shape=[jax.ShapeDtypeStruct((b, n, d), _F32), up_shape, down_shape],
        scratch_shapes=[
            pltpu.VMEM((2, KV_TILE, tq), _F32),
            pltpu.VMEM((2, 1, tq), _F32),
            pltpu.VMEM((n_units, 1, tq), _F32),
            pltpu.VMEM((n_units, HEAD_DIM + ONES_ROWS, tq), _F32),
            pltpu.VMEM((rows, ATTN_WIDTH), _BF16),
        ],
        compiler_params=_compiler_params(2),
        name="attn_out",
    )(qt, k_all, vt_all, conv, x, mods, w_out, w_up, w_down)


def _ffn_kernel(xp_ref, x_ref, xn_ref, mod_ref, gain_ref, wup_ref, cwb_ref, wd_ref, *rest, layer):
    if len(rest) == 2:
        o_ref, h_ref = rest
    else:
        wup_f32_ref, wdown_f32_ref, o_ref, wup_bf16_ref, wdown_bf16_ref, h_ref = rest
        wup_bf16_ref[...] = wup_f32_ref[...].astype(_BF16)
        wdown_bf16_ref[...] = wdown_f32_ref[...].astype(_BF16)
    tile = x_ref.shape[1]
    gain, shift, scale = _layer_row(gain_ref, layer), mod_ref[3:4, :], mod_ref[4:5, :]
    x = x_ref[0]
    prev_ok, next_ok = _edge_flags()
    f_main = _norm_modulate(x, gain, shift, scale)
    f_prev = _norm_modulate(xp_ref[0], gain, shift, scale) * prev_ok
    f_next = _norm_modulate(xn_ref[0], gain, shift, scale) * next_ok
    f_ext = jnp.concatenate([f_prev, f_main, f_next], axis=0).astype(_BF16)
    f_mb = f_main.astype(_BF16)
    n_chunks = D_FF // FF_CHUNK

    def up(c):
        cols = slice(c * FF_CHUNK, (c + 1) * FF_CHUNK)
        val_cols = slice(D_FF + c * FF_CHUNK, D_FF + (c + 1) * FF_CHUNK)
        return _mm(f_ext, wup_ref[:, cols]), _mm(f_mb, wup_ref[:, val_cols])

    def act(c, gate, val):
        cols = slice(c * FF_CHUNK, (c + 1) * FF_CHUNK)
        cwb = cwb_ref[:, cols]
        conv = cwb[0:1] * _shift_rows(gate, 1) + cwb[1:2] * gate + cwb[2:3] * _shift_rows(gate, -1)
        h_ref[:, cols] = (_silu(conv[HALO:HALO + tile] + cwb[3:4]) * val).astype(_BF16)

    nxt = up(0)
    for c in range(n_chunks):
        cur = nxt
        if c + 1 < n_chunks:
            nxt = up(c + 1)
        act(c, *cur)
    y = _mm(h_ref[...], wd_ref[...])
    o_ref[0] = x + mod_ref[5:6, :] * y


def _ffn(x, mods, gains, w_up_bf16, conv_wb, w_down_bf16, layer, next_weights=None):
    b, n, d = x.shape
    tile = TOKEN_TILE
    grid = (b, n // tile)
    prev_spec, main_spec, next_spec = _halo_specs(tile, HALO, n, d)
    whole = lambda a: _resident(a.shape, lambda bi, i: (0, 0))
    in_specs = [
        prev_spec, main_spec, next_spec,
        _mod_spec(d, layer, lambda bi, i: bi),
        _layer_spec(gains, layer),
        whole(w_up_bf16), _layer_spec(conv_wb, layer), whole(w_down_bf16),
    ]
    args = [x, x, x, mods, gains, w_up_bf16, conv_wb, w_down_bf16]
    out_specs, out_shape = [main_spec], [jax.ShapeDtypeStruct((b, n, d), _F32)]
    if next_weights is not None:
        for w in next_weights:
            cast_in, cast_out, cast_shape = _cast_plan(w, layer + 1, grid)
            in_specs.append(cast_in)
            out_specs.append(cast_out)
            out_shape.append(cast_shape)
        args += list(next_weights)
    return pl.pallas_call(
        functools.partial(_ffn_kernel, layer=layer),
        grid=grid,
        in_specs=in_specs,
        out_specs=out_specs,
        out_shape=out_shape,
        scratch_shapes=[pltpu.VMEM((tile, D_FF), _BF16)],
        compiler_params=_compiler_params(2),
        name="ffn",
    )(*args)


def _pool_kernel(xp_ref, x_ref, xn_ref, mod_ref, gain_ref, pw_ref, ps_ref, o_ref, *, layer):
    tile = x_ref.shape[1]
    halo = xp_ref.shape[1]
    n_total = tile * pl.num_programs(1)
    gain, shift, scale = _layer_row(gain_ref, layer), mod_ref[0:1, :], mod_ref[1:2, :]
    x = x_ref[0]
    prev_ok, next_ok = _edge_flags()
    a_prev = _norm_modulate(xp_ref[0], gain, shift, scale) * prev_ok
    a_main = _norm_modulate(x, gain, shift, scale)
    a_next = _norm_modulate(xn_ref[0], gain, shift, scale) * next_ok
    a_ext = jnp.concatenate([a_prev, a_main, a_next], axis=0)
    edge = POOL_HALO // 2
    assert edge >= max(POOL_WINDOWS) // 2 and tile > 2 * edge
    rows = lax.broadcasted_iota(jnp.int32, (edge, 1), 0)
    t_top = pl.program_id(1) * tile + rows
    t_bot = t_top + (tile - edge)

    def inv_count(t, w):
        return 1.0 / (jnp.minimum(t + w - w // 2, n_total) - jnp.maximum(t - w // 2, 0)).astype(_F32)

    mixed = []
    for gi, w in enumerate(POOL_WINDOWS):
        a = a_ext[:, gi * POOL_GROUP:(gi + 1) * POOL_GROUP]
        s = a
        k = 1
        while 2 * k < w:
            s = s + _shift_rows(s, -k)
            k *= 2
        s = _shift_rows(s, k) + s
        s, a = s[halo:halo + tile], a[halo:halo + tile]
        pooled = jnp.concatenate([s[:edge] * inv_count(t_top, w) - a[:edge],
                                  s[edge:tile - edge] * (1.0 / w) - a[edge:tile - edge],
                                  s[tile - edge:] * inv_count(t_bot, w) - a[tile - edge:]], axis=0)
        mixed.append(_mm(pooled.astype(_BF16), pw_ref[gi]))
    o_ref[0] = x + jnp.concatenate(mixed, axis=-1) * (ps_ref[...] * mod_ref[2:3, :])


def _pool(x, mods, gains, pool_w, pool_scale, layer):
    b, n, d = x.shape
    tile = WIDE_TILE
    prev_spec, main_spec, next_spec = _halo_specs(tile, POOL_HALO, n, d)
    return pl.pallas_call(
        functools.partial(_pool_kernel, layer=layer),
        grid=(b, n // tile),
        in_specs=[
            prev_spec, main_spec, next_spec,
            _mod_spec(d, layer, lambda bi, i: bi),
            _layer_spec(gains, layer),
            _layer_spec(pool_w, layer // 2),
            _layer_spec(pool_scale, layer // 2),
        ],
        out_specs=main_spec,
        out_shape=jax.ShapeDtypeStruct((b, n, d), _F32),
        compiler_params=_compiler_params(2),
        name="pool",
    )(x, x, x, mods, gains, pool_w, pool_scale)


def kernel(x, c, ctx, c_ctx, ada_w, ada_b, mix_norm, ffn_norm, even_w_in, even_q_gain, even_k_gain, even_conv_w,
           even_w_out, odd_pool_w, odd_pool_scale, ffn_w_up, ffn_conv_w, ffn_conv_b, ffn_w_down):
    b, n, d = x.shape
    depth = ada_w.shape[0]
    lc = ctx.shape[1]
    assert depth == 2 and d == D_MODEL and b + 1 == N_COND
    assert n % TOKEN_TILE == 0 and n % Q_TILE == 0 and n % WIDE_TILE == 0
    assert even_q_gain.shape[0] == even_k_gain.shape[0] == odd_pool_scale.shape[0] == 1

    cond = jnp.concatenate([c, c_ctx[None, :], jnp.zeros((SUBLANES - b - 1, d), _F32)], axis=0)
    mods = _adaln(cond.T, ada_w, ada_b).reshape(depth, SUBLANES, N_MOD, d)

    w_in, w_out, pool_w, w_up, w_down = even_w_in, even_w_out, odd_pool_w, ffn_w_up, ffn_w_down
    conv_wb = jnp.concatenate([ffn_conv_w, ffn_conv_b[:, None, :],
                               jnp.zeros((depth, SUBLANES - ffn_conv_w.shape[1] - 1, D_FF), _F32)], axis=1)

    qt, k_all, vt_all, conv = _in_proj(x, mods, mix_norm, w_in, even_q_gain, even_k_gain, even_conv_w, n + lc)
    k_all, vt_all = _ctx_kv(ctx, mods, mix_norm, w_in, even_k_gain, k_all, vt_all, n)
    x, w_up0, w_down0 = _attn_out(qt, k_all, vt_all, conv, x, mods, w_out, w_up, w_down, EVEN_LAYER)
    x, w_up1, w_down1 = _ffn(x, mods, ffn_norm, w_up0, conv_wb, w_down0, EVEN_LAYER, next_weights=(w_up, w_down))

    x = _pool(x, mods, mix_norm, pool_w, odd_pool_scale, ODD_LAYER)
    (x,) = _ffn(x, mods, ffn_norm, w_up1, conv_wb, w_down1, ODD_LAYER)
    return x
```

```python
import functools

import jax
import jax.numpy as jnp
import numpy as np
from jax import lax
from jax.experimental import pallas as pl
from jax.experimental.pallas import tpu as pltpu

D_MODEL = 1024
GRID_W = 64
HEAD_DIM = 128
N_Q_HEADS = 4
N_KV_HEADS = 2
Q_PER_KV = N_Q_HEADS // N_KV_HEADS
ATTN_WIDTH = N_Q_HEADS * HEAD_DIM
KV_WIDTH = N_KV_HEADS * HEAD_DIM
CONV_WIDTH = D_MODEL - ATTN_WIDTH
IN_PROJ_WIDTH = ATTN_WIDTH + 2 * KV_WIDTH + 3 * CONV_WIDTH
ROPE_THETA = 10000.0
ROPE_FREQS = HEAD_DIM // 4
SCORE_SCALE_LOG2 = HEAD_DIM ** -0.5 * 1.4426950408889634
POOL_WINDOWS = (2, 4, 8, 16)
POOL_GROUP = D_MODEL // len(POOL_WINDOWS)
D_FF = 2816
N_MOD = 6
N_COND = 3
EVEN_LAYER, ODD_LAYER = 0, 1
EPS = 1e-6

SUBLANES = 8
VMEM_LIMIT_BYTES = 56 * 1024 * 1024

HALO = SUBLANES
POOL_HALO = 2 * SUBLANES
TOKEN_TILE = 1024
WIDE_TILE = 1024
POOL_TILE = 2048
Q_TILE = 512
Q_TILES_PER_STEP = 2
KV_TILE = 768
ONES_ROWS = 16
FF_CHUNK = 256
ADALN_COLS = 1536

_BF16 = jnp.bfloat16
_F32 = jnp.float32


def _compiler_params(n_axes):
    return pltpu.CompilerParams(dimension_semantics=("arbitrary",) * n_axes,
                                vmem_limit_bytes=VMEM_LIMIT_BYTES)


def _resident(shape, index_map):
    return pl.BlockSpec(shape, index_map, pipeline_mode=pl.Buffered(1))


def _layer_spec(arr, layer):
    if arr.ndim == 2:
        return _resident(arr.shape, lambda *_: (0, 0))
    zeros = (0,) * (arr.ndim - 1)
    return _resident((None,) + arr.shape[1:], lambda *_: (layer,) + zeros)


def _layer_row(ref, layer):
    return ref[layer:layer + 1, :]


def _mod_spec(d, layer, row_of):
    return pl.BlockSpec((None, None, N_MOD, d), lambda *ids: (layer, row_of(*ids), 0, 0))


def _cast_plan(w, layer, grid):
    rows, cols = w.shape[1:]
    n_steps = grid[0] * grid[1]
    block, shared = rows // n_steps, 1
    while block % (2 * SUBLANES):
        block, shared = 2 * block, 2 * shared
    assert rows % block == 0 and n_steps * block == rows * shared
    block_of = lambda b, i: (b * grid[1] + i) // shared
    return (pl.BlockSpec((None, block, cols), lambda b, i: (layer, block_of(b, i), 0)),
            pl.BlockSpec((block, cols), lambda b, i: (block_of(b, i), 0)),
            jax.ShapeDtypeStruct((rows, cols), _BF16))


def _norm_modulate(x, gain, shift, scale):
    y = x * lax.rsqrt(jnp.mean(x * x, axis=-1, keepdims=True) + EPS)
    return y * (gain * (1.0 + scale)) + shift


def _mm(a, w):
    return lax.dot_general(a, w, (((1,), (0,)), ((), ())), preferred_element_type=_F32)


def _silu(x):
    h = 0.5 * x
    return h + h * jnp.tanh(h)


def _shift_rows(u, k):
    return pltpu.roll(u, k % u.shape[0], axis=0)


def _adaln_kernel(ct_ref, w_ref, b_ref, o_ref):
    s = _silu(ct_ref[...])
    w = w_ref[0]
    rows = [jnp.sum(s[:, r:r + 1] * w, axis=0, keepdims=True) + b_ref[0] for r in range(N_COND)]
    rows.append(jnp.zeros((SUBLANES - N_COND, w.shape[1]), _F32))
    o_ref[0] = jnp.concatenate(rows, axis=0)


def _adaln(cond_t, ada_w, ada_b):
    depth, d, width = ada_w.shape
    return pl.pallas_call(
        _adaln_kernel,
        grid=(depth, width // ADALN_COLS),
        in_specs=[
            pl.BlockSpec((d, SUBLANES), lambda l, j: (0, 0)),
            pl.BlockSpec((1, d, ADALN_COLS), lambda l, j: (l, 0, j)),
            pl.BlockSpec((1, 1, ADALN_COLS), lambda l, j: (l, 0, j)),
        ],
        out_specs=pl.BlockSpec((1, SUBLANES, ADALN_COLS), lambda l, j: (l, 0, j)),
        out_shape=jax.ShapeDtypeStruct((depth, SUBLANES, width), _F32),
        compiler_params=_compiler_params(2),
        name="adaln",
    )(cond_t, ada_w, ada_b.reshape(depth, 1, width))


def _head_rms(xh, gain):
    return xh * lax.rsqrt(jnp.mean(xh * xh, axis=-1, keepdims=True) + EPS) * gain


def _rope(xh, cos, sin_signed, first_half):
    partner = jnp.where(first_half, pltpu.roll(xh, HEAD_DIM - ROPE_FREQS, axis=1), pltpu.roll(xh, ROPE_FREQS, axis=1))
    return xh * cos + partner * sin_signed


def _rope_axis_tables(n_rows):
    axis_dim = HEAD_DIM // 2
    f32 = np.float32
    inv_freq = np.power(f32(ROPE_THETA), -np.arange(0, axis_dim, 2, dtype=f32) / f32(axis_dim)).astype(f32)
    ang_r = np.arange(n_rows, dtype=f32)[:, None] * inv_freq
    ang_c = np.arange(GRID_W, dtype=f32)[:, None] * inv_freq
    zr, zc = np.zeros((n_rows, axis_dim), f32), np.zeros((GRID_W, axis_dim), f32)
    return (np.concatenate([np.cos(ang_r), np.cos(ang_r), zr], axis=1),
            np.concatenate([-np.sin(ang_r), np.sin(ang_r), zr], axis=1),
            np.concatenate([zc, np.cos(ang_c), np.cos(ang_c)], axis=1),
            np.concatenate([zc, -np.sin(ang_c), np.sin(ang_c)], axis=1))


def _tile_table(row_ref, col_ref):
    col = col_ref[...]
    return jnp.concatenate([jnp.broadcast_to(row_ref[r:r + 1, :], col.shape) + col for r in range(row_ref.shape[0])],
                           axis=0)


def _halo_specs(tile, halo, n, d):
    per_tile = tile // halo
    n_halo_blocks = n // halo
    prev_spec = pl.BlockSpec((1, halo, d), lambda b, i: (b, jnp.maximum(i * per_tile - 1, 0), 0))
    main_spec = pl.BlockSpec((1, tile, d), lambda b, i: (b, i, 0))
    next_spec = pl.BlockSpec((1, halo, d), lambda b, i: (b, jnp.minimum((i + 1) * per_tile, n_halo_blocks - 1), 0))
    return prev_spec, main_spec, next_spec


def _edge_flags():
    i = pl.program_id(1)
    return (i > 0).astype(_F32), (i < pl.num_programs(1) - 1).astype(_F32)


def _in_proj_kernel(xp_ref, x_ref, xn_ref, mod_ref, gain_ref, w_ref, qg_ref, kg_ref, cw_ref,
                    rcos_ref, rsin_ref, ccos_ref, csin_ref, qt_ref, k_ref, vt_ref, conv_ref,
                    aext_ref, amain_ref, qkv_ref, cx_ref, gb_ref):
    tile = x_ref.shape[1]
    o_k = ATTN_WIDTH
    o_v = o_k + KV_WIDTH
    o_b = o_v + KV_WIDTH
    o_c = o_b + CONV_WIDTH
    every_step = pl.program_id(1) < pl.num_programs(1)

    @pl.when(every_step)
    def _():
        gain, shift, scale = _layer_row(gain_ref, EVEN_LAYER), mod_ref[0:1, :], mod_ref[1:2, :]
        prev_ok, next_ok = _edge_flags()
        a_main = _norm_modulate(x_ref[0], gain, shift, scale)
        a_prev = _norm_modulate(xp_ref[0], gain, shift, scale) * prev_ok
        a_next = _norm_modulate(xn_ref[0], gain, shift, scale) * next_ok
        aext_ref[...] = jnp.concatenate([a_prev, a_main, a_next], axis=0).astype(_BF16)
        a_mb = a_main.astype(_BF16)
        amain_ref[...] = a_mb
        qkv_ref[...] = _mm(a_mb, w_ref[:, 0:o_b])

    cx_ref[...] = _mm(aext_ref[...], w_ref[:, o_c:])
    gb_ref[...] = _mm(amain_ref[...], w_ref[:, o_b:o_c])
    cos, sin = _tile_table(rcos_ref, ccos_ref), _tile_table(rsin_ref, csin_ref)
    lane = lax.broadcasted_iota(jnp.int32, (1, HEAD_DIM), 1)
    first_half = (lane % (2 * ROPE_FREQS)) < ROPE_FREQS
    for h in range(N_Q_HEADS):
        sl = slice(h * HEAD_DIM, (h + 1) * HEAD_DIM)
        qh = _rope(_head_rms(qkv_ref[:, sl], qg_ref[...]), cos, sin, first_half)
        qt_ref[0, sl, :] = (qh * SCORE_SCALE_LOG2).T.astype(_BF16)
    for h in range(N_KV_HEADS):
        sl = slice(h * HEAD_DIM, (h + 1) * HEAD_DIM)
        kh = qkv_ref[:, o_k + h * HEAD_DIM:o_k + (h + 1) * HEAD_DIM]
        k_ref[0, :, sl] = _rope(_head_rms(kh, kg_ref[...]), cos, sin, first_half).astype(_BF16)
    vt_ref[0] = qkv_ref[:, o_v:o_b].T.astype(_BF16)

    @pl.when(every_step)
    def _():
        cx = cx_ref[...]
        u = cx[:, :CONV_WIDTH] * cx[:, CONV_WIDTH:]
        cw = cw_ref[...]
        conv = cw[0:1] * _shift_rows(u, 1) + cw[1:2] * u + cw[2:3] * _shift_rows(u, -1)
        conv_ref[0] = (gb_ref[...] * conv[HALO:HALO + tile]).astype(_BF16)


def _in_proj(x, mods, gains, w_in, q_gain, k_gain, conv_w, n_keys):
    b, n, d = x.shape
    tile = WIDE_TILE
    rows_per_tile = tile // GRID_W
    assert tile % GRID_W == 0
    prev_spec, main_spec, next_spec = _halo_specs(tile, HALO, n, d)
    rope = _rope_axis_tables(n // GRID_W)
    row_table = pl.BlockSpec((rows_per_tile, HEAD_DIM), lambda bi, i: (i, 0))
    col_table = _resident((GRID_W, HEAD_DIM), lambda bi, i: (0, 0))
    return pl.pallas_call(
        _in_proj_kernel,
        grid=(b, n // tile),
        in_specs=[
            prev_spec, main_spec, next_spec,
            _mod_spec(d, 0, lambda bi, i: bi),
            _layer_spec(gains, 0),
            _layer_spec(w_in, 0),
            _layer_spec(q_gain, 0), _layer_spec(k_gain, 0),
            _layer_spec(conv_w, 0),
            row_table, row_table, col_table, col_table,
        ],
        out_specs=[pl.BlockSpec((1, ATTN_WIDTH, tile), lambda bi, i: (bi, 0, i)),
                   pl.BlockSpec((1, tile, KV_WIDTH), lambda bi, i: (bi, i, 0)),
                   pl.BlockSpec((1, KV_WIDTH, tile), lambda bi, i: (bi, 0, i)),
                   pl.BlockSpec((1, tile, CONV_WIDTH), lambda bi, i: (bi, i, 0))],
        out_shape=[jax.ShapeDtypeStruct((b, ATTN_WIDTH, n), _BF16), jax.ShapeDtypeStruct((b, n_keys, KV_WIDTH), _BF16),
                   jax.ShapeDtypeStruct((b, KV_WIDTH, n_keys), _BF16), jax.ShapeDtypeStruct((b, n, CONV_WIDTH), _BF16)],
        scratch_shapes=[
            pltpu.VMEM((tile + 2 * HALO, d), _BF16),
            pltpu.VMEM((tile, d), _BF16),
            pltpu.VMEM((tile, ATTN_WIDTH + 2 * KV_WIDTH), _F32),
            pltpu.VMEM((tile + 2 * HALO, 2 * CONV_WIDTH), _F32),
            pltpu.VMEM((tile, CONV_WIDTH), _F32),
        ],
        compiler_params=_compiler_params(2),
        name="in_proj",
    )(x, x, x, mods, gains, w_in, q_gain, k_gain, conv_w, rope[0], rope[1], rope[2], rope[3])


def _ctx_kv_kernel(ctx_ref, mod_ref, gain_ref, w_ref, kg_ref, k_in_ref, vt_in_ref, k_ref, vt_ref):
    del k_in_ref, vt_in_ref
    a = _norm_modulate(ctx_ref[0], _layer_row(gain_ref, EVEN_LAYER), mod_ref[0:1, :], mod_ref[1:2, :]).astype(_BF16)
    kv = _mm(a, w_ref[...])
    for h in range(N_KV_HEADS):
        sl = slice(h * HEAD_DIM, (h + 1) * HEAD_DIM)
        k_ref[0, :, sl] = _head_rms(kv[:, sl], kg_ref[...]).astype(_BF16)
    vt_ref[0] = kv[:, KV_WIDTH:].T.astype(_BF16)


def _ctx_kv(ctx, mods, gains, w_in, k_gain, k_all, vt_all, n_latent):
    b, lc, d = ctx.shape
    assert n_latent % lc == 0 and 2 * KV_WIDTH == ATTN_WIDTH
    ctx_block = n_latent // lc
    return pl.pallas_call(
        _ctx_kv_kernel,
        grid=(b,),
        in_specs=[
            pl.BlockSpec((1, lc, d), lambda i: (i, 0, 0)),
            _mod_spec(d, 0, lambda i: b),
            _layer_spec(gains, 0),
            _resident((None, d, 2 * KV_WIDTH), lambda i: (0, 0, 1)),
            _layer_spec(k_gain, 0),
            pl.BlockSpec(memory_space=pl.ANY), pl.BlockSpec(memory_space=pl.ANY),
        ],
        out_specs=[pl.BlockSpec((1, lc, KV_WIDTH), lambda i: (i, ctx_block, 0)),
                   pl.BlockSpec((1, KV_WIDTH, lc), lambda i: (i, 0, ctx_block))],
        out_shape=[jax.ShapeDtypeStruct(k_all.shape, _BF16), jax.ShapeDtypeStruct(vt_all.shape, _BF16)],
        input_output_aliases={5: 0, 6: 1},
        compiler_params=_compiler_params(1),
        name="ctx_kv",
    )(ctx, mods, gains, w_in, k_gain, k_all, vt_all)


def _attn_kernel(qt_ref, k_ref, vt_ref, conv_ref, x_ref, mod_ref, wo_ref, wup_f32_ref, wdown_f32_ref,
                 o_ref, wup_bf16_ref, wdown_bf16_ref, s_ref, cmax_ref, m_ref, acc_ref, attn_ref):
    n_chunks = k_ref.shape[1] // KV_TILE
    tq = Q_TILE
    n_units = m_ref.shape[0]

    def keys_of(j):
        if isinstance(j, int):
            return slice(j * KV_TILE, (j + 1) * KV_TILE)
        return pl.ds(pl.multiple_of(j * KV_TILE, KV_TILE), KV_TILE)

    def score_stage(j, u):
        t, head = divmod(u, N_Q_HEADS)
        g = head // Q_PER_KV
        kch = k_ref[0, keys_of(j), g * HEAD_DIM:(g + 1) * HEAD_DIM]
        qt = qt_ref[0, head * HEAD_DIM:(head + 1) * HEAD_DIM, t * tq:(t + 1) * tq]
        s = jnp.dot(kch, qt, preferred_element_type=_F32)
        s_ref[u % 2] = s
        cmax_ref[u % 2] = jnp.max(s, axis=0, keepdims=True)

    def value_stage(j, u):
        g = (u % N_Q_HEADS) // Q_PER_KV
        vtch = vt_ref[0, g * HEAD_DIM:(g + 1) * HEAD_DIM, keys_of(j)]
        v_aug = jnp.concatenate([vtch, jnp.ones((ONES_ROWS, KV_TILE), _BF16)], axis=0)
        m_prev = m_ref[u]
        m_new = jnp.maximum(m_prev, cmax_ref[u % 2])
        alpha = jnp.exp2(m_prev - m_new)
        p = jnp.exp2(s_ref[u % 2] - m_new).astype(_BF16)
        acc_ref[u] = alpha * acc_ref[u] + jnp.dot(v_aug, p, preferred_element_type=_F32)
        m_ref[u] = m_new

    def chunk(j, last):
        for u in range(n_units):
            if u + 1 < n_units:
                score_stage(j, u + 1)
            elif not last:
                score_stage(j + 1, 0)
            value_stage(j, u)

    m_ref[...] = jnp.full(m_ref.shape, -jnp.inf, _F32)
    acc_ref[...] = jnp.zeros(acc_ref.shape, _F32)
    score_stage(0, 0)

    def body(j, carry):
        chunk(j, last=False)
        return carry

    lax.fori_loop(0, n_chunks - 1, body, 0, unroll=2)
    chunk(n_chunks - 1, last=True)
    for u in range(n_units):
        t, head = divmod(u, N_Q_HEADS)
        acc = acc_ref[u]
        out_t = acc[:HEAD_DIM] / acc[HEAD_DIM:HEAD_DIM + 1]
        attn_ref[t * tq:(t + 1) * tq, head * HEAD_DIM:(head + 1) * HEAD_DIM] = out_t.T.astype(_BF16)

    mixed = jnp.concatenate([attn_ref[...], conv_ref[0]], axis=-1)
    y = _mm(mixed, wo_ref[...])
    o_ref[0] = x_ref[0] + mod_ref[2:3, :] * y
    wup_bf16_ref[...] = wup_f32_ref[...].astype(_BF16)
    wdown_bf16_ref[...] = wdown_f32_ref[...].astype(_BF16)


def _attn_out(qt, k_all, vt_all, conv, x, mods, w_out, w_up, w_down, ffn_layer):
    b, n, d = x.shape
    n_keys = k_all.shape[1]
    assert n_keys % KV_TILE == 0
    tq = Q_TILE
    rows = Q_TILES_PER_STEP * tq
    assert n % rows == 0
    n_units = Q_TILES_PER_STEP * N_Q_HEADS
    grid = (b, n // rows)
    per_batch = lambda rows_, width: pl.BlockSpec((1, rows_, width), lambda bi, i: (bi, 0, 0))
    tile = lambda width: pl.BlockSpec((1, rows, width), lambda bi, i: (bi, i, 0))
    up_in, up_out, up_shape = _cast_plan(w_up, ffn_layer, grid)
    down_in, down_out, down_shape = _cast_plan(w_down, ffn_layer, grid)
    return pl.pallas_call(
        _attn_kernel,
        grid=grid,
        in_specs=[
            pl.BlockSpec((1, ATTN_WIDTH, rows), lambda bi, i: (bi, 0, i)),
            per_batch(n_keys, KV_WIDTH), per_batch(KV_WIDTH, n_keys),
            tile(CONV_WIDTH),
            tile(d),
            _mod_spec(d, 0, lambda bi, i: bi),
            _layer_spec(w_out, 0),
            up_in, down_in,
        ],
        out_specs=[tile(d), up_out, down_out],
        out_shape=[jax.ShapeDtypeStruct((b, n, d), _F32), up_shape, down_shape],
        scratch_shapes=[
            pltpu.VMEM((2, KV_TILE, tq), _F32),
            pltpu.VMEM((2, 1, tq), _F32),
            pltpu.VMEM((n_units, 1, tq), _F32),
            pltpu.VMEM((n_units, HEAD_DIM + ONES_ROWS, tq), _F32),
            pltpu.VMEM((rows, ATTN_WIDTH), _BF16),
        ],
        compiler_params=_compiler_params(2),
        name="attn_out",
    )(qt, k_all, vt_all, conv, x, mods, w_out, w_up, w_down)


def _ffn_kernel(xp_ref, x_ref, xn_ref, mod_ref, gain_ref, wup_ref, cwb_ref, wd_ref, *rest, layer):
    if len(rest) == 2:
        o_ref, h_ref = rest
    else:
        wup_f32_ref, wdown_f32_ref, o_ref, wup_bf16_ref, wdown_bf16_ref, h_ref = rest
        wup_bf16_ref[...] = wup_f32_ref[...].astype(_BF16)
        wdown_bf16_ref[...] = wdown_f32_ref[...].astype(_BF16)
    tile = x_ref.shape[1]
    gain, shift, scale = _layer_row(gain_ref, layer), mod_ref[3:4, :], mod_ref[4:5, :]
    x = x_ref[0]
    prev_ok, next_ok = _edge_flags()
    f_main = _norm_modulate(x, gain, shift, scale)
    f_prev = _norm_modulate(xp_ref[0], gain, shift, scale) * prev_ok
    f_next = _norm_modulate(xn_ref[0], gain, shift, scale) * next_ok
    f_ext = jnp.concatenate([f_prev, f_main, f_next], axis=0).astype(_BF16)
    f_mb = f_main.astype(_BF16)
    n_chunks = D_FF // FF_CHUNK

    def up(c):
        cols = slice(c * FF_CHUNK, (c + 1) * FF_CHUNK)
        val_cols = slice(D_FF + c * FF_CHUNK, D_FF + (c + 1) * FF_CHUNK)
        return _mm(f_ext, wup_ref[:, cols]), _mm(f_mb, wup_ref[:, val_cols])

    def act(c, gate, val):
        cols = slice(c * FF_CHUNK, (c + 1) * FF_CHUNK)
        cwb = cwb_ref[:, cols]
        conv = cwb[0:1] * _shift_rows(gate, 1) + cwb[1:2] * gate + cwb[2:3] * _shift_rows(gate, -1)
        h_ref[:, cols] = (_silu(conv[HALO:HALO + tile] + cwb[3:4]) * val).astype(_BF16)

    nxt = up(0)
    for c in range(n_chunks):
        cur = nxt
        if c + 1 < n_chunks:
            nxt = up(c + 1)
        act(c, *cur)
    y = _mm(h_ref[...], wd_ref[...])
    o_ref[0] = x + mod_ref[5:6, :] * y


def _ffn(x, mods, gains, w_up_bf16, conv_wb, w_down_bf16, layer, next_weights=None):
    b, n, d = x.shape
    tile = TOKEN_TILE
    grid = (b, n // tile)
    prev_spec, main_spec, next_spec = _halo_specs(tile, HALO, n, d)
    whole = lambda a: _resident(a.shape, lambda bi, i: (0, 0))
    in_specs = [
        prev_spec, main_spec, next_spec,
        _mod_spec(d, layer, lambda bi, i: bi),
        _layer_spec(gains, layer),
        whole(w_up_bf16), _layer_spec(conv_wb, layer), whole(w_down_bf16),
    ]
    args = [x, x, x, mods, gains, w_up_bf16, conv_wb, w_down_bf16]
    out_specs, out_shape = [main_spec], [jax.ShapeDtypeStruct((b, n, d), _F32)]
    if next_weights is not None:
        for w in next_weights:
            cast_in, cast_out, cast_shape = _cast_plan(w, layer + 1, grid)
            in_specs.append(cast_in)
            out_specs.append(cast_out)
            out_shape.append(cast_shape)
        args += list(next_weights)
    return pl.pallas_call(
        functools.partial(_ffn_kernel, layer=layer),
        grid=grid,
        in_specs=in_specs,
        out_specs=out_specs,
        out_shape=out_shape,
        scratch_shapes=[pltpu.VMEM((tile, D_FF), _BF16)],
        compiler_params=_compiler_params(2),
        name="ffn",
    )(*args)


def _pool_kernel(xp_ref, x_ref, xn_ref, mod_ref, gain_ref, pw_ref, ps_ref, o_ref, *, layer):
    tile = x_ref.shape[1]
    halo = xp_ref.shape[1]
    n_total = tile * pl.num_programs(1)
    gain, shift, scale = _layer_row(gain_ref, layer), mod_ref[0:1, :], mod_ref[1:2, :]
    x = x_ref[0]
    prev_ok, next_ok = _edge_flags()
    a_prev = _norm_modulate(xp_ref[0], gain, shift, scale) * prev_ok
    a_main = _norm_modulate(x, gain, shift, scale)
    a_next = _norm_modulate(xn_ref[0], gain, shift, scale) * next_ok
    a_ext = jnp.concatenate([a_prev, a_main, a_next], axis=0)
    edge = POOL_HALO // 2
    assert edge >= max(POOL_WINDOWS) // 2 and tile > 2 * edge
    rows = lax.broadcasted_iota(jnp.int32, (edge, 1), 0)
    t_top = pl.program_id(1) * tile + rows
    t_bot = t_top + (tile - edge)

    def inv_count(t, w):
        return 1.0 / (jnp.minimum(t + w - w // 2, n_total) - jnp.maximum(t - w // 2, 0)).astype(_F32)

    mixed = []
    for gi, w in enumerate(POOL_WINDOWS):
        a = a_ext[:, gi * POOL_GROUP:(gi + 1) * POOL_GROUP]
        s = a
        k = 1
        while 2 * k < w:
            s = s + _shift_rows(s, -k)
            k *= 2
        s = _shift_rows(s, k) + s
        s, a = s[halo:halo + tile], a[halo:halo + tile]
        pooled = jnp.concatenate([s[:edge] * inv_count(t_top, w) - a[:edge],
                                  s[edge:tile - edge] * (1.0 / w) - a[edge:tile - edge],
                                  s[tile - edge:] * inv_count(t_bot, w) - a[tile - edge:]], axis=0)
        mixed.append(_mm(pooled.astype(_BF16), pw_ref[gi]))
    o_ref[0] = x + jnp.concatenate(mixed, axis=-1) * (ps_ref[...] * mod_ref[2:3, :])


def _pool(x, mods, gains, pool_w, pool_scale, layer):
    b, n, d = x.shape
    tile = POOL_TILE
    prev_spec, main_spec, next_spec = _halo_specs(tile, POOL_HALO, n, d)
    return pl.pallas_call(
        functools.partial(_pool_kernel, layer=layer),
        grid=(b, n // tile),
        in_specs=[
            prev_spec, main_spec, next_spec,
            _mod_spec(d, layer, lambda bi, i: bi),
            _layer_spec(gains, layer),
            _layer_spec(pool_w, layer // 2),
            _layer_spec(pool_scale, layer // 2),
        ],
        out_specs=main_spec,
        out_shape=jax.ShapeDtypeStruct((b, n, d), _F32),
        compiler_params=_compiler_params(2),
        name="pool",
    )(x, x, x, mods, gains, pool_w, pool_scale)


def kernel(x, c, ctx, c_ctx, ada_w, ada_b, mix_norm, ffn_norm, even_w_in, even_q_gain, even_k_gain, even_conv_w,
           even_w_out, odd_pool_w, odd_pool_scale, ffn_w_up, ffn_conv_w, ffn_conv_b, ffn_w_down):
    b, n, d = x.shape
    depth = ada_w.shape[0]
    lc = ctx.shape[1]
    assert depth == 2 and d == D_MODEL and b + 1 == N_COND
    assert n % TOKEN_TILE == 0 and n % Q_TILE == 0 and n % WIDE_TILE == 0 and n % POOL_TILE == 0
    assert even_q_gain.shape[0] == even_k_gain.shape[0] == odd_pool_scale.shape[0] == 1

    cond = jnp.concatenate([c, c_ctx[None, :], jnp.zeros((SUBLANES - b - 1, d), _F32)], axis=0)
    mods = _adaln(cond.T, ada_w, ada_b).reshape(depth, SUBLANES, N_MOD, d)

    w_in, w_out, pool_w, w_up, w_down = even_w_in, even_w_out, odd_pool_w, ffn_w_up, ffn_w_down
    conv_wb = jnp.concatenate([ffn_conv_w, ffn_conv_b[:, None, :],
                               jnp.zeros((depth, SUBLANES - ffn_conv_w.shape[1] - 1, D_FF), _F32)], axis=1)

    qt, k_all, vt_all, conv = _in_proj(x, mods, mix_norm, w_in, even_q_gain, even_k_gain, even_conv_w, n + lc)
    k_all, vt_all = _ctx_kv(ctx, mods, mix_norm, w_in, even_k_gain, k_all, vt_all, n)
    x, w_up0, w_down0 = _attn_out(qt, k_all, vt_all, conv, x, mods, w_out, w_up, w_down, EVEN_LAYER)
    x, w_up1, w_down1 = _ffn(x, mods, ffn_norm, w_up0, conv_wb, w_down0, EVEN_LAYER, next_weights=(w_up, w_down))

    x = _pool(x, mods, mix_norm, pool_w, odd_pool_scale, ODD_LAYER)
    (x,) = _ffn(x, mods, ffn_norm, w_up1, conv_wb, w_down1, ODD_LAYER)
    return x
```

```python
import functools

import jax
import jax.numpy as jnp
import numpy as np
from jax import lax
from jax.experimental import pallas as pl
from jax.experimental.pallas import tpu as pltpu

D_MODEL = 1024
GRID_W = 64
HEAD_DIM = 128
N_Q_HEADS = 4
N_KV_HEADS = 2
Q_PER_KV = N_Q_HEADS // N_KV_HEADS
ATTN_WIDTH = N_Q_HEADS * HEAD_DIM
KV_WIDTH = N_KV_HEADS * HEAD_DIM
CONV_WIDTH = D_MODEL - ATTN_WIDTH
IN_PROJ_WIDTH = ATTN_WIDTH + 2 * KV_WIDTH + 3 * CONV_WIDTH
ROPE_THETA = 10000.0
ROPE_FREQS = HEAD_DIM // 4
SCORE_SCALE_LOG2 = HEAD_DIM ** -0.5 * 1.4426950408889634
POOL_WINDOWS = (2, 4, 8, 16)
POOL_GROUP = D_MODEL // len(POOL_WINDOWS)
D_FF = 2816
N_MOD = 6
N_COND = 3
EVEN_LAYER, ODD_LAYER = 0, 1
EPS = 1e-6

SUBLANES = 8
VMEM_LIMIT_BYTES = 56 * 1024 * 1024

HALO = SUBLANES
POOL_HALO = 2 * SUBLANES
TOKEN_TILE = 1024
WIDE_TILE = 1024
POOL_TILE = 2048
Q_TILE = 512
Q_TILES_PER_STEP = 2
KV_TILE = 768
ONES_ROWS = 16
FF_CHUNK = 256
ADALN_COLS = 1536

_BF16 = jnp.bfloat16
_F32 = jnp.float32


def _compiler_params(n_axes):
    return pltpu.CompilerParams(dimension_semantics=("arbitrary",) * n_axes,
                                vmem_limit_bytes=VMEM_LIMIT_BYTES)


def _resident(shape, index_map):
    return pl.BlockSpec(shape, index_map, pipeline_mode=pl.Buffered(1))


def _layer_spec(arr, layer):
    if arr.ndim == 2:
        return _resident(arr.shape, lambda *_: (0, 0))
    zeros = (0,) * (arr.ndim - 1)
    return _resident((None,) + arr.shape[1:], lambda *_: (layer,) + zeros)


def _layer_row(ref, layer):
    return ref[layer:layer + 1, :]


def _mod_spec(d, layer, row_of):
    return pl.BlockSpec((None, None, N_MOD, d), lambda *ids: (layer, row_of(*ids), 0, 0))


def _cast_plan(w, layer, grid):
    rows, cols = w.shape[1:]
    n_steps = grid[0] * grid[1]
    block, shared = rows // n_steps, 1
    while block % (2 * SUBLANES):
        block, shared = 2 * block, 2 * shared
    assert rows % block == 0 and n_steps * block == rows * shared
    block_of = lambda b, i: (b * grid[1] + i) // shared
    return (pl.BlockSpec((None, block, cols), lambda b, i: (layer, block_of(b, i), 0)),
            pl.BlockSpec((block, cols), lambda b, i: (block_of(b, i), 0)),
            jax.ShapeDtypeStruct((rows, cols), _BF16))


def _norm_modulate(x, gain, shift, scale):
    y = x * lax.rsqrt(jnp.mean(x * x, axis=-1, keepdims=True) + EPS)
    return y * (gain * (1.0 + scale)) + shift


def _mm(a, w):
    return lax.dot_general(a, w, (((1,), (0,)), ((), ())), preferred_element_type=_F32)


def _silu(x):
    h = 0.5 * x
    return h + h * jnp.tanh(h)


def _shift_rows(u, k):
    return pltpu.roll(u, k % u.shape[0], axis=0)


def _adaln_kernel(ct_ref, w_ref, b_ref, o_ref):
    s = _silu(ct_ref[...])
    w = w_ref[0]
    rows = [jnp.sum(s[:, r:r + 1] * w, axis=0, keepdims=True) + b_ref[0] for r in range(N_COND)]
    rows.append(jnp.zeros((SUBLANES - N_COND, w.shape[1]), _F32))
    o_ref[0] = jnp.concatenate(rows, axis=0)


def _adaln(cond_t, ada_w, ada_b):
    depth, d, width = ada_w.shape
    return pl.pallas_call(
        _adaln_kernel,
        grid=(depth, width // ADALN_COLS),
        in_specs=[
            pl.BlockSpec((d, SUBLANES), lambda l, j: (0, 0)),
            pl.BlockSpec((1, d, ADALN_COLS), lambda l, j: (l, 0, j)),
            pl.BlockSpec((1, 1, ADALN_COLS), lambda l, j: (l, 0, j)),
        ],
        out_specs=pl.BlockSpec((1, SUBLANES, ADALN_COLS), lambda l, j: (l, 0, j)),
        out_shape=jax.ShapeDtypeStruct((depth, SUBLANES, width), _F32),
        compiler_params=_compiler_params(2),
        name="adaln",
    )(cond_t, ada_w, ada_b.reshape(depth, 1, width))


def _head_rms(xh, gain):
    return xh * lax.rsqrt(jnp.mean(xh * xh, axis=-1, keepdims=True) + EPS) * gain


def _rope(xh, cos, sin_signed, first_half):
    partner = jnp.where(first_half, pltpu.roll(xh, HEAD_DIM - ROPE_FREQS, axis=1), pltpu.roll(xh, ROPE_FREQS, axis=1))
    return xh * cos + partner * sin_signed


def _rope_axis_tables(n_rows):
    axis_dim = HEAD_DIM // 2
    f32 = np.float32
    inv_freq = np.power(f32(ROPE_THETA), -np.arange(0, axis_dim, 2, dtype=f32) / f32(axis_dim)).astype(f32)
    ang_r = np.arange(n_rows, dtype=f32)[:, None] * inv_freq
    ang_c = np.arange(GRID_W, dtype=f32)[:, None] * inv_freq
    zr, zc = np.zeros((n_rows, axis_dim), f32), np.zeros((GRID_W, axis_dim), f32)
    return (np.concatenate([np.cos(ang_r), np.cos(ang_r), zr], axis=1),
            np.concatenate([-np.sin(ang_r), np.sin(ang_r), zr], axis=1),
            np.concatenate([zc, np.cos(ang_c), np.cos(ang_c)], axis=1),
            np.concatenate([zc, -np.sin(ang_c), np.sin(ang_c)], axis=1))


def _tile_table(row_ref, col_ref):
    col = col_ref[...]
    return jnp.concatenate([jnp.broadcast_to(row_ref[r:r + 1, :], col.shape) + col for r in range(row_ref.shape[0])],
                           axis=0)


def _halo_specs(tile, halo, n, d):
    per_tile = tile // halo
    n_halo_blocks = n // halo
    prev_spec = pl.BlockSpec((1, halo, d), lambda b, i: (b, jnp.maximum(i * per_tile - 1, 0), 0))
    main_spec = pl.BlockSpec((1, tile, d), lambda b, i: (b, i, 0))
    next_spec = pl.BlockSpec((1, halo, d), lambda b, i: (b, jnp.minimum((i + 1) * per_tile, n_halo_blocks - 1), 0))
    return prev_spec, main_spec, next_spec


def _edge_flags():
    i = pl.program_id(1)
    return (i > 0).astype(_F32), (i < pl.num_programs(1) - 1).astype(_F32)


def _in_proj_kernel(xp_ref, x_ref, xn_ref, mod_ref, gain_ref, w_ref, qg_ref, kg_ref, cw_ref,
                    rcos_ref, rsin_ref, ccos_ref, csin_ref, qt_ref, k_ref, vt_ref, conv_ref,
                    aext_ref, amain_ref, qkv_ref, cx_ref, gb_ref):
    tile = x_ref.shape[1]
    o_k = ATTN_WIDTH
    o_v = o_k + KV_WIDTH
    o_b = o_v + KV_WIDTH
    o_c = o_b + CONV_WIDTH
    every_step = pl.program_id(1) < pl.num_programs(1)

    @pl.when(every_step)
    def _():
        gain, shift, scale = _layer_row(gain_ref, EVEN_LAYER), mod_ref[0:1, :], mod_ref[1:2, :]
        prev_ok, next_ok = _edge_flags()
        a_main = _norm_modulate(x_ref[0], gain, shift, scale)
        a_prev = _norm_modulate(xp_ref[0], gain, shift, scale) * prev_ok
        a_next = _norm_modulate(xn_ref[0], gain, shift, scale) * next_ok
        aext_ref[...] = jnp.concatenate([a_prev, a_main, a_next], axis=0).astype(_BF16)
        a_mb = a_main.astype(_BF16)
        amain_ref[...] = a_mb
        qkv_ref[...] = _mm(a_mb, w_ref[:, 0:o_b])

    cx_ref[...] = _mm(aext_ref[...], w_ref[:, o_c:])
    gb_ref[...] = _mm(amain_ref[...], w_ref[:, o_b:o_c])
    cos, sin = _tile_table(rcos_ref, ccos_ref), _tile_table(rsin_ref, csin_ref)
    lane = lax.broadcasted_iota(jnp.int32, (1, HEAD_DIM), 1)
    first_half = (lane % (2 * ROPE_FREQS)) < ROPE_FREQS
    for h in range(N_Q_HEADS):
        sl = slice(h * HEAD_DIM, (h + 1) * HEAD_DIM)
        qh = _rope(_head_rms(qkv_ref[:, sl], qg_ref[...]), cos, sin, first_half)
        qt_ref[0, sl, :] = (qh * SCORE_SCALE_LOG2).T.astype(_BF16)
    for h in range(N_KV_HEADS):
        sl = slice(h * HEAD_DIM, (h + 1) * HEAD_DIM)
        kh = qkv_ref[:, o_k + h * HEAD_DIM:o_k + (h + 1) * HEAD_DIM]
        k_ref[0, :, sl] = _rope(_head_rms(kh, kg_ref[...]), cos, sin, first_half).astype(_BF16)
    vt_ref[0] = qkv_ref[:, o_v:o_b].T.astype(_BF16)

    @pl.when(every_step)
    def _():
        cx = cx_ref[...]
        u = cx[:, :CONV_WIDTH] * cx[:, CONV_WIDTH:]
        cw = cw_ref[...]
        conv = cw[0:1] * _shift_rows(u, 1) + cw[1:2] * u + cw[2:3] * _shift_rows(u, -1)
        conv_ref[0] = (gb_ref[...] * conv[HALO:HALO + tile]).astype(_BF16)


def _in_proj(x, mods, gains, w_in, q_gain, k_gain, conv_w, n_keys):
    b, n, d = x.shape
    tile = WIDE_TILE
    rows_per_tile = tile // GRID_W
    assert tile % GRID_W == 0
    prev_spec, main_spec, next_spec = _halo_specs(tile, HALO, n, d)
    rope = _rope_axis_tables(n // GRID_W)
    row_table = pl.BlockSpec((rows_per_tile, HEAD_DIM), lambda bi, i: (i, 0))
    col_table = _resident((GRID_W, HEAD_DIM), lambda bi, i: (0, 0))
    return pl.pallas_call(
        _in_proj_kernel,
        grid=(b, n // tile),
        in_specs=[
            prev_spec, main_spec, next_spec,
            _mod_spec(d, 0, lambda bi, i: bi),
            _layer_spec(gains, 0),
            _layer_spec(w_in, 0),
            _layer_spec(q_gain, 0), _layer_spec(k_gain, 0),
            _layer_spec(conv_w, 0),
            row_table, row_table, col_table, col_table,
        ],
        out_specs=[pl.BlockSpec((1, ATTN_WIDTH, tile), lambda bi, i: (bi, 0, i)),
                   pl.BlockSpec((1, tile, KV_WIDTH), lambda bi, i: (bi, i, 0)),
                   pl.BlockSpec((1, KV_WIDTH, tile), lambda bi, i: (bi, 0, i)),
                   pl.BlockSpec((1, tile, CONV_WIDTH), lambda bi, i: (bi, i, 0))],
        out_shape=[jax.ShapeDtypeStruct((b, ATTN_WIDTH, n), _BF16), jax.ShapeDtypeStruct((b, n_keys, KV_WIDTH), _BF16),
                   jax.ShapeDtypeStruct((b, KV_WIDTH, n_keys), _BF16), jax.ShapeDtypeStruct((b, n, CONV_WIDTH), _BF16)],
        scratch_shapes=[
            pltpu.VMEM((tile + 2 * HALO, d), _BF16),
            pltpu.VMEM((tile, d), _BF16),
            pltpu.VMEM((tile, ATTN_WIDTH + 2 * KV_WIDTH), _F32),
            pltpu.VMEM((tile + 2 * HALO, 2 * CONV_WIDTH), _F32),
            pltpu.VMEM((tile, CONV_WIDTH), _F32),
        ],
        compiler_params=_compiler_params(2),
        name="in_proj",
    )(x, x, x, mods, gains, w_in, q_gain, k_gain, conv_w, rope[0], rope[1], rope[2], rope[3])


def _ctx_kv_kernel(ctx_ref, mod_ref, gain_ref, w_ref, kg_ref, k_in_ref, vt_in_ref, k_ref, vt_ref):
    del k_in_ref, vt_in_ref
    a = _norm_modulate(ctx_ref[0], _layer_row(gain_ref, EVEN_LAYER), mod_ref[0:1, :], mod_ref[1:2, :]).astype(_BF16)
    kv = _mm(a, w_ref[...])
    for h in range(N_KV_HEADS):
        sl = slice(h * HEAD_DIM, (h + 1) * HEAD_DIM)
        k_ref[0, :, sl] = _head_rms(kv[:, sl], kg_ref[...]).astype(_BF16)
    vt_ref[0] = kv[:, KV_WIDTH:].T.astype(_BF16)


def _ctx_kv(ctx, mods, gains, w_in, k_gain, k_all, vt_all, n_latent):
    b, lc, d = ctx.shape
    assert n_latent % lc == 0 and 2 * KV_WIDTH == ATTN_WIDTH
    ctx_block = n_latent // lc
    return pl.pallas_call(
        _ctx_kv_kernel,
        grid=(b,),
        in_specs=[
            pl.BlockSpec((1, lc, d), lambda i: (i, 0, 0)),
            _mod_spec(d, 0, lambda i: b),
            _layer_spec(gains, 0),
            _resident((None, d, 2 * KV_WIDTH), lambda i: (0, 0, 1)),
            _layer_spec(k_gain, 0),
            pl.BlockSpec(memory_space=pl.ANY), pl.BlockSpec(memory_space=pl.ANY),
        ],
        out_specs=[pl.BlockSpec((1, lc, KV_WIDTH), lambda i: (i, ctx_block, 0)),
                   pl.BlockSpec((1, KV_WIDTH, lc), lambda i: (i, 0, ctx_block))],
        out_shape=[jax.ShapeDtypeStruct(k_all.shape, _BF16), jax.ShapeDtypeStruct(vt_all.shape, _BF16)],
        input_output_aliases={5: 0, 6: 1},
        compiler_params=_compiler_params(1),
        name="ctx_kv",
    )(ctx, mods, gains, w_in, k_gain, k_all, vt_all)


def _attn_kernel(qt_ref, k_ref, vt_ref, conv_ref, x_ref, mod_ref, wo_ref, wup_f32_ref, wdown_f32_ref,
                 o_ref, wup_bf16_ref, wdown_bf16_ref, s_ref, cmax_ref, m_ref, acc_ref, attn_ref):
    n_chunks = k_ref.shape[1] // KV_TILE
    tq = Q_TILE
    n_units = m_ref.shape[0]

    def keys_of(j):
        if isinstance(j, int):
            return slice(j * KV_TILE, (j + 1) * KV_TILE)
        return pl.ds(pl.multiple_of(j * KV_TILE, KV_TILE), KV_TILE)

    def score_stage(j, u):
        t, head = divmod(u, N_Q_HEADS)
        g = head // Q_PER_KV
        kch = k_ref[0, keys_of(j), g * HEAD_DIM:(g + 1) * HEAD_DIM]
        qt = qt_ref[0, head * HEAD_DIM:(head + 1) * HEAD_DIM, t * tq:(t + 1) * tq]
        s = jnp.dot(kch, qt, preferred_element_type=_F32)
        s_ref[u % 2] = s
        cmax_ref[u % 2] = jnp.max(s, axis=0, keepdims=True)

    def value_stage(j, u):
        g = (u % N_Q_HEADS) // Q_PER_KV
        vtch = vt_ref[0, g * HEAD_DIM:(g + 1) * HEAD_DIM, keys_of(j)]
        v_aug = jnp.concatenate([vtch, jnp.ones((ONES_ROWS, KV_TILE), _BF16)], axis=0)
        m_prev = m_ref[u]
        m_new = jnp.maximum(m_prev, cmax_ref[u % 2])
        alpha = jnp.exp2(m_prev - m_new)
        p = jnp.exp2(s_ref[u % 2] - m_new).astype(_BF16)
        acc_ref[u] = alpha * acc_ref[u] + jnp.dot(v_aug, p, preferred_element_type=_F32)
        m_ref[u] = m_new

    def chunk(j, last):
        for u in range(n_units):
            if u + 1 < n_units:
                score_stage(j, u + 1)
            elif not last:
                score_stage(j + 1, 0)
            value_stage(j, u)

    m_ref[...] = jnp.full(m_ref.shape, -jnp.inf, _F32)
    acc_ref[...] = jnp.zeros(acc_ref.shape, _F32)
    score_stage(0, 0)

    def body(j, carry):
        chunk(j, last=False)
        return carry

    lax.fori_loop(0, n_chunks - 1, body, 0, unroll=2)
    chunk(n_chunks - 1, last=True)
    for u in range(n_units):
        t, head = divmod(u, N_Q_HEADS)
        acc = acc_ref[u]
        out_t = acc[:HEAD_DIM] / acc[HEAD_DIM:HEAD_DIM + 1]
        attn_ref[t * tq:(t + 1) * tq, head * HEAD_DIM:(head + 1) * HEAD_DIM] = out_t.T.astype(_BF16)

    mixed = jnp.concatenate([attn_ref[...], conv_ref[0]], axis=-1)
    y = _mm(mixed, wo_ref[...])
    o_ref[0] = x_ref[0] + mod_ref[2:3, :] * y
    wup_bf16_ref[...] = wup_f32_ref[...].astype(_BF16)
    wdown_bf16_ref[...] = wdown_f32_ref[...].astype(_BF16)


def _attn_out(qt, k_all, vt_all, conv, x, mods, w_out, w_up, w_down, ffn_layer):
    b, n, d = x.shape
    n_keys = k_all.shape[1]
    assert n_keys % KV_TILE == 0
    tq = Q_TILE
    rows = Q_TILES_PER_STEP * tq
    assert n % rows == 0
    n_units = Q_TILES_PER_STEP * N_Q_HEADS
    grid = (b, n // rows)
    per_batch = lambda rows_, width: pl.BlockSpec((1, rows_, width), lambda bi, i: (bi, 0, 0))
    tile = lambda width: pl.BlockSpec((1, rows, width), lambda bi, i: (bi, i, 0))
    up_in, up_out, up_shape = _cast_plan(w_up, ffn_layer, grid)
    down_in, down_out, down_shape = _cast_plan(w_down, ffn_layer, grid)
    return pl.pallas_call(
        _attn_kernel,
        grid=grid,
        in_specs=[
            pl.BlockSpec((1, ATTN_WIDTH, rows), lambda bi, i: (bi, 0, i)),
            per_batch(n_keys, KV_WIDTH), per_batch(KV_WIDTH, n_keys),
            tile(CONV_WIDTH),
            tile(d),
            _mod_spec(d, 0, lambda bi, i: bi),
            _layer_spec(w_out, 0),
            up_in, down_in,
        ],
        out_specs=[tile(d), up_out, down_out],
        out_shape=[jax.ShapeDtypeStruct((b, n, d), _F32), up_shape, down_shape],
        scratch_shapes=[
            pltpu.VMEM((2, KV_TILE, tq), _F32),
            pltpu.VMEM((2, 1, tq), _F32),
            pltpu.VMEM((n_units, 1, tq), _F32),
            pltpu.VMEM((n_units, HEAD_DIM + ONES_ROWS, tq), _F32),
            pltpu.VMEM((rows, ATTN_WIDTH), _BF16),
        ],
        compiler_params=_compiler_params(2),
        name="attn_out",
    )(qt, k_all, vt_all, conv, x, mods, w_out, w_up, w_down)


def _ffn_kernel(xp_ref, x_ref, xn_ref, mod_ref, gain_ref, wup_ref, cwb_ref, wd_ref, *rest, layer):
    if len(rest) == 2:
        o_ref, h_ref = rest
    else:
        wup_f32_ref, wdown_f32_ref, o_ref, wup_bf16_ref, wdown_bf16_ref, h_ref = rest
        wup_bf16_ref[...] = wup_f32_ref[...].astype(_BF16)
        wdown_bf16_ref[...] = wdown_f32_ref[...].astype(_BF16)
    tile = x_ref.shape[1]
    gain, shift, scale = _layer_row(gain_ref, layer), mod_ref[3:4, :], mod_ref[4:5, :]
    x = x_ref[0]
    prev_ok, next_ok = _edge_flags()
    f_main = _norm_modulate(x, gain, shift, scale)
    f_prev = _norm_modulate(xp_ref[0], gain, shift, scale) * prev_ok
    f_next = _norm_modulate(xn_ref[0], gain, shift, scale) * next_ok
    f_ext = jnp.concatenate([f_prev, f_main, f_next], axis=0).astype(_BF16)
    f_mb = f_main.astype(_BF16)
    n_chunks = D_FF // FF_CHUNK

    def up(c):
        cols = slice(c * FF_CHUNK, (c + 1) * FF_CHUNK)
        val_cols = slice(D_FF + c * FF_CHUNK, D_FF + (c + 1) * FF_CHUNK)
        return _mm(f_ext, wup_ref[:, cols]), _mm(f_mb, wup_ref[:, val_cols])

    def act(c, gate, val):
        cols = slice(c * FF_CHUNK, (c + 1) * FF_CHUNK)
        cwb = cwb_ref[:, cols]
        conv = cwb[0:1] * _shift_rows(gate, 1) + cwb[1:2] * gate + cwb[2:3] * _shift_rows(gate, -1)
        h_ref[:, cols] = (_silu(conv[HALO:HALO + tile] + cwb[3:4]) * val).astype(_BF16)

    nxt = up(0)
    for c in range(n_chunks):
        cur = nxt
        if c + 1 < n_chunks:
            nxt = up(c + 1)
        act(c, *cur)
    y = _mm(h_ref[...], wd_ref[...])
    o_ref[0] = x + mod_ref[5:6, :] * y


def _ffn(x, mods, gains, w_up_bf16, conv_wb, w_down_bf16, layer, next_weights=None):
    b, n, d = x.shape
    tile = TOKEN_TILE
    grid = (b, n // tile)
    prev_spec, main_spec, next_spec = _halo_specs(tile, HALO, n, d)
    whole = lambda a: _resident(a.shape, lambda bi, i: (0, 0))
    in_specs = [
        prev_spec, main_spec, next_spec,
        _mod_spec(d, layer, lambda bi, i: bi),
        _layer_spec(gains, layer),
        whole(w_up_bf16), _layer_spec(conv_wb, layer), whole(w_down_bf16),
    ]
    args = [x, x, x, mods, gains, w_up_bf16, conv_wb, w_down_bf16]
    out_specs, out_shape = [main_spec], [jax.ShapeDtypeStruct((b, n, d), _F32)]
    if next_weights is not None:
        for w in next_weights:
            cast_in, cast_out, cast_shape = _cast_plan(w, layer + 1, grid)
            in_specs.append(cast_in)
            out_specs.append(cast_out)
            out_shape.append(cast_shape)
        args += list(next_weights)
    return pl.pallas_call(
        functools.partial(_ffn_kernel, layer=layer),
        grid=grid,
        in_specs=in_specs,
        out_specs=out_specs,
        out_shape=out_shape,
        scratch_shapes=[pltpu.VMEM((tile, D_FF), _BF16)],
        compiler_params=_compiler_params(2),
        name="ffn",
    )(*args)


def _pool_kernel(xp_ref, x_ref, xn_ref, mod_ref, gain_ref, pw_ref, ps_ref, o_ref, *, layer):
    tile = x_ref.shape[1]
    halo = xp_ref.shape[1]
    n_total = tile * pl.num_programs(1)
    gs = _layer_row(gain_ref, layer) * (1.0 + mod_ref[1:2, :])
    x = x_ref[0]
    prev_ok, next_ok = _edge_flags()
    rms = lambda v: v * lax.rsqrt(jnp.mean(v * v, axis=-1, keepdims=True) + EPS)
    a_ext = jnp.concatenate([rms(xp_ref[0]) * prev_ok, rms(x), rms(xn_ref[0]) * next_ok], axis=0)
    edge = POOL_HALO // 2
    assert edge >= max(POOL_WINDOWS) // 2 and tile > 2 * edge
    rows = lax.broadcasted_iota(jnp.int32, (edge, 1), 0)
    t_top = pl.program_id(1) * tile + rows
    t_bot = t_top + (tile - edge)

    def inv_count(t, w):
        return 1.0 / (jnp.minimum(t + w - w // 2, n_total) - jnp.maximum(t - w // 2, 0)).astype(_F32)

    mixed = []
    for gi, w in enumerate(POOL_WINDOWS):
        a = a_ext[:, gi * POOL_GROUP:(gi + 1) * POOL_GROUP]
        s = a
        k = 1
        while 2 * k < w:
            s = s + _shift_rows(s, -k)
            k *= 2
        s = _shift_rows(s, k) + s
        s, a = s[halo:halo + tile], a[halo:halo + tile]
        pooled = jnp.concatenate([s[:edge] * inv_count(t_top, w) - a[:edge],
                                  s[edge:tile - edge] * (1.0 / w) - a[edge:tile - edge],
                                  s[tile - edge:] * inv_count(t_bot, w) - a[tile - edge:]], axis=0)
        pooled = pooled * gs[:, gi * POOL_GROUP:(gi + 1) * POOL_GROUP]
        mixed.append(_mm(pooled.astype(_BF16), pw_ref[gi]))
    o_ref[0] = x + jnp.concatenate(mixed, axis=-1) * (ps_ref[...] * mod_ref[2:3, :])


def _pool(x, mods, gains, pool_w, pool_scale, layer):
    b, n, d = x.shape
    tile = POOL_TILE
    prev_spec, main_spec, next_spec = _halo_specs(tile, POOL_HALO, n, d)
    return pl.pallas_call(
        functools.partial(_pool_kernel, layer=layer),
        grid=(b, n // tile),
        in_specs=[
            prev_spec, main_spec, next_spec,
            _mod_spec(d, layer, lambda bi, i: bi),
            _layer_spec(gains, layer),
            _layer_spec(pool_w, layer // 2),
            _layer_spec(pool_scale, layer // 2),
        ],
        out_specs=main_spec,
        out_shape=jax.ShapeDtypeStruct((b, n, d), _F32),
        compiler_params=_compiler_params(2),
        name="pool",
    )(x, x, x, mods, gains, pool_w, pool_scale)


def kernel(x, c, ctx, c_ctx, ada_w, ada_b, mix_norm, ffn_norm, even_w_in, even_q_gain, even_k_gain, even_conv_w,
           even_w_out, odd_pool_w, odd_pool_scale, ffn_w_up, ffn_conv_w, ffn_conv_b, ffn_w_down):
    b, n, d = x.shape
    depth = ada_w.shape[0]
    lc = ctx.shape[1]
    assert depth == 2 and d == D_MODEL and b + 1 == N_COND
    assert n % TOKEN_TILE == 0 and n % Q_TILE == 0 and n % WIDE_TILE == 0 and n % POOL_TILE == 0
    assert even_q_gain.shape[0] == even_k_gain.shape[0] == odd_pool_scale.shape[0] == 1

    cond = jnp.concatenate([c, c_ctx[None, :], jnp.zeros((SUBLANES - b - 1, d), _F32)], axis=0)
    mods = _adaln(cond.T, ada_w, ada_b).reshape(depth, SUBLANES, N_MOD, d)

    w_in, w_out, pool_w, w_up, w_down = even_w_in, even_w_out, odd_pool_w, ffn_w_up, ffn_w_down
    conv_wb = jnp.concatenate([ffn_conv_w, ffn_conv_b[:, None, :],
                               jnp.zeros((depth, SUBLANES - ffn_conv_w.shape[1] - 1, D_FF), _F32)], axis=1)

    qt, k_all, vt_all, conv = _in_proj(x, mods, mix_norm, w_in, even_q_gain, even_k_gain, even_conv_w, n + lc)
    k_all, vt_all = _ctx_kv(ctx, mods, mix_norm, w_in, even_k_gain, k_all, vt_all, n)
    x, w_up0, w_down0 = _attn_out(qt, k_all, vt_all, conv, x, mods, w_out, w_up, w_down, EVEN_LAYER)
    x, w_up1, w_down1 = _ffn(x, mods, ffn_norm, w_up0, conv_wb, w_down0, EVEN_LAYER, next_weights=(w_up, w_down))

    x = _pool(x, mods, mix_norm, pool_w, odd_pool_scale, ODD_LAYER)
    (x,) = _ffn(x, mods, ffn_norm, w_up1, conv_wb, w_down1, ODD_LAYER)
    return x
```
